```python
import math
import jax, jax.numpy as jnp
from jax import lax
import numpy as np


D_MODEL = 1024
BATCH = 8
SEQ = 2048
DEPTH = 1

D_MIX = D_MODEL
DIFF_WIDTH = D_MIX // 2
HGRN_WIDTH = D_MIX - DIFF_WIDTH
N_DIFF_HEADS = 4
DIFF_V_DIM = DIFF_WIDTH // N_DIFF_HEADS
DIFF_HEAD_DIM = DIFF_V_DIM // 2
N_HGRN_HEADS = 4
HGRN_EXPAND = 128
HGRN_V_DIM = HGRN_WIDTH // N_HGRN_HEADS
Q_BLOCK = 128
HGRN_CHUNK = 64
ROPE_THETA = 10000.0
N_GROUPS = 4
EXPERTS_PER_GROUP = 8
TOP_K = 2
D_EXPERT = D_MODEL // 2
RMS_EPS = 1e-6
SUBLN_EPS = 1e-5
MAX_POS_OFFSET = 512

PROJ_SIZES = (
    2 * N_DIFF_HEADS * DIFF_HEAD_DIM,
    2 * N_DIFF_HEADS * DIFF_HEAD_DIM,
    N_DIFF_HEADS * DIFF_V_DIM,
    N_HGRN_HEADS * HGRN_EXPAND,
    N_HGRN_HEADS * HGRN_EXPAND,
    N_HGRN_HEADS * HGRN_V_DIM,
    N_HGRN_HEADS * HGRN_V_DIM,
)
D_IN_PROJ = sum(PROJ_SIZES)

kernel_name = "hymba_diffattn_hgrn2_hiermoe_adaln"


def rms_norm(x, w, eps):
    xf = x.astype(jnp.float32)
    y = xf * lax.rsqrt(jnp.mean(xf * xf, axis=-1, keepdims=True) + eps)
    return (y * w.astype(jnp.float32)).astype(x.dtype)


def apply_rope(x, positions):
    d = x.shape[-1]
    half = d // 2
    inv_freq = ROPE_THETA ** (-jnp.arange(half, dtype=jnp.float32) / half)
    ang = positions.astype(jnp.float32)[..., None] * inv_freq
    cos = jnp.cos(ang)[:, :, None, :]
    sin = jnp.sin(ang)[:, :, None, :]
    xf = x.astype(jnp.float32)
    x1, x2 = xf[..., :half], xf[..., half:]
    out = jnp.concatenate([x1 * cos - x2 * sin, x2 * cos + x1 * sin], axis=-1)
    return out.astype(x.dtype)


def diff_attention(q, k, v, positions, lq1, lk1, lq2, lk2, subln_w, lambda_init):
    B, S = q.shape[0], q.shape[1]
    H, d = N_DIFF_HEADS, DIFF_HEAD_DIM
    q = apply_rope(q, positions)
    k = apply_rope(k, positions)
    lam = (jnp.exp(jnp.sum(lq1.astype(jnp.float32) * lk1.astype(jnp.float32)))
           - jnp.exp(jnp.sum(lq2.astype(jnp.float32) * lk2.astype(jnp.float32)))
           + lambda_init)
    qh = q.transpose(0, 2, 1, 3) * (d ** -0.5)
    kh = k.transpose(0, 2, 1, 3)
    vh = v.transpose(0, 2, 1, 3)
    nb = S // Q_BLOCK
    q_blocks = qh.reshape(B, 2 * H, nb, Q_BLOCK, d).transpose(2, 0, 1, 3, 4)
    key_idx = jnp.arange(S)

    def one_block(args):
        qb, bi = args
        s = jnp.einsum('bhqd,bhkd->bhqk', qb, kh).astype(jnp.float32)
        q_idx = bi * Q_BLOCK + jnp.arange(Q_BLOCK)
        mask = key_idx[None, :] <= q_idx[:, None]
        s = jnp.where(mask, s, -jnp.inf)
        p = jax.nn.softmax(s, axis=-1).reshape(B, H, 2, Q_BLOCK, S)
        a = p[:, :, 0] - lam * p[:, :, 1]
        return jnp.einsum('bhqk,bhkv->bhqv', a.astype(vh.dtype), vh)

    o = lax.map(one_block, (q_blocks, jnp.arange(nb)))
    o = o.transpose(1, 0, 3, 2, 4).reshape(B, S, H, 2 * d)
    o = rms_norm(o, subln_w, SUBLN_EPS) * (1.0 - lambda_init)
    return o.reshape(B, S, H * 2 * d)


def chunk_gla(q, k, v, logf):
    B, S, H, K = q.shape
    V = v.shape[-1]
    C = HGRN_CHUNK
    nc = S // C

    def to_chunks(t):
        return t.reshape(B, nc, C, H, t.shape[-1]).transpose(1, 0, 3, 2, 4)

    tri = jnp.tril(jnp.ones((C, C), dtype=bool))

    def step(state, inp):
        qb, kb, vb, gb = inp
        b = jnp.cumsum(gb, axis=2)
        diff = b[:, :, :, None, :] - b[:, :, None, :, :]
        dec = jnp.exp(jnp.where(tri[:, :, None], diff, -jnp.inf))
        attn = jnp.einsum('bhtsk,bhsk->bhts', qb[:, :, :, None, :] * dec, kb)
        o_intra = jnp.einsum('bhts,bhsv->bhtv', attn, vb)
        o_inter = jnp.einsum('bhtk,bhkv->bhtv', qb * jnp.exp(b), state)
        b_last = b[:, :, -1:, :]
        k_dec = kb * jnp.exp(b_last - b)
        new_state = (jnp.exp(b_last[:, :, 0, :])[..., None] * state
                     + jnp.einsum('bhsk,bhsv->bhkv', k_dec, vb))
        return new_state, o_intra + o_inter

    state0 = jnp.zeros((B, H, K, V), jnp.float32)
    _, o = lax.scan(step, state0, (to_chunks(q), to_chunks(k), to_chunks(v), to_chunks(logf)))
    return o.transpose(1, 0, 3, 2, 4).reshape(B, S, H, V)


def hgrn2(q, f, i, g, lb, gnorm_w):
    B, S = q.shape[0], q.shape[1]
    qf = jax.nn.silu(q.astype(jnp.float32))
    forget = lb + (1.0 - lb) * jax.nn.sigmoid(f.astype(jnp.float32))
    k_in = 1.0 - forget
    logf = jnp.log(forget)
    o = chunk_gla(qf, k_in, i.astype(jnp.float32), logf)
    o = rms_norm(o, gnorm_w, RMS_EPS) * jax.nn.silu(g.astype(jnp.float32))
    return o.reshape(B, S, N_HGRN_HEADS * HGRN_V_DIM).astype(q.dtype)


def token_mix(h, positions, w_in, lq1, lk1, lq2, lk2, subln_w, lb, gnorm_w, w_out, layer_idx):
    B, S, _ = h.shape
    proj = h @ w_in
    parts = []
    start = 0
    for size in PROJ_SIZES:
        parts.append(proj[..., start:start + size])
        start += size
    dq, dk, dv, hq, hf, hi, hg = parts
    lambda_init = 0.8 - 0.6 * math.exp(-0.3 * layer_idx)
    a = diff_attention(
        dq.reshape(B, S, 2 * N_DIFF_HEADS, DIFF_HEAD_DIM),
        dk.reshape(B, S, 2 * N_DIFF_HEADS, DIFF_HEAD_DIM),
        dv.reshape(B, S, N_DIFF_HEADS, DIFF_V_DIM),
        positions, lq1, lk1, lq2, lk2, subln_w, lambda_init)
    r = hgrn2(
        hq.reshape(B, S, N_HGRN_HEADS, HGRN_EXPAND),
        hf.reshape(B, S, N_HGRN_HEADS, HGRN_EXPAND),
        hi.reshape(B, S, N_HGRN_HEADS, HGRN_V_DIM),
        hg.reshape(B, S, N_HGRN_HEADS, HGRN_V_DIM),
        lb.reshape(N_HGRN_HEADS, HGRN_EXPAND), gnorm_w)
    mixed = jnp.concatenate([a.astype(h.dtype), r.astype(h.dtype)], axis=-1)
    return mixed @ w_out


def hier_moe(h, w_gr, b_gr, w_er, b_er, w_gate, w_up, w_down):
    B, S, D = h.shape
    N = B * S
    t = h.reshape(N, D)
    g_logits = (t @ w_gr).astype(jnp.float32) + b_gr.astype(jnp.float32)
    g_prob = jax.nn.softmax(g_logits, axis=-1)
    g_sel = jnp.argmax(g_logits, axis=-1)
    p_g = jnp.take_along_axis(g_prob, g_sel[:, None], axis=-1)[:, 0]
    e_logits = ((t @ w_er).astype(jnp.float32).reshape(N, N_GROUPS, EXPERTS_PER_GROUP)
                + b_er.astype(jnp.float32))
    e_in = jnp.take_along_axis(e_logits, g_sel[:, None, None], axis=1)[:, 0]
    top_v, top_i = lax.top_k(e_in, TOP_K)
    weight = p_g[:, None] * jax.nn.softmax(top_v, axis=-1)
    flat_idx = g_sel[:, None] * EXPERTS_PER_GROUP + top_i
    comb = jnp.einsum('nk,nke->ne', weight,
                      jax.nn.one_hot(flat_idx, N_GROUPS * EXPERTS_PER_GROUP, dtype=jnp.float32))
    comb = comb.reshape(N, N_GROUPS, EXPERTS_PER_GROUP).astype(t.dtype)
    y = jnp.zeros((N, D), jnp.float32)
    for gi in range(N_GROUPS):
        a = jnp.einsum('nd,edf->nef', t, w_gate[gi])
        u = jnp.einsum('nd,edf->nef', t, w_up[gi])
        hid = jax.nn.silu(a) * u * comb[:, gi, :, None]
        y = y + jnp.einsum('nef,efd->nd', hid, w_down[gi]).astype(jnp.float32)
    return y.astype(h.dtype).reshape(B, S, D)


def setup_inputs(seed: int = 0) -> dict:
    key = jax.random.key(seed)
    ks = jax.random.split(key, 24)
    f32 = jnp.float32

    def nrm(k, shape, scale):
        return jax.random.normal(k, shape, f32) * scale

    G, E, F = N_GROUPS, EXPERTS_PER_GROUP, D_EXPERT
    positions = (jnp.arange(SEQ, dtype=jnp.int32)[None, :]
                 + jax.random.randint(ks[2], (BATCH, 1), 0, MAX_POS_OFFSET, dtype=jnp.int32))
    return {
        "x": nrm(ks[0], (BATCH, SEQ, D_MODEL), 1.0),
        "c": nrm(ks[1], (BATCH, D_MODEL), 1.0),
        "positions": positions,
        "w_ada": nrm(ks[3], (DEPTH, D_MODEL, 6 * D_MODEL), 0.5 * D_MODEL ** -0.5),
        "b_ada": nrm(ks[4], (DEPTH, 6 * D_MODEL), 0.02),
        "norm1_w": 1.0 + nrm(ks[5], (DEPTH, D_MODEL), 0.02),
        "w_in": nrm(ks[6], (DEPTH, D_MODEL, D_IN_PROJ), D_MODEL ** -0.5),
        "lambda_q1": nrm(ks[7], (DEPTH, DIFF_HEAD_DIM), 0.1),
        "lambda_k1": nrm(ks[8], (DEPTH, DIFF_HEAD_DIM), 0.1),
        "lambda_q2": nrm(ks[9], (DEPTH, DIFF_HEAD_DIM), 0.1),
        "lambda_k2": nrm(ks[10], (DEPTH, DIFF_HEAD_DIM), 0.1),
        "subln_w": 1.0 + nrm(ks[11], (DEPTH, DIFF_V_DIM), 0.02),
        "hgrn_lb_logits": nrm(ks[12], (DEPTH + 1, N_HGRN_HEADS * HGRN_EXPAND), 0.1),
        "gnorm_w": 1.0 + nrm(ks[13], (DEPTH, HGRN_V_DIM), 0.02),
        "w_out": nrm(ks[14], (DEPTH, D_MIX, D_MODEL), D_MIX ** -0.5),
        "norm2_w": 1.0 + nrm(ks[15], (DEPTH, D_MODEL), 0.02),
        "w_group_router": nrm(ks[16], (DEPTH, D_MODEL, G), D_MODEL ** -0.5),
        "b_group_router": nrm(ks[17], (DEPTH, G), 0.01),
        "w_expert_router": nrm(ks[18], (DEPTH, D_MODEL, G * E), D_MODEL ** -0.5),
        "b_expert_router": nrm(ks[19], (DEPTH, G, E), 0.01),
        "w_gate": nrm(ks[20], (DEPTH, G, E, D_MODEL, F), D_MODEL ** -0.5),
        "w_up": nrm(ks[21], (DEPTH, G, E, D_MODEL, F), D_MODEL ** -0.5),
        "w_down": nrm(ks[22], (DEPTH, G, E, F, D_MODEL), F ** -0.5),
        "final_norm_w": 1.0 + nrm(ks[23], (D_MODEL,), 0.02),
    }


def reference(x, c, positions, w_ada, b_ada, norm1_w, w_in, lambda_q1, lambda_k1, lambda_q2,
              lambda_k2, subln_w, hgrn_lb_logits, gnorm_w, w_out, norm2_w, w_group_router,
              b_group_router, w_expert_router, b_expert_router, w_gate, w_up, w_down,
              final_norm_w):
    lb_all = jnp.cumsum(jax.nn.softmax(hgrn_lb_logits.astype(jnp.float32), axis=0), axis=0)
    c_act = jax.nn.silu(c)
    for l in range(DEPTH):
        mod = c_act @ w_ada[l] + b_ada[l]
        sh1, sc1, g1, sh2, sc2, g2 = jnp.split(mod, 6, axis=-1)
        h = rms_norm(x, norm1_w[l], RMS_EPS) * (1.0 + sc1[:, None, :]) + sh1[:, None, :]
        mix = token_mix(h, positions, w_in[l], lambda_q1[l], lambda_k1[l], lambda_q2[l],
                        lambda_k2[l], subln_w[l], lb_all[l], gnorm_w[l], w_out[l], l)
        x = x + g1[:, None, :] * mix
        h = rms_norm(x, norm2_w[l], RMS_EPS) * (1.0 + sc2[:, None, :]) + sh2[:, None, :]
        ffn = hier_moe(h, w_group_router[l], b_group_router[l], w_expert_router[l],
                       b_expert_router[l], w_gate[l], w_up[l], w_down[l])
        x = x + g2[:, None, :] * ffn
    return rms_norm(x, final_norm_w, RMS_EPS)
```

```python
import functools
import math

import jax
import jax.numpy as jnp
from jax import lax
from jax.experimental import pallas as pl
from jax.experimental.pallas import tpu as pltpu

F32 = jnp.float32
BF16 = jnp.bfloat16
HIGHEST = lax.Precision.HIGHEST

N_DIFF_HEADS = 4
DIFF_HEAD_DIM = 64
DIFF_V_DIM = 2 * DIFF_HEAD_DIM
N_HGRN_HEADS = 4
HGRN_EXPAND = 128
HGRN_V_DIM = 128
ROPE_THETA = 10000.0
N_GROUPS = 4
EXPERTS_PER_GROUP = 8
N_EXPERTS = N_GROUPS * EXPERTS_PER_GROUP
RMS_EPS = 1e-6
SUBLN_EPS = 1e-5
SEG = 512
LANES = 128
ROUTER_LANE0 = N_GROUPS

ROW_TILE = 256
ATTN_TILE = 256
HGRN_CHUNK = 64
HGRN_SUB = 16
EXPERT_TILE = 256
VMEM_LIMIT = 48 * 1024 * 1024


def _params(sem, **kw):
    return pltpu.CompilerParams(dimension_semantics=sem, vmem_limit_bytes=VMEM_LIMIT, **kw)


def _silu(x):
    return x * jax.nn.sigmoid(x)


def _rms(x, eps):
    return x * lax.rsqrt(jnp.mean(x * x, axis=-1, keepdims=True) + eps)


def _mod_kernel(c_ref, w_ref, b_ref, o_ref):
    ca = _silu(c_ref[...])
    o_ref[...] = jnp.dot(ca, w_ref[...], preferred_element_type=F32, precision=HIGHEST) + b_ref[...]


def _mod(c, w_ada, b_ada):
    bsz, d = c.shape
    n_out = w_ada.shape[1]
    return pl.pallas_call(
        _mod_kernel,
        out_shape=jax.ShapeDtypeStruct((bsz, n_out), F32),
        grid=(n_out // d,),
        in_specs=[pl.BlockSpec((bsz, d), lambda j: (0, 0)),
                  pl.BlockSpec((d, d), lambda j: (0, j)),
                  pl.BlockSpec((1, d), lambda j: (0, j))],
        out_specs=pl.BlockSpec((bsz, d), lambda j: (0, j)),
        compiler_params=_params(("arbitrary",)),
        name="mod",
    )(c, w_ada, b_ada.reshape(1, n_out))


def _trig_kernel(a_ref, cos_ref, sin_ref):
    a = a_ref[...]
    cos_ref[...] = jnp.cos(a)
    sin_ref[...] = jnp.sin(a)


def _trig(ang):
    rows = ang.shape[0]
    blk = min(rows, 512)
    spec = pl.BlockSpec((blk, LANES), lambda i: (i, 0))
    return pl.pallas_call(
        _trig_kernel,
        out_shape=(jax.ShapeDtypeStruct(ang.shape, F32),) * 2,
        grid=(rows // blk,),
        in_specs=[spec],
        out_specs=(spec, spec),
        compiler_params=_params(("arbitrary",)),
        name="trig",
    )(ang)


def _inproj_kernel(x_ref, sc_ref, sh_ref, nw_ref, w_ref, cos_ref, sin_ref,
                   q_ref, k_ref, v_ref, hq_ref, hf_ref, hi_ref, hg_ref):
    h = _rms(x_ref[...], RMS_EPS) * nw_ref[...]
    h = h * (1.0 + sc_ref[...]) + sh_ref[...]
    hb = h.astype(BF16)
    cos = cos_ref[...]
    sin = sin_ref[...]
    lane = lax.broadcasted_iota(jnp.int32, cos.shape, 1)
    first_half = (lane % DIFF_HEAD_DIM) < (DIFF_HEAD_DIM // 2)

    def seg(i):
        return jnp.dot(hb, w_ref[:, i * SEG:(i + 1) * SEG], preferred_element_type=F32)

    def rope(p, scale):
        outs = []
        for c in range(SEG // LANES):
            pc = p[:, c * LANES:(c + 1) * LANES]
            partner = jnp.where(first_half, pltpu.roll(pc, LANES - DIFF_HEAD_DIM // 2, 1),
                                pltpu.roll(pc, DIFF_HEAD_DIM // 2, 1))
            outs.append((pc * cos + partner * sin) * scale)
        return jnp.concatenate(outs, axis=1)

    q_ref[...] = rope(seg(0), DIFF_HEAD_DIM ** -0.5).astype(BF16)
    k_ref[...] = rope(seg(1), 1.0).astype(BF16)
    v_ref[...] = seg(2).astype(BF16)
    hq_ref[...] = seg(3).astype(BF16)
    hf_ref[...] = seg(4)
    hi_ref[...] = seg(5).astype(BF16)
    hg_ref[...] = seg(6).astype(BF16)


def _inproj(x2, mod4, norm_w, w_in_bf, cos, sin, seq):
    n, d = x2.shape
    tm = min(ROW_TILE, seq)
    tpb = seq // tm
    row = lambda i: (i, 0)
    modspec = lambda j: pl.BlockSpec((None, None, 1, d), lambda i: (i // tpb, j, 0, 0))
    seg_spec = pl.BlockSpec((tm, SEG), row)
    bf = jax.ShapeDtypeStruct((n, SEG), BF16)
    return pl.pallas_call(
        _inproj_kernel,
        out_shape=(bf, bf, bf, bf, jax.ShapeDtypeStruct((n, SEG), F32), bf, bf),
        grid=(n // tm,),
        in_specs=[pl.BlockSpec((tm, d), row), modspec(1), modspec(0),
                  pl.BlockSpec((1, d), lambda i: (0, 0)),
                  pl.BlockSpec(w_in_bf.shape, lambda i: (0, 0)),
                  pl.BlockSpec((tm, LANES), row), pl.BlockSpec((tm, LANES), row)],
        out_specs=(seg_spec,) * 7,
        compiler_params=_params(("arbitrary",)),
        name="inproj",
    )(x2, mod4, mod4, norm_w.reshape(1, d), w_in_bf, cos, sin)


def _attn_kernel(q_ref, k_ref, v_ref, lq1_ref, lk1_ref, lq2_ref, lk2_ref, sw_ref, o_ref, *, lambda_init):
    tq = q_ref.shape[0]
    qi = pl.program_id(2)
    hd = DIFF_HEAD_DIM
    q = q_ref[...]
    qs = (q[:, :hd], q[:, hd:])
    nt = (((1,), (1,)), ((), ()))

    def step(j, carry, masked):
        kb = k_ref[pl.ds(pl.multiple_of(j * tq, tq), tq), :]
        vb = v_ref[pl.ds(pl.multiple_of(j * tq, tq), tq), :]
        out = []
        for mp in range(2):
            m, l, acc = carry[mp]
            s = lax.dot_general(qs[mp], kb[:, mp * hd:(mp + 1) * hd], nt, preferred_element_type=F32)
            if masked:
                r = lax.broadcasted_iota(jnp.int32, s.shape, 0)
                c = lax.broadcasted_iota(jnp.int32, s.shape, 1)
                s = jnp.where(c <= r, s, -jnp.inf)
            m_new = jnp.maximum(m, jnp.max(s, axis=-1, keepdims=True))
            alpha = jnp.exp(m - m_new)
            p = jnp.exp(s - m_new)
            l = alpha * l + jnp.sum(p, axis=-1, keepdims=True)
            acc = alpha * acc + jnp.dot(p.astype(BF16), vb, preferred_element_type=F32)
            out.append((m_new, l, acc))
        return tuple(out)

    init = tuple((jnp.full((tq, 1), -jnp.inf, F32), jnp.zeros((tq, 1), F32),
                  jnp.zeros((tq, DIFF_V_DIM), F32)) for _ in range(2))
    carry = lax.fori_loop(0, qi, lambda j, c: step(j, c, False), init)
    (_, l1, a1), (_, l2, a2) = step(qi, carry, True)

    lam = (jnp.exp(jnp.sum(lq1_ref[...] * lk1_ref[...], axis=-1, keepdims=True))
           - jnp.exp(jnp.sum(lq2_ref[...] * lk2_ref[...], axis=-1, keepdims=True)) + lambda_init)
    o = a1 / l1 - lam * (a2 / l2)
    o = _rms(o, SUBLN_EPS) * sw_ref[...] * (1.0 - lambda_init)
    o_ref[...] = o.astype(BF16)


def _attn(q, k, v, lq1, lk1, lq2, lk2, subln_w, lambda_init):
    b, s, _ = q.shape
    tq = min(ATTN_TILE, s)
    vec = lambda a: a.reshape(1, -1)
    small = lambda w: pl.BlockSpec((1, w), lambda bi, h, i: (0, 0))
    kv_spec = pl.BlockSpec((None, s, DIFF_V_DIM), lambda bi, h, i: (bi, 0, h))
    q_spec = pl.BlockSpec((None, tq, DIFF_V_DIM), lambda bi, h, i: (bi, i, h))
    return pl.pallas_call(
        functools.partial(_attn_kernel, lambda_init=lambda_init),
        out_shape=jax.ShapeDtypeStruct(q.shape, BF16),
        grid=(b, N_DIFF_HEADS, s // tq),
        in_specs=[q_spec, kv_spec, kv_spec, small(DIFF_HEAD_DIM), small(DIFF_HEAD_DIM),
                  small(DIFF_HEAD_DIM), small(DIFF_HEAD_DIM), small(DIFF_V_DIM)],
        out_specs=q_spec,
        compiler_params=_params(("arbitrary",) * 3),
        name="attn",
    )(q, k, v, vec(lq1), vec(lk1), vec(lq2), vec(lk2), vec(subln_w))


def _hgrn_kernel(q_ref, f_ref, i_ref, g_ref, lbl_ref, gw_ref, o_ref, *, layer):
    seq = q_ref.shape[0]
    ck, sub = HGRN_CHUNK, HGRN_SUB
    lg = lbl_ref[...]
    e = jnp.exp(lg - jnp.max(lg, axis=0, keepdims=True))
    lb = jnp.sum(e[:layer + 1], axis=0, keepdims=True) / jnp.sum(e, axis=0, keepdims=True)
    gw = gw_ref[...]
    r_i = lax.broadcasted_iota(jnp.int32, (ck, ck), 0)
    c_i = lax.broadcasted_iota(jnp.int32, (ck, ck), 1)
    tril = (c_i <= r_i).astype(F32)
    nt = (((1,), (1,)), ((), ()))
    tn = (((0,), (0,)), ((), ()))

    def chunk(c, state_t):
        rows = pl.ds(pl.multiple_of(c * ck, ck), ck)
        qf = _silu(q_ref[rows, :].astype(F32))
        forget = lb + (1.0 - lb) * jax.nn.sigmoid(f_ref[rows, :])
        kin = 1.0 - forget
        bsum = jnp.dot(tril, jnp.log(forget), preferred_element_type=F32, precision=HIGHEST)
        vb = i_ref[rows, :]
        o = lax.dot_general((qf * jnp.exp(bsum)).astype(BF16), state_t.astype(BF16), nt,
                            preferred_element_type=F32)
        parts = []
        for i in range(ck // sub):
            lo, hi = i * sub, (i + 1) * sub
            ref_pt = bsum[lo:lo + 1, :]
            qh = (qf[lo:hi] * jnp.exp(bsum[lo:hi] - ref_pt)).astype(BF16)
            kh = (kin[:hi] * jnp.exp(ref_pt - bsum[:hi])).astype(BF16)
            att = lax.dot_general(qh, kh, nt, preferred_element_type=F32)
            causal = (lax.broadcasted_iota(jnp.int32, (sub, hi), 1)
                      <= lax.broadcasted_iota(jnp.int32, (sub, hi), 0) + lo)
            att = jnp.where(causal, att, 0.0)
            parts.append(jnp.dot(att.astype(BF16), vb[:hi], preferred_element_type=F32))
        o = o + jnp.concatenate(parts, axis=0)
        last = bsum[ck - 1:ck, :]
        kdec = (kin * jnp.exp(last - bsum)).astype(BF16)
        state_t = state_t * jnp.exp(last) + lax.dot_general(vb, kdec, tn, preferred_element_type=F32)
        o = _rms(o, RMS_EPS) * gw * _silu(g_ref[rows, :].astype(F32))
        o_ref[rows, :] = o.astype(BF16)
        return state_t

    lax.fori_loop(0, seq // ck, chunk, jnp.zeros((HGRN_V_DIM, HGRN_EXPAND), F32))


def _hgrn(hq, hf, hi, hg, lb_logits, gnorm_w, layer):
    b, s, _ = hq.shape
    w = HGRN_EXPAND
    blk = pl.BlockSpec((None, s, w), lambda bi, h: (bi, 0, h))
    return pl.pallas_call(
        functools.partial(_hgrn_kernel, layer=layer),
        out_shape=jax.ShapeDtypeStruct(hq.shape, BF16),
        grid=(b, N_HGRN_HEADS),
        in_specs=[blk, blk, blk, blk,
                  pl.BlockSpec((lb_logits.shape[0], w), lambda bi, h: (0, h)),
                  pl.BlockSpec((1, w), lambda bi, h: (0, 0))],
        out_specs=blk,
        compiler_params=_params(("arbitrary",) * 2),
        name="hgrn",
    )(hq, hf, hi, hg, lb_logits, gnorm_w.reshape(1, w))


def _outproj_kernel(a_ref, r_ref, x_ref, g1_ref, sc_ref, sh_ref, nw_ref, wo_ref, wr_ref, br_ref,
                    x1_ref, h2_ref, route_ref, cnt_ref, run_ref):
    i = pl.program_id(0)
    tm = x_ref.shape[0]
    half = a_ref.shape[1]

    @pl.when(i == 0)
    def _():
        run_ref[...] = jnp.zeros_like(run_ref)

    mix = (jnp.dot(a_ref[...], wo_ref[:half, :], preferred_element_type=F32)
           + jnp.dot(r_ref[...], wo_ref[half:, :], preferred_element_type=F32))
    x1 = x_ref[...] + g1_ref[...] * mix
    x1_ref[...] = x1
    h2 = _rms(x1, RMS_EPS) * nw_ref[...]
    h2 = h2 * (1.0 + sc_ref[...]) + sh_ref[...]
    h2_ref[...] = h2

    logits = jnp.dot(h2, wr_ref[...], preferred_element_type=F32, precision=HIGHEST) + br_ref[...]
    lane = lax.broadcasted_iota(jnp.int32, logits.shape, 1)
    big = jnp.int32(LANES)
    neg = -jnp.inf
    is_g = lane < N_GROUPS
    gl = jnp.where(is_g, logits, neg)
    gmax = jnp.max(gl, axis=-1, keepdims=True)
    gsel = jnp.min(jnp.where(gl == gmax, lane, big), axis=-1, keepdims=True)
    pg = 1.0 / jnp.sum(jnp.where(is_g, jnp.exp(logits - gmax), 0.0), axis=-1, keepdims=True)
    lo = ROUTER_LANE0 + EXPERTS_PER_GROUP * gsel
    el = jnp.where((lane >= lo) & (lane < lo + EXPERTS_PER_GROUP), logits, neg)
    v0 = jnp.max(el, axis=-1, keepdims=True)
    i0 = jnp.min(jnp.where(el == v0, lane, big), axis=-1, keepdims=True)
    el = jnp.where(lane == i0, neg, el)
    v1 = jnp.max(el, axis=-1, keepdims=True)
    i1 = jnp.min(jnp.where(el == v1, lane, big), axis=-1, keepdims=True)
    t = jnp.exp(v1 - v0)
    w0 = pg / (1.0 + t)
    w1 = pg * t / (1.0 + t)
    hit0 = lane == i0
    hit1 = lane == i1
    onehot = jnp.where(hit0 | hit1, 1.0, 0.0)
    r_i = lax.broadcasted_iota(jnp.int32, (tm, tm), 0)
    c_i = lax.broadcasted_iota(jnp.int32, (tm, tm), 1)
    before = jnp.where(c_i < r_i, 1.0, 0.0).astype(BF16)
    prefix = jnp.dot(before, onehot.astype(BF16), preferred_element_type=F32) + run_ref[...]
    rank0 = jnp.sum(jnp.where(hit0, prefix, 0.0), axis=-1, keepdims=True)
    rank1 = jnp.sum(jnp.where(hit1, prefix, 0.0), axis=-1, keepdims=True)
    run = run_ref[...] + jnp.sum(onehot, axis=0, keepdims=True)
    run_ref[...] = run
    cnt_ref[...] = run
    e0 = (i0 - ROUTER_LANE0).astype(F32)
    e1 = (i1 - ROUTER_LANE0).astype(F32)
    slab = jnp.zeros(logits.shape, F32)
    for ln, val in enumerate((e0, e1, w0, w1, rank0, rank1)):
        slab = jnp.where(lane == ln, val, slab)
    route_ref[...] = slab


def _outproj(a2, r2, x2, mod4, norm_w, w_out_bf, w_router, b_router, seq):
    n, d = x2.shape
    tm = min(ROW_TILE, seq)
    tpb = seq // tm
    row = lambda i: (i, 0)
    fixed = lambda i: (0, 0)
    modspec = lambda j: pl.BlockSpec((None, None, 1, d), lambda i: (i // tpb, j, 0, 0))
    return pl.pallas_call(
        _outproj_kernel,
        out_shape=(jax.ShapeDtypeStruct((n, d), F32), jax.ShapeDtypeStruct((n, d), F32),
                   jax.ShapeDtypeStruct((n, LANES), F32), jax.ShapeDtypeStruct((1, LANES), F32)),
        grid=(n // tm,),
        in_specs=[pl.BlockSpec((tm, a2.shape[1]), row), pl.BlockSpec((tm, r2.shape[1]), row),
                  pl.BlockSpec((tm, d), row), modspec(2), modspec(4), modspec(3),
                  pl.BlockSpec((1, d), fixed), pl.BlockSpec(w_out_bf.shape, fixed),
                  pl.BlockSpec(w_router.shape, fixed), pl.BlockSpec((1, LANES), fixed)],
        out_specs=(pl.BlockSpec((tm, d), row), pl.BlockSpec((tm, d), row),
                   pl.BlockSpec((tm, LANES), row), pl.BlockSpec((1, LANES), fixed)),
        scratch_shapes=[pltpu.VMEM((1, LANES), F32)],
        compiler_params=_params(("arbitrary",)),
        name="outproj",
    )(a2, r2, x2, mod4, mod4, mod4, norm_w.reshape(1, d), w_out_bf, w_router, b_router)


def _row_copy(src_ref, src_row, dst_ref, dst_row, sem):
    return pltpu.make_async_copy(src_ref.at[pl.ds(src_row, 1), :], dst_ref.at[pl.ds(dst_row, 1), :], sem)


def _dispatch_kernel(dest_ref, h_ref, xs_in_ref, xs_ref, sem):
    del xs_in_ref
    tm = h_ref.shape[0]

    def issue(r, _):
        _row_copy(h_ref, r, xs_ref, dest_ref[0, r], sem).start()
        _row_copy(h_ref, r, xs_ref, dest_ref[0, tm + r], sem).start()
        return 0

    lax.fori_loop(0, tm, issue, 0)

    def drain(r, _):
        _row_copy(h_ref, 0, xs_ref, 0, sem).wait()
        _row_copy(h_ref, 0, xs_ref, 0, sem).wait()
        return 0

    lax.fori_loop(0, tm, drain, 0)


def _dispatch(dest3, h2, xs_zero):
    n, d = h2.shape
    tm = dest3.shape[2] // 2
    return pl.pallas_call(
        _dispatch_kernel,
        out_shape=jax.ShapeDtypeStruct(xs_zero.shape, xs_zero.dtype),
        grid=(n // tm,),
        in_specs=[pl.BlockSpec((None, 1, 2 * tm), lambda i: (i, 0, 0), memory_space=pltpu.SMEM),
                  pl.BlockSpec((tm, d), lambda i: (i, 0)),
                  pl.BlockSpec(memory_space=pl.ANY)],
        out_specs=pl.BlockSpec(memory_space=pl.ANY),
        scratch_shapes=[pltpu.SemaphoreType.DMA],
        input_output_aliases={2: 0},
        compiler_params=_params(("arbitrary",), has_side_effects=True),
        name="dispatch",
    )(dest3, h2, xs_zero)


def _experts_kernel(te_ref, nu_ref, xs_ref, wg_ref, wu_ref, wd_ref, ys_ref, wg_s, wu_s, wd_s):
    i = pl.program_id(0)

    @pl.when(i < nu_ref[0])
    def _():
        prev = te_ref[jnp.maximum(i - 1, 0)]

        @pl.when((i == 0) | (te_ref[i] != prev))
        def _():
            wg_s[...] = wg_ref[...].astype(BF16)
            wu_s[...] = wu_ref[...].astype(BF16)
            wd_s[...] = wd_ref[...].astype(BF16)

        x = xs_ref[...].astype(BF16)
        a = jnp.dot(x, wg_s[...], preferred_element_type=F32)
        u = jnp.dot(x, wu_s[...], preferred_element_type=F32)
        hid = (_silu(a) * u).astype(BF16)
        ys_ref[...] = jnp.dot(hid, wd_s[...], preferred_element_type=F32)


def _experts(tile_expert, n_used, xs, w_gate, w_up, w_down):
    m, d = xs.shape
    e, _, f = w_gate.shape
    tm = EXPERT_TILE
    n_tiles = m // tm
    row = lambda i, te, nu: (jnp.minimum(i, nu[0] - 1), 0)
    wsel = lambda i, te, nu: (te[i], 0, 0)
    grid_spec = pltpu.PrefetchScalarGridSpec(
        num_scalar_prefetch=2,
        grid=(n_tiles,),
        in_specs=[pl.BlockSpec((tm, d), row),
                  pl.BlockSpec((None, d, f), wsel), pl.BlockSpec((None, d, f), wsel),
                  pl.BlockSpec((None, f, d), wsel)],
        out_specs=pl.BlockSpec((tm, d), row),
        scratch_shapes=[pltpu.VMEM((d, f), BF16), pltpu.VMEM((d, f), BF16), pltpu.VMEM((f, d), BF16)],
    )
    return pl.pallas_call(
        _experts_kernel,
        out_shape=jax.ShapeDtypeStruct((m, d), F32),
        grid_spec=grid_spec,
        compiler_params=_params(("arbitrary",)),
        name="experts",
    )(tile_expert, n_used, xs, w_gate, w_up, w_down)


def _combine_kernel(dest_ref, ys_ref, x1_ref, route_ref, g2_ref, fw_ref, o_ref, y0_s, y1_s, sem):
    tm = x1_ref.shape[0]

    def issue(r, _):
        _row_copy(ys_ref, dest_ref[0, r], y0_s, r, sem).start()
        _row_copy(ys_ref, dest_ref[0, tm + r], y1_s, r, sem).start()
        return 0

    lax.fori_loop(0, tm, issue, 0)

    def drain(r, _):
        _row_copy(ys_ref, 0, y0_s, 0, sem).wait()
        _row_copy(ys_ref, 0, y1_s, 0, sem).wait()
        return 0

    lax.fori_loop(0, tm, drain, 0)
    route = route_ref[...]
    w0 = route[:, 2:3]
    w1 = route[:, 3:4]
    x = x1_ref[...] + g2_ref[...] * (w0 * y0_s[...] + w1 * y1_s[...])
    o_ref[...] = _rms(x, RMS_EPS) * fw_ref[...]


def _combine(dest3, ys, x1, route, mod4, final_w, seq):
    n, d = x1.shape
    tm = dest3.shape[2] // 2
    tpb = seq // tm
    row = lambda i: (i, 0)
    return pl.pallas_call(
        _combine_kernel,
        out_shape=jax.ShapeDtypeStruct((n, d), F32),
        grid=(n // tm,),
        in_specs=[pl.BlockSpec((None, 1, 2 * tm), lambda i: (i, 0, 0), memory_space=pltpu.SMEM),
                  pl.BlockSpec(memory_space=pl.ANY),
                  pl.BlockSpec((tm, d), row), pl.BlockSpec((tm, LANES), row),
                  pl.BlockSpec((None, None, 1, d), lambda i: (i // tpb, 5, 0, 0)),
                  pl.BlockSpec((1, d), lambda i: (0, 0))],
        out_specs=pl.BlockSpec((tm, d), row),
        scratch_shapes=[pltpu.VMEM((tm, d), F32), pltpu.VMEM((tm, d), F32), pltpu.SemaphoreType.DMA],
        compiler_params=_params(("arbitrary",)),
        name="combine",
    )(dest3, ys, x1, route, mod4, final_w.reshape(1, d))


def _plan(route, counts_row, seq):
    n = route.shape[0]
    tm = min(ROW_TILE, seq)
    e0 = route[:, 0].astype(jnp.int32)
    e1 = route[:, 1].astype(jnp.int32)
    rank0 = route[:, 4].astype(jnp.int32)
    rank1 = route[:, 5].astype(jnp.int32)
    counts = counts_row[0, ROUTER_LANE0:ROUTER_LANE0 + N_EXPERTS].astype(jnp.int32)
    tiles = (counts + EXPERT_TILE - 1) // EXPERT_TILE
    tile_end = jnp.cumsum(tiles)
    offs = (tile_end - tiles) * EXPERT_TILE
    dest0 = offs[e0] + rank0
    dest1 = offs[e1] + rank1
    dest3 = jnp.concatenate([dest0.reshape(n // tm, 1, tm), dest1.reshape(n // tm, 1, tm)], axis=2)
    n_tiles = (2 * n) // EXPERT_TILE + N_EXPERTS
    n_used = tile_end[-1]
    tile_ids = jnp.minimum(jnp.arange(n_tiles, dtype=jnp.int32), n_used - 1)
    tile_expert = jnp.sum(tile_ids[:, None] >= tile_end[None, :], axis=1).astype(jnp.int32)
    return dest3, tile_expert, n_used.reshape(1).astype(jnp.int32), n_tiles


def kernel(x, c, positions, w_ada, b_ada, norm1_w, w_in, lambda_q1, lambda_k1, lambda_q2, lambda_k2, subln_w, hgrn_lb_logits, gnorm_w, w_out, norm2_w, w_group_router, b_group_router, w_expert_router, b_expert_router, w_gate, w_up, w_down, final_norm_w):
    b, s, d = x.shape
    n = b * s
    depth = w_ada.shape[0]
    half = DIFF_HEAD_DIM // 2

    inv_freq = ROPE_THETA ** (-jnp.arange(half, dtype=F32) / half)
    ang = positions.astype(F32).reshape(n, 1) * inv_freq[None, :]
    cos_t, sin_t = _trig(ang.reshape(n * half // LANES, LANES))
    cos_t = cos_t.reshape(n, half)
    sin_t = sin_t.reshape(n, half)
    cos = jnp.tile(cos_t, (1, LANES // half))
    sin = jnp.tile(jnp.concatenate([-sin_t, sin_t], axis=1), (1, LANES // DIFF_HEAD_DIM))

    xf = x.reshape(n, d)
    for l in range(depth):
        lambda_init = 0.8 - 0.6 * math.exp(-0.3 * l)
        mod4 = _mod(c, w_ada[l], b_ada[l]).reshape(b, 6, 1, d)
        q, k, v, hq, hf, hi, hg = _inproj(xf, mod4, norm1_w[l], w_in[l].astype(BF16), cos, sin, s)
        to3 = lambda t: t.reshape(b, s, SEG)
        a = _attn(to3(q), to3(k), to3(v), lambda_q1[l], lambda_k1[l], lambda_q2[l], lambda_k2[l],
                  subln_w[l], lambda_init)
        r = _hgrn(to3(hq), to3(hf), to3(hi), to3(hg), hgrn_lb_logits, gnorm_w[l], l)

        w_router = jnp.zeros((d, LANES), F32)
        w_router = w_router.at[:, :N_GROUPS].set(w_group_router[l])
        w_router = w_router.at[:, ROUTER_LANE0:ROUTER_LANE0 + N_EXPERTS].set(w_expert_router[l])
        b_router = jnp.zeros((1, LANES), F32)
        b_router = b_router.at[0, :N_GROUPS].set(b_group_router[l])
        b_router = b_router.at[0, ROUTER_LANE0:ROUTER_LANE0 + N_EXPERTS].set(b_expert_router[l].reshape(-1))
        x1, h2, route, counts = _outproj(a.reshape(n, SEG), r.reshape(n, SEG), xf, mod4, norm2_w[l],
                                         w_out[l].astype(BF16), w_router, b_router, s)

        dest3, tile_expert, n_used, n_tiles = _plan(route, counts, s)
        xs = _dispatch(dest3, h2, jnp.zeros((n_tiles * EXPERT_TILE, d), F32))
        e_w = lambda w: w.reshape((N_EXPERTS,) + w.shape[2:])
        ys = _experts(tile_expert, n_used, xs, e_w(w_gate[l]), e_w(w_up[l]), e_w(w_down[l]))
        last = l == depth - 1
        if not last:
            raise NotImplementedError("stacked layers need a non-final combine")
        xf = _combine(dest3, ys, x1, route, mod4, final_norm_w, s)
    return xf.reshape(b, s, d)
```

```python
import functools
import math

import jax
import jax.numpy as jnp
from jax import lax
from jax.experimental import pallas as pl
from jax.experimental.pallas import tpu as pltpu

F32 = jnp.float32
BF16 = jnp.bfloat16
HIGHEST = lax.Precision.HIGHEST

N_DIFF_HEADS = 4
DIFF_HEAD_DIM = 64
DIFF_V_DIM = 2 * DIFF_HEAD_DIM
N_HGRN_HEADS = 4
HGRN_EXPAND = 128
HGRN_V_DIM = 128
ROPE_THETA = 10000.0
N_GROUPS = 4
EXPERTS_PER_GROUP = 8
N_EXPERTS = N_GROUPS * EXPERTS_PER_GROUP
RMS_EPS = 1e-6
SUBLN_EPS = 1e-5
SEG = 512
LANES = 128
ROUTER_LANE0 = N_GROUPS

ROW_TILE = 256
OUTPROJ_TILE = 512
ATTN_TILE = 256
LOG2E = math.log2(math.e)
HGRN_CHUNK = 64
HGRN_SUB = 16
HGRN_UNROLL = 8
EXPERT_TILE = 256
DMA_UNROLL = 16
VMEM_LIMIT = 48 * 1024 * 1024


def _params(sem, **kw):
    return pltpu.CompilerParams(dimension_semantics=sem, vmem_limit_bytes=VMEM_LIMIT, **kw)


def _silu(x):
    return x * jax.nn.sigmoid(x)


def _rms(x, eps):
    return x * lax.rsqrt(jnp.mean(x * x, axis=-1, keepdims=True) + eps)


def _mod_kernel(c_ref, w_ref, b_ref, o_ref):
    ca = _silu(c_ref[...])
    o_ref[...] = jnp.dot(ca, w_ref[...], preferred_element_type=F32, precision=HIGHEST) + b_ref[...]


def _mod(c, w_ada, b_ada):
    bsz, d = c.shape
    n_out = w_ada.shape[1]
    return pl.pallas_call(
        _mod_kernel,
        out_shape=jax.ShapeDtypeStruct((bsz, n_out), F32),
        grid=(n_out // d,),
        in_specs=[pl.BlockSpec((bsz, d), lambda j: (0, 0)),
                  pl.BlockSpec((d, d), lambda j: (0, j)),
                  pl.BlockSpec((1, d), lambda j: (0, j))],
        out_specs=pl.BlockSpec((bsz, d), lambda j: (0, j)),
        compiler_params=_params(("arbitrary",)),
        name="mod",
    )(c, w_ada, b_ada.reshape(1, n_out))


def _trig_kernel(a_ref, cos_ref, sin_ref):
    a = a_ref[...]
    cos_ref[...] = jnp.cos(a)
    sin_ref[...] = jnp.sin(a)


def _trig(ang):
    rows = ang.shape[0]
    blk = min(rows, 512)
    spec = pl.BlockSpec((blk, LANES), lambda i: (i, 0))
    return pl.pallas_call(
        _trig_kernel,
        out_shape=(jax.ShapeDtypeStruct(ang.shape, F32),) * 2,
        grid=(rows // blk,),
        in_specs=[spec],
        out_specs=(spec, spec),
        compiler_params=_params(("arbitrary",)),
        name="trig",
    )(ang)


def _inproj_kernel(x_ref, sc_ref, sh_ref, nw_ref, w_ref, cos_ref, sin_ref,
                   q_ref, k_ref, v_ref, hq_ref, hf_ref, hi_ref, hg_ref):
    h = _rms(x_ref[...], RMS_EPS) * nw_ref[...]
    h = h * (1.0 + sc_ref[...]) + sh_ref[...]
    hb = h.astype(BF16)
    cos = cos_ref[...]
    sin = sin_ref[...]
    lane = lax.broadcasted_iota(jnp.int32, cos.shape, 1)
    first_half = (lane % DIFF_HEAD_DIM) < (DIFF_HEAD_DIM // 2)

    def seg(i):
        return jnp.dot(hb, w_ref[:, i * SEG:(i + 1) * SEG], preferred_element_type=F32)

    def rope(p, scale):
        outs = []
        for c in range(SEG // LANES):
            pc = p[:, c * LANES:(c + 1) * LANES]
            partner = jnp.where(first_half, pltpu.roll(pc, LANES - DIFF_HEAD_DIM // 2, 1),
                                pltpu.roll(pc, DIFF_HEAD_DIM // 2, 1))
            outs.append((pc * cos + partner * sin) * scale)
        return jnp.concatenate(outs, axis=1)

    q_ref[...] = rope(seg(0), DIFF_HEAD_DIM ** -0.5 * LOG2E).astype(BF16)
    k_ref[...] = rope(seg(1), 1.0).astype(BF16)
    v_ref[...] = seg(2).astype(BF16)
    hq_ref[...] = seg(3).astype(BF16)
    hf_ref[...] = seg(4)
    hi_ref[...] = seg(5).astype(BF16)
    hg_ref[...] = seg(6).astype(BF16)


def _inproj(x2, mod4, norm_w, w_in_bf, cos, sin, seq):
    n, d = x2.shape
    tm = min(ROW_TILE, seq)
    tpb = seq // tm
    row = lambda i: (i, 0)
    modspec = lambda j: pl.BlockSpec((None, None, 1, d), lambda i: (i // tpb, j, 0, 0))
    seg_spec = pl.BlockSpec((tm, SEG), row)
    bf = jax.ShapeDtypeStruct((n, SEG), BF16)
    return pl.pallas_call(
        _inproj_kernel,
        out_shape=(bf, bf, bf, bf, jax.ShapeDtypeStruct((n, SEG), F32), bf, bf),
        grid=(n // tm,),
        in_specs=[pl.BlockSpec((tm, d), row), modspec(1), modspec(0),
                  pl.BlockSpec((1, d), lambda i: (0, 0)),
                  pl.BlockSpec(w_in_bf.shape, lambda i: (0, 0)),
                  pl.BlockSpec((tm, LANES), row), pl.BlockSpec((tm, LANES), row)],
        out_specs=(seg_spec,) * 7,
        compiler_params=_params(("arbitrary",)),
        name="inproj",
    )(x2, mod4, mod4, norm_w.reshape(1, d), w_in_bf, cos, sin)


def _attn_kernel(q_ref, k_ref, v_ref, lq1_ref, lk1_ref, lq2_ref, lk2_ref, sw_ref, o_ref,
                 vt_s, acc_s, *, lambda_init):
    tq = q_ref.shape[0]
    seq = k_ref.shape[0]
    tk = tq
    vd = DIFF_V_DIM
    qi = pl.program_id(1)
    nt = (((1,), (1,)), ((), ()))
    chains = [(h, mp) for h in range(N_DIFF_HEADS) for mp in range(2)]

    @pl.when(qi == 0)
    def _():
        for c in range(seq // tq):
            vt_s[:, c * tq:(c + 1) * tq] = v_ref[c * tq:(c + 1) * tq, :].astype(F32).T.astype(BF16)

    acc_s[...] = jnp.zeros_like(acc_s)
    lane = lax.broadcasted_iota(jnp.int32, (tq, vd), 1)
    qz = []
    for h, mp in chains:
        qh = q_ref[:, h * vd:(h + 1) * vd]
        keep = (lane < DIFF_HEAD_DIM) if mp == 0 else (lane >= DIFF_HEAD_DIM)
        qz.append(jnp.where(keep, qh, jnp.zeros_like(qh)))

    def step(j, carry, diag):
        row0 = pl.multiple_of(j * tk, tk)
        scores = [lax.dot_general(k_ref[pl.ds(row0, tk), h * vd:(h + 1) * vd], qz[c], nt,
                                  preferred_element_type=F32) for c, (h, _) in enumerate(chains)]
        probs, alphas, out = [], [], []
        for c, s in enumerate(scores):
            m, l8 = carry[c]
            if diag:
                kidx = lax.broadcasted_iota(jnp.int32, s.shape, 0)
                qidx = lax.broadcasted_iota(jnp.int32, s.shape, 1)
                s = jnp.where(kidx <= qidx, s, -jnp.inf)
            m_new = jnp.maximum(m, jnp.max(s, axis=0, keepdims=True))
            alpha = jnp.exp2(m - m_new)
            p = jnp.exp2(s - m_new)
            out.append((m_new, alpha * l8 + jnp.sum(p.reshape(tk // 8, 8, tq), axis=0)))
            probs.append(p.astype(BF16))
            alphas.append(alpha)
        pvs = [jnp.dot(vt_s[h * vd:(h + 1) * vd, pl.ds(row0, tk)], probs[c], preferred_element_type=F32)
               for c, (h, _) in enumerate(chains)]
        for c, pv in enumerate(pvs):
            acc_s[c] = alphas[c] * acc_s[c] + pv
        return tuple(out)

    init = tuple((jnp.full((1, tq), -jnp.inf, F32), jnp.zeros((8, tq), F32)) for _ in chains)
    carry = lax.fori_loop(0, qi, lambda j, c: step(j, c, False), init)
    carry = step(qi, carry, True)

    lam = (jnp.exp(jnp.sum(lq1_ref[...] * lk1_ref[...], axis=-1, keepdims=True))
           - jnp.exp(jnp.sum(lq2_ref[...] * lk2_ref[...], axis=-1, keepdims=True)) + lambda_init)
    sw = jnp.concatenate([sw_ref[...]] * (tq // LANES), axis=1)
    for h in range(N_DIFF_HEADS):
        inv1 = 1.0 / jnp.sum(carry[2 * h][1], axis=0, keepdims=True)
        inv2 = 1.0 / jnp.sum(carry[2 * h + 1][1], axis=0, keepdims=True)
        o = acc_s[2 * h] * inv1 - lam * (acc_s[2 * h + 1] * inv2)
        o = o * lax.rsqrt(jnp.mean(o * o, axis=0, keepdims=True) + SUBLN_EPS)
        o_ref[:, h * vd:(h + 1) * vd] = (o * sw * (1.0 - lambda_init)).T.astype(BF16)


def _attn(q, k, v, lq1, lk1, lq2, lk2, subln_w, lambda_init):
    b, s, width = q.shape
    tq = min(ATTN_TILE, s)
    vec = lambda a: a.reshape(1, -1)
    small = lambda w: pl.BlockSpec((1, w), lambda bi, i: (0, 0))
    kv_spec = pl.BlockSpec((None, s, width), lambda bi, i: (bi, 0, 0))
    q_spec = pl.BlockSpec((None, tq, width), lambda bi, i: (bi, i, 0))
    sw_cols = jnp.broadcast_to(subln_w.reshape(DIFF_V_DIM, 1), (DIFF_V_DIM, LANES))
    return pl.pallas_call(
        functools.partial(_attn_kernel, lambda_init=lambda_init),
        out_shape=jax.ShapeDtypeStruct(q.shape, BF16),
        grid=(b, s // tq),
        in_specs=[q_spec, kv_spec, kv_spec, small(DIFF_HEAD_DIM), small(DIFF_HEAD_DIM),
                  small(DIFF_HEAD_DIM), small(DIFF_HEAD_DIM),
                  pl.BlockSpec((DIFF_V_DIM, LANES), lambda bi, i: (0, 0))],
        out_specs=q_spec,
        scratch_shapes=[pltpu.VMEM((width, s), BF16), pltpu.VMEM((2 * N_DIFF_HEADS, DIFF_V_DIM, tq), F32)],
        compiler_params=_params(("arbitrary",) * 2),
        name="attn",
    )(q, k, v, vec(lq1), vec(lk1), vec(lq2), vec(lk2), sw_cols)


def _hgrn_kernel(q_ref, f_ref, i_ref, g_ref, lbl_ref, gw_ref, o_ref, qd_s, oi_s, u_s, d_s, *, layer):
    seq = q_ref.shape[0]
    ck, sub = HGRN_CHUNK, HGRN_SUB
    lg = lbl_ref[...]
    e = jnp.exp(lg - jnp.max(lg, axis=0, keepdims=True))
    lb = jnp.sum(e[:layer + 1], axis=0, keepdims=True) / jnp.sum(e, axis=0, keepdims=True)
    gw = gw_ref[...]
    causal = (lax.broadcasted_iota(jnp.int32, (ck, ck), 1)
              <= lax.broadcasted_iota(jnp.int32, (ck, ck), 0))
    tril = causal.astype(BF16)
    nt = (((1,), (1,)), ((), ()))
    tn = (((0,), (0,)), ((), ()))
    w = HGRN_EXPAND
    nsub = ck // sub

    def phase1(c2, _):
        chunks = [c2 * HGRN_UNROLL + u for u in range(HGRN_UNROLL)]
        rows = [pl.ds(pl.multiple_of(c * ck, ck), ck) for c in chunks]
        kins, splits = [], []
        for r in rows:
            forget = lb + (1.0 - lb) * jax.nn.sigmoid(f_ref[r, :])
            kins.append(1.0 - forget)
            g = jnp.log(forget)
            g_hi = g.astype(BF16)
            r1 = g - g_hi.astype(F32)
            g_mid = r1.astype(BF16)
            g_lo = (r1 - g_mid.astype(F32)).astype(BF16)
            splits.append(jnp.concatenate([g_hi, g_mid, g_lo], axis=1))
        sums = [jnp.dot(tril, x, preferred_element_type=F32) for x in splits]

        q_blk, k_blk, kdecs = [], [], []
        for c, r, kin, cs in zip(chunks, rows, kins, sums):
            bsum = cs[:, :w] + cs[:, w:2 * w] + cs[:, 2 * w:]
            qf = _silu(q_ref[r, :].astype(F32))
            qd_s[r, :] = (qf * jnp.exp(bsum)).astype(BF16)
            q_rows, k_cols = [], []
            for i in range(nsub):
                lo, hi = i * sub, (i + 1) * sub
                ref_pt = bsum[lo:lo + 1, :]
                qh = (qf[lo:hi] * jnp.exp(bsum[lo:hi] - ref_pt)).astype(BF16)
                kh = (kin[:hi] * jnp.exp(ref_pt - bsum[:hi])).astype(BF16)
                pieces = [qh if j == i else jnp.zeros((sub, w), BF16) for j in range(nsub)]
                q_rows.append(jnp.concatenate(pieces, axis=1))
                k_cols.append(kh if hi == ck else jnp.concatenate([kh, jnp.zeros((ck - hi, w), BF16)], axis=0))
            q_blk.append(jnp.concatenate(q_rows, axis=0))
            k_blk.append(jnp.concatenate(k_cols, axis=1))
            last = bsum[ck - 1:ck, :]
            kdecs.append((kin * jnp.exp(last - bsum)).astype(BF16))
            d_s[c] = jnp.exp(last)
        vbs = [i_ref[r, :] for r in rows]
        atts = [lax.dot_general(qb, kb, nt, preferred_element_type=F32) for qb, kb in zip(q_blk, k_blk)]
        for c, vb, kdec in zip(chunks, vbs, kdecs):
            u_s[c] = lax.dot_general(vb, kdec, tn, preferred_element_type=F32)
        probs = [jnp.where(causal, att, 0.0).astype(BF16) for att in atts]
        for r, p, vb in zip(rows, probs, vbs):
            oi_s[r, :] = jnp.dot(p, vb, preferred_element_type=F32)
        return 0

    lax.fori_loop(0, seq // (ck * HGRN_UNROLL), phase1, 0)

    def phase2(c2, state_t):
        for u in range(HGRN_UNROLL):
            c = c2 * HGRN_UNROLL + u
            rows = pl.ds(pl.multiple_of(c * ck, ck), ck)
            o = oi_s[rows, :] + lax.dot_general(qd_s[rows, :], state_t.astype(BF16), nt,
                                                preferred_element_type=F32)
            o = _rms(o, RMS_EPS) * gw * _silu(g_ref[rows, :].astype(F32))
            o_ref[rows, :] = o.astype(BF16)
            state_t = state_t * d_s[c] + u_s[c]
        return state_t

    lax.fori_loop(0, seq // (ck * HGRN_UNROLL), phase2, jnp.zeros((HGRN_V_DIM, HGRN_EXPAND), F32))


def _hgrn(hq, hf, hi, hg, lb_logits, gnorm_w, layer):
    b, s, _ = hq.shape
    w = HGRN_EXPAND
    blk = pl.BlockSpec((None, s, w), lambda bi, h: (bi, 0, h))
    return pl.pallas_call(
        functools.partial(_hgrn_kernel, layer=layer),
        out_shape=jax.ShapeDtypeStruct(hq.shape, BF16),
        grid=(b, N_HGRN_HEADS),
        in_specs=[blk, blk, blk, blk,
                  pl.BlockSpec((lb_logits.shape[0], w), lambda bi, h: (0, h)),
                  pl.BlockSpec((1, w), lambda bi, h: (0, 0))],
        out_specs=blk,
        scratch_shapes=[pltpu.VMEM((s, w), BF16), pltpu.VMEM((s, HGRN_V_DIM), F32),
                        pltpu.VMEM((s // HGRN_CHUNK, HGRN_V_DIM, w), F32),
                        pltpu.VMEM((s // HGRN_CHUNK, 1, w), F32)],
        compiler_params=_params(("arbitrary",) * 2),
        name="hgrn",
    )(hq, hf, hi, hg, lb_logits, gnorm_w.reshape(1, w))


def _outproj_kernel(a_ref, r_ref, x_ref, g1_ref, sc_ref, sh_ref, nw_ref, wo_ref, wr_ref, br_ref,
                    x1_ref, h2_ref, route_ref, cnt_ref, run_ref):
    i = pl.program_id(0)
    tm = x_ref.shape[0]
    half = a_ref.shape[1]

    @pl.when(i == 0)
    def _():
        run_ref[...] = jnp.zeros_like(run_ref)

    mix = (jnp.dot(a_ref[...], wo_ref[:half, :], preferred_element_type=F32)
           + jnp.dot(r_ref[...], wo_ref[half:, :], preferred_element_type=F32))
    x1 = x_ref[...] + g1_ref[...] * mix
    x1_ref[...] = x1
    h2 = _rms(x1, RMS_EPS) * nw_ref[...]
    h2 = h2 * (1.0 + sc_ref[...]) + sh_ref[...]
    h2_ref[...] = h2

    h_hi = h2.astype(BF16)
    h_lo = (h2 - h_hi.astype(F32)).astype(BF16)
    part = (jnp.dot(h_hi, wr_ref[...], preferred_element_type=F32)
            + jnp.dot(h_lo, wr_ref[...], preferred_element_type=F32))
    logits = part[:, :LANES] + part[:, LANES:] + br_ref[...]
    lane = lax.broadcasted_iota(jnp.int32, logits.shape, 1)
    big = jnp.int32(LANES)
    neg = -jnp.inf
    is_g = lane < N_GROUPS
    gl = jnp.where(is_g, logits, neg)
    gmax = jnp.max(gl, axis=-1, keepdims=True)
    gsel = jnp.min(jnp.where(gl == gmax, lane, big), axis=-1, keepdims=True)
    pg = 1.0 / jnp.sum(jnp.where(is_g, jnp.exp(logits - gmax), 0.0), axis=-1, keepdims=True)
    lo = ROUTER_LANE0 + EXPERTS_PER_GROUP * gsel
    el = jnp.where((lane >= lo) & (lane < lo + EXPERTS_PER_GROUP), logits, neg)
    v0 = jnp.max(el, axis=-1, keepdims=True)
    i0 = jnp.min(jnp.where(el == v0, lane, big), axis=-1, keepdims=True)
    el = jnp.where(lane == i0, neg, el)
    v1 = jnp.max(el, axis=-1, keepdims=True)
    i1 = jnp.min(jnp.where(el == v1, lane, big), axis=-1, keepdims=True)
    t = jnp.exp(v1 - v0)
    w0 = pg / (1.0 + t)
    w1 = pg * t / (1.0 + t)
    hit0 = lane == i0
    hit1 = lane == i1
    onehot = jnp.where(hit0 | hit1, 1.0, 0.0)
    r_i = lax.broadcasted_iota(jnp.int32, (tm, tm), 0)
    c_i = lax.broadcasted_iota(jnp.int32, (tm, tm), 1)
    before = jnp.where(c_i < r_i, 1.0, 0.0).astype(BF16)
    prefix = jnp.dot(before, onehot.astype(BF16), preferred_element_type=F32) + run_ref[...]
    rank0 = jnp.sum(jnp.where(hit0, prefix, 0.0), axis=-1, keepdims=True)
    rank1 = jnp.sum(jnp.where(hit1, prefix, 0.0), axis=-1, keepdims=True)
    run = run_ref[...] + jnp.sum(onehot, axis=0, keepdims=True)
    run_ref[...] = run
    cnt_ref[...] = run
    e0 = (i0 - ROUTER_LANE0).astype(F32)
    e1 = (i1 - ROUTER_LANE0).astype(F32)
    slab = jnp.zeros(logits.shape, F32)
    for ln, val in enumerate((e0, e1, w0, w1, rank0, rank1)):
        slab = jnp.where(lane == ln, val, slab)
    route_ref[...] = slab


def _outproj(a2, r2, x2, mod4, norm_w, w_out_bf, w_router, b_router, seq):
    n, d = x2.shape
    tm = min(OUTPROJ_TILE, seq)
    tpb = seq // tm
    row = lambda i: (i, 0)
    fixed = lambda i: (0, 0)
    modspec = lambda j: pl.BlockSpec((None, None, 1, d), lambda i: (i // tpb, j, 0, 0))
    return pl.pallas_call(
        _outproj_kernel,
        out_shape=(jax.ShapeDtypeStruct((n, d), F32), jax.ShapeDtypeStruct((n, d), F32),
                   jax.ShapeDtypeStruct((n, LANES), F32), jax.ShapeDtypeStruct((1, LANES), F32)),
        grid=(n // tm,),
        in_specs=[pl.BlockSpec((tm, a2.shape[1]), row), pl.BlockSpec((tm, r2.shape[1]), row),
                  pl.BlockSpec((tm, d), row), modspec(2), modspec(4), modspec(3),
                  pl.BlockSpec((1, d), fixed), pl.BlockSpec(w_out_bf.shape, fixed),
                  pl.BlockSpec(w_router.shape, fixed), pl.BlockSpec((1, LANES), fixed)],
        out_specs=(pl.BlockSpec((tm, d), row), pl.BlockSpec((tm, d), row),
                   pl.BlockSpec((tm, LANES), row), pl.BlockSpec((1, LANES), fixed)),
        scratch_shapes=[pltpu.VMEM((1, LANES), F32)],
        compiler_params=_params(("arbitrary",)),
        name="outproj",
    )(a2, r2, x2, mod4, mod4, mod4, norm_w.reshape(1, d), w_out_bf, w_router, b_router)


def _row_copy(src_ref, src_row, dst_ref, dst_row, sem):
    return pltpu.make_async_copy(src_ref.at[pl.ds(src_row, 1), :], dst_ref.at[pl.ds(dst_row, 1), :], sem)


def _dispatch_kernel(dest_ref, h_ref, xs_in_ref, xs_ref, sem):
    del xs_in_ref
    tm = h_ref.shape[0]

    def issue(g, _):
        for u in range(DMA_UNROLL):
            r = g * DMA_UNROLL + u
            _row_copy(h_ref, r, xs_ref, dest_ref[0, r], sem).start()
            _row_copy(h_ref, r, xs_ref, dest_ref[0, tm + r], sem).start()
        return 0

    lax.fori_loop(0, tm // DMA_UNROLL, issue, 0)

    def drain(g, _):
        for _u in range(2 * DMA_UNROLL):
            _row_copy(h_ref, 0, xs_ref, 0, sem).wait()
        return 0

    lax.fori_loop(0, tm // DMA_UNROLL, drain, 0)


def _dispatch(dest3, h2, xs_zero):
    n, d = h2.shape
    tm = dest3.shape[2] // 2
    return pl.pallas_call(
        _dispatch_kernel,
        out_shape=jax.ShapeDtypeStruct(xs_zero.shape, xs_zero.dtype),
        grid=(n // tm,),
        in_specs=[pl.BlockSpec((None, 1, 2 * tm), lambda i: (i, 0, 0), memory_space=pltpu.SMEM),
                  pl.BlockSpec((tm, d), lambda i: (i, 0)),
                  pl.BlockSpec(memory_space=pl.ANY)],
        out_specs=pl.BlockSpec(memory_space=pl.ANY),
        scratch_shapes=[pltpu.SemaphoreType.DMA],
        input_output_aliases={2: 0},
        compiler_params=_params(("arbitrary",), has_side_effects=True),
        name="dispatch",
    )(dest3, h2, xs_zero)


def _experts_kernel(te_ref, nu_ref, xs_ref, wg_ref, wu_ref, wd_ref, ys_ref, wg_s, wu_s, wd_s):
    i = pl.program_id(0)

    @pl.when(i < nu_ref[0])
    def _():
        prev = te_ref[jnp.maximum(i - 1, 0)]

        @pl.when((i == 0) | (te_ref[i] != prev))
        def _():
            wg_s[...] = wg_ref[...].astype(BF16)
            wu_s[...] = wu_ref[...].astype(BF16)
            wd_s[...] = wd_ref[...].astype(BF16)

        x = xs_ref[...].astype(BF16)
        a = jnp.dot(x, wg_s[...], preferred_element_type=F32)
        u = jnp.dot(x, wu_s[...], preferred_element_type=F32)
        hid = (_silu(a) * u).astype(BF16)
        ys_ref[...] = jnp.dot(hid, wd_s[...], preferred_element_type=F32)

    @pl.when(i >= nu_ref[0])
    def _():
        ys_ref[...] = jnp.zeros_like(ys_ref)


def _experts(tile_expert, n_used, xs, w_gate, w_up, w_down):
    m, d = xs.shape
    e, _, f = w_gate.shape
    tm = EXPERT_TILE
    n_tiles = m // tm
    row = lambda i, te, nu: (jnp.minimum(i, nu[0] - 1), 0)
    wsel = lambda i, te, nu: (te[i], 0, 0)
    grid_spec = pltpu.PrefetchScalarGridSpec(
        num_scalar_prefetch=2,
        grid=(n_tiles,),
        in_specs=[pl.BlockSpec((tm, d), row),
                  pl.BlockSpec((None, d, f), wsel), pl.BlockSpec((None, d, f), wsel),
                  pl.BlockSpec((None, f, d), wsel)],
        out_specs=pl.BlockSpec((tm, d), lambda i, te, nu: (i, 0)),
        scratch_shapes=[pltpu.VMEM((d, f), BF16), pltpu.VMEM((d, f), BF16), pltpu.VMEM((f, d), BF16)],
    )
    return pl.pallas_call(
        _experts_kernel,
        out_shape=jax.ShapeDtypeStruct((m, d), F32),
        grid_spec=grid_spec,
        compiler_params=_params(("arbitrary",)),
        name="experts",
    )(tile_expert, n_used, xs, w_gate, w_up, w_down)


def _combine_kernel(dest_ref, dnext_ref, ys_ref, x1_ref, route_ref, g2_ref, fw_ref, o_ref, y0_s, y1_s, sem):
    tm = x1_ref.shape[0]
    i = pl.program_id(0)
    slot = i % 2

    def gather(idx_ref, sl):
        def issue(g, _):
            for u in range(DMA_UNROLL):
                r = g * DMA_UNROLL + u
                _row_copy(ys_ref, idx_ref[0, r], y0_s.at[sl], r, sem.at[sl]).start()
                _row_copy(ys_ref, idx_ref[0, tm + r], y1_s.at[sl], r, sem.at[sl]).start()
            return 0

        lax.fori_loop(0, tm // DMA_UNROLL, issue, 0)

    @pl.when(i == 0)
    def _():
        gather(dest_ref, 0)

    @pl.when(i + 1 < pl.num_programs(0))
    def _():
        gather(dnext_ref, 1 - slot)

    def drain(g, _):
        for _u in range(DMA_UNROLL):
            _row_copy(ys_ref, 0, y0_s.at[slot], 0, sem.at[slot]).wait()
            _row_copy(ys_ref, 0, y1_s.at[slot], 0, sem.at[slot]).wait()
        return 0

    lax.fori_loop(0, tm // DMA_UNROLL, drain, 0)
    route = route_ref[...]
    w0 = route[:, 2:3]
    w1 = route[:, 3:4]
    x = x1_ref[...] + g2_ref[...] * (w0 * y0_s[slot] + w1 * y1_s[slot])
    o_ref[...] = _rms(x, RMS_EPS) * fw_ref[...]


def _combine(dest3, ys, x1, route, mod4, final_w, seq):
    n, d = x1.shape
    tm = dest3.shape[2] // 2
    tpb = seq // tm
    row = lambda i: (i, 0)
    steps = n // tm
    return pl.pallas_call(
        _combine_kernel,
        out_shape=jax.ShapeDtypeStruct((n, d), F32),
        grid=(steps,),
        in_specs=[pl.BlockSpec((None, 1, 2 * tm), lambda i: (i, 0, 0), memory_space=pltpu.SMEM),
                  pl.BlockSpec((None, 1, 2 * tm), lambda i: (jnp.minimum(i + 1, steps - 1), 0, 0),
                               memory_space=pltpu.SMEM),
                  pl.BlockSpec(memory_space=pl.ANY),
                  pl.BlockSpec((tm, d), row), pl.BlockSpec((tm, LANES), row),
                  pl.BlockSpec((None, None, 1, d), lambda i: (i // tpb, 5, 0, 0)),
                  pl.BlockSpec((1, d), lambda i: (0, 0))],
        out_specs=pl.BlockSpec((tm, d), row),
        scratch_shapes=[pltpu.VMEM((2, tm, d), F32), pltpu.VMEM((2, tm, d), F32),
                        pltpu.SemaphoreType.DMA((2,))],
        compiler_params=_params(("arbitrary",)),
        name="combine",
    )(dest3, dest3, ys, x1, route, mod4, final_w.reshape(1, d))


def _dest_kernel(route_ref, offs_ref, o_ref):
    route = route_ref[...]
    lane = lax.broadcasted_iota(jnp.int32, route.shape, 1)
    offs = offs_ref[...]
    slab = jnp.zeros(route.shape, F32)
    for k in range(2):
        e_lane = route[:, k:k + 1].astype(jnp.int32) + ROUTER_LANE0
        off = jnp.sum(jnp.where(lane == e_lane, offs, 0.0), axis=-1, keepdims=True)
        slab = jnp.where(lane == k, off + route[:, 4 + k:5 + k], slab)
    o_ref[...] = slab.T[:8, :].astype(jnp.int32)


def _dest(route, offs_row, tm):
    n = route.shape[0]
    blk = min(n, 2048)
    out = pl.pallas_call(
        _dest_kernel,
        out_shape=jax.ShapeDtypeStruct((8, n), jnp.int32),
        grid=(n // blk,),
        in_specs=[pl.BlockSpec((blk, LANES), lambda i: (i, 0)), pl.BlockSpec((1, LANES), lambda i: (0, 0))],
        out_specs=pl.BlockSpec((8, blk), lambda i: (0, i)),
        compiler_params=_params(("arbitrary",)),
        name="dest",
    )(route, offs_row)
    return jnp.concatenate([out[0].reshape(n // tm, 1, tm), out[1].reshape(n // tm, 1, tm)], axis=2)


def _plan(route, counts_row, seq):
    n = route.shape[0]
    tm = min(ROW_TILE, seq)
    counts = counts_row[0, ROUTER_LANE0:ROUTER_LANE0 + N_EXPERTS].astype(jnp.int32)
    tiles = (counts + EXPERT_TILE - 1) // EXPERT_TILE
    tile_end = jnp.cumsum(tiles)
    offs = (tile_end - tiles) * EXPERT_TILE
    offs_row = jnp.zeros((1, LANES), F32).at[0, ROUTER_LANE0:ROUTER_LANE0 + N_EXPERTS].set(offs.astype(F32))
    dest3 = _dest(route, offs_row, tm)
    n_tiles = (2 * n) // EXPERT_TILE + N_EXPERTS
    n_used = tile_end[-1]
    tile_ids = jnp.minimum(jnp.arange(n_tiles, dtype=jnp.int32), n_used - 1)
    tile_expert = jnp.sum(tile_ids[:, None] >= tile_end[None, :], axis=1).astype(jnp.int32)
    return dest3, tile_expert, n_used.reshape(1).astype(jnp.int32), n_tiles


def kernel(x, c, positions, w_ada, b_ada, norm1_w, w_in, lambda_q1, lambda_k1, lambda_q2, lambda_k2, subln_w, hgrn_lb_logits, gnorm_w, w_out, norm2_w, w_group_router, b_group_router, w_expert_router, b_expert_router, w_gate, w_up, w_down, final_norm_w):
    b, s, d = x.shape
    n = b * s
    depth = w_ada.shape[0]
    half = DIFF_HEAD_DIM // 2

    inv_freq = ROPE_THETA ** (-jnp.arange(half, dtype=F32) / half)
    ang = positions.astype(F32).reshape(n, 1) * inv_freq[None, :]
    cos_t, sin_t = _trig(ang.reshape(n * half // LANES, LANES))
    cos_t = cos_t.reshape(n, half)
    sin_t = sin_t.reshape(n, half)
    cos = jnp.tile(cos_t, (1, LANES // half))
    sin = jnp.tile(jnp.concatenate([-sin_t, sin_t], axis=1), (1, LANES // DIFF_HEAD_DIM))

    xf = x.reshape(n, d)
    for l in range(depth):
        lambda_init = 0.8 - 0.6 * math.exp(-0.3 * l)
        mod4 = _mod(c, w_ada[l], b_ada[l]).reshape(b, 6, 1, d)
        q, k, v, hq, hf, hi, hg = _inproj(xf, mod4, norm1_w[l], w_in[l].astype(BF16), cos, sin, s)
        to3 = lambda t: t.reshape(b, s, SEG)
        a = _attn(to3(q), to3(k), to3(v), lambda_q1[l], lambda_k1[l], lambda_q2[l], lambda_k2[l],
                  subln_w[l], lambda_init)
        r = _hgrn(to3(hq), to3(hf), to3(hi), to3(hg), hgrn_lb_logits, gnorm_w[l], l)

        w_router = jnp.zeros((d, LANES), F32)
        w_router = w_router.at[:, :N_GROUPS].set(w_group_router[l])
        w_router = w_router.at[:, ROUTER_LANE0:ROUTER_LANE0 + N_EXPERTS].set(w_expert_router[l])
        b_router = jnp.zeros((1, LANES), F32)
        b_router = b_router.at[0, :N_GROUPS].set(b_group_router[l])
        b_router = b_router.at[0, ROUTER_LANE0:ROUTER_LANE0 + N_EXPERTS].set(b_expert_router[l].reshape(-1))
        wr_hi = w_router.astype(BF16)
        wr_split = jnp.concatenate([wr_hi, (w_router - wr_hi.astype(F32)).astype(BF16)], axis=1)
        x1, h2, route, counts = _outproj(a.reshape(n, SEG), r.reshape(n, SEG), xf, mod4, norm2_w[l],
                                         w_out[l].astype(BF16), wr_split, b_router, s)

        dest3, tile_expert, n_used, n_tiles = _plan(route, counts, s)
        xs = _dispatch(dest3, h2, jnp.zeros((n_tiles * EXPERT_TILE, d), F32))
        e_w = lambda w: w.reshape((N_EXPERTS,) + w.shape[2:])
        ys = _experts(tile_expert, n_used, xs, e_w(w_gate[l]), e_w(w_up[l]), e_w(w_down[l]))
        last = l == depth - 1
        if not last:
            raise NotImplementedError("stacked layers need a non-final combine")
        xf = _combine(dest3, ys, x1, route, mod4, final_norm_w, s)
    return xf.reshape(b, s, d)
```

```python
import functools
import math

import jax
import jax.numpy as jnp
from jax import lax
from jax.experimental import pallas as pl
from jax.experimental.pallas import tpu as pltpu

F32 = jnp.float32
BF16 = jnp.bfloat16
HIGHEST = lax.Precision.HIGHEST

N_DIFF_HEADS = 4
DIFF_HEAD_DIM = 64
DIFF_V_DIM = 2 * DIFF_HEAD_DIM
N_HGRN_HEADS = 4
HGRN_EXPAND = 128
HGRN_V_DIM = 128
ROPE_THETA = 10000.0
N_GROUPS = 4
EXPERTS_PER_GROUP = 8
N_EXPERTS = N_GROUPS * EXPERTS_PER_GROUP
RMS_EPS = 1e-6
SUBLN_EPS = 1e-5
SEG = 512
LANES = 128
SUBLANES = 8
ROUTER_LANE0 = N_GROUPS

ROW_TILE = 256
OUTPROJ_TILE = 512
ATTN_TILE = 256
ATTN_ONES_ROWS = 16
LOG2E = math.log2(math.e)
HGRN_CHUNK = 64
HGRN_SUB = 16
HGRN_UNROLL = 8
EXPERT_TILE = 256
PAD_PIECES = (256, 128, 64, 32, 16, 8)
DMA_UNROLL = 16
VMEM_LIMIT = 48 * 1024 * 1024


def _params(sem, **kw):
    return pltpu.CompilerParams(dimension_semantics=sem, vmem_limit_bytes=VMEM_LIMIT, **kw)


def _silu(x):
    return x * jax.nn.sigmoid(x)


def _rms(x, eps):
    return x * lax.rsqrt(jnp.mean(x * x, axis=-1, keepdims=True) + eps)


def _mod_kernel(c_ref, w_ref, b_ref, o_ref):
    ca = _silu(c_ref[...])
    o_ref[...] = jnp.dot(ca, w_ref[...], preferred_element_type=F32, precision=HIGHEST) + b_ref[...]


def _mod(c, w_ada, b_ada):
    bsz, d = c.shape
    n_out = w_ada.shape[1]
    return pl.pallas_call(
        _mod_kernel,
        out_shape=jax.ShapeDtypeStruct((bsz, n_out), F32),
        grid=(n_out // d,),
        in_specs=[pl.BlockSpec((bsz, d), lambda j: (0, 0)),
                  pl.BlockSpec((d, d), lambda j: (0, j)),
                  pl.BlockSpec((1, d), lambda j: (0, j))],
        out_specs=pl.BlockSpec((bsz, d), lambda j: (0, j)),
        compiler_params=_params(("arbitrary",)),
        name="mod",
    )(c, w_ada, b_ada.reshape(1, n_out))


def _trig_kernel(a_ref, cos_ref, sin_ref):
    a = a_ref[...]
    cos_ref[...] = jnp.cos(a)
    sin_ref[...] = jnp.sin(a)


def _trig(ang):
    rows = ang.shape[0]
    blk = min(rows, 512)
    spec = pl.BlockSpec((blk, LANES), lambda i: (i, 0))
    return pl.pallas_call(
        _trig_kernel,
        out_shape=(jax.ShapeDtypeStruct(ang.shape, F32),) * 2,
        grid=(rows // blk,),
        in_specs=[spec],
        out_specs=(spec, spec),
        compiler_params=_params(("arbitrary",)),
        name="trig",
    )(ang)


def _inproj_kernel(x_ref, sc_ref, sh_ref, nw_ref, w_ref, cos_ref, sin_ref,
                   q_ref, k_ref, v_ref, hq_ref, hf_ref, hi_ref, hg_ref):
    h = _rms(x_ref[...], RMS_EPS) * nw_ref[...]
    h = h * (1.0 + sc_ref[...]) + sh_ref[...]
    hb = h.astype(BF16)
    cos = cos_ref[...]
    sin = sin_ref[...]
    lane = lax.broadcasted_iota(jnp.int32, cos.shape, 1)
    first_half = (lane % DIFF_HEAD_DIM) < (DIFF_HEAD_DIM // 2)

    def seg(i):
        return jnp.dot(hb, w_ref[:, i * SEG:(i + 1) * SEG], preferred_element_type=F32)

    def rope(p, scale):
        outs = []
        for c in range(SEG // LANES):
            pc = p[:, c * LANES:(c + 1) * LANES]
            partner = jnp.where(first_half, pltpu.roll(pc, LANES - DIFF_HEAD_DIM // 2, 1),
                                pltpu.roll(pc, DIFF_HEAD_DIM // 2, 1))
            outs.append((pc * cos + partner * sin) * scale)
        return jnp.concatenate(outs, axis=1)

    q_ref[...] = rope(seg(0), DIFF_HEAD_DIM ** -0.5 * LOG2E).astype(BF16)
    k_ref[...] = rope(seg(1), 1.0).astype(BF16)
    v_ref[...] = seg(2).astype(BF16)
    hq_ref[...] = seg(3).astype(BF16)
    hf_ref[...] = seg(4)
    hi_ref[...] = seg(5).astype(BF16)
    hg_ref[...] = seg(6).astype(BF16)


def _inproj(x2, mod4, norm_w, w_in_bf, cos, sin, seq):
    n, d = x2.shape
    tm = min(ROW_TILE, seq)
    tpb = seq // tm
    row = lambda i: (i, 0)
    modspec = lambda j: pl.BlockSpec((None, None, 1, d), lambda i: (i // tpb, j, 0, 0))
    seg_spec = pl.BlockSpec((tm, SEG), row)
    bf = jax.ShapeDtypeStruct((n, SEG), BF16)
    return pl.pallas_call(
        _inproj_kernel,
        out_shape=(bf, bf, bf, bf, jax.ShapeDtypeStruct((n, SEG), F32), bf, bf),
        grid=(n // tm,),
        in_specs=[pl.BlockSpec((tm, d), row), modspec(1), modspec(0),
                  pl.BlockSpec((1, d), lambda i: (0, 0)),
                  pl.BlockSpec(w_in_bf.shape, lambda i: (0, 0)),
                  pl.BlockSpec((tm, LANES), row), pl.BlockSpec((tm, LANES), row)],
        out_specs=(seg_spec,) * 7,
        compiler_params=_params(("arbitrary",)),
        name="inproj",
    )(x2, mod4, mod4, norm_w.reshape(1, d), w_in_bf, cos, sin)


def _attn_kernel(q_ref, k_ref, v_ref, lq1_ref, lk1_ref, lq2_ref, lk2_ref, sw_ref, o_ref,
                 vt_s, acc_s, p_s, *, lambda_init):
    tq = q_ref.shape[0]
    seq = k_ref.shape[0]
    tk = tq
    vd = DIFF_V_DIM
    qi = pl.program_id(1)
    nt = (((1,), (1,)), ((), ()))
    chains = [(h, mp) for h in range(N_DIFF_HEADS) for mp in range(2)]

    va = vd + ATTN_ONES_ROWS

    @pl.when(qi == 0)
    def _():
        for h in range(N_DIFF_HEADS):
            vt_s[h * va + vd:(h + 1) * va, :] = jnp.ones((ATTN_ONES_ROWS, seq), BF16)
            for c in range(seq // tq):
                vt_s[h * va:h * va + vd, c * tq:(c + 1) * tq] = (
                    v_ref[c * tq:(c + 1) * tq, h * vd:(h + 1) * vd].astype(F32).T.astype(BF16))

    acc_s[...] = jnp.zeros_like(acc_s)
    lane = lax.broadcasted_iota(jnp.int32, (tq, vd), 1)
    qz = []
    for h, mp in chains:
        qh = q_ref[:, h * vd:(h + 1) * vd]
        keep = (lane < DIFF_HEAD_DIM) if mp == 0 else (lane >= DIFF_HEAD_DIM)
        qz.append(jnp.where(keep, qh, jnp.zeros_like(qh)))

    def fold(block, alphas):
        row0 = pl.multiple_of(block * tk, tk)
        pvs = [jnp.dot(vt_s[h * va:(h + 1) * va, pl.ds(row0, tk)], p_s[c], preferred_element_type=F32)
               for c, (h, _) in enumerate(chains)]
        for c, pv in enumerate(pvs):
            acc_s[c] = alphas[c] * acc_s[c] + pv

    def step(j, carry, diag):
        ms, alphas = carry
        row0 = pl.multiple_of(j * tk, tk)
        scores = [lax.dot_general(k_ref[pl.ds(row0, tk), h * vd:(h + 1) * vd], qz[c], nt,
                                  preferred_element_type=F32) for c, (h, _) in enumerate(chains)]
        fold(jnp.maximum(j - 1, 0), alphas)
        new_ms, new_alphas = [], []
        for c, s in enumerate(scores):
            if diag:
                kidx = lax.broadcasted_iota(jnp.int32, s.shape, 0)
                qidx = lax.broadcasted_iota(jnp.int32, s.shape, 1)
                s = jnp.where(kidx <= qidx, s, -jnp.inf)
            m_new = jnp.maximum(ms[c], jnp.max(s, axis=0, keepdims=True))
            new_alphas.append(jnp.exp2(ms[c] - m_new))
            p_s[c] = jnp.exp2(s - m_new).astype(BF16)
            new_ms.append(m_new)
        return tuple(new_ms), tuple(new_alphas)

    p_s[...] = jnp.zeros_like(p_s)
    init = (tuple(jnp.full((1, tq), -jnp.inf, F32) for _ in chains),
            tuple(jnp.ones((1, tq), F32) for _ in chains))
    carry = lax.fori_loop(0, qi, lambda j, c: step(j, c, False), init)
    _, alphas = step(qi, carry, True)
    fold(qi, alphas)

    lam = (jnp.exp(jnp.sum(lq1_ref[...] * lk1_ref[...], axis=-1, keepdims=True))
           - jnp.exp(jnp.sum(lq2_ref[...] * lk2_ref[...], axis=-1, keepdims=True)) + lambda_init)
    sw = jnp.concatenate([sw_ref[...]] * (tq // LANES), axis=1)
    for h in range(N_DIFF_HEADS):
        inv1 = 1.0 / acc_s[2 * h, vd:vd + 1, :]
        inv2 = 1.0 / acc_s[2 * h + 1, vd:vd + 1, :]
        o = acc_s[2 * h, :vd, :] * inv1 - lam * (acc_s[2 * h + 1, :vd, :] * inv2)
        o = o * lax.rsqrt(jnp.mean(o * o, axis=0, keepdims=True) + SUBLN_EPS)
        o_ref[:, h * vd:(h + 1) * vd] = (o * sw * (1.0 - lambda_init)).T.astype(BF16)


def _attn(q, k, v, lq1, lk1, lq2, lk2, subln_w, lambda_init):
    b, s, width = q.shape
    tq = min(ATTN_TILE, s)
    vec = lambda a: a.reshape(1, -1)
    small = lambda w: pl.BlockSpec((1, w), lambda bi, i: (0, 0))
    kv_spec = pl.BlockSpec((None, s, width), lambda bi, i: (bi, 0, 0))
    q_spec = pl.BlockSpec((None, tq, width), lambda bi, i: (bi, i, 0))
    sw_cols = jnp.broadcast_to(subln_w.reshape(DIFF_V_DIM, 1), (DIFF_V_DIM, LANES))
    return pl.pallas_call(
        functools.partial(_attn_kernel, lambda_init=lambda_init),
        out_shape=jax.ShapeDtypeStruct(q.shape, BF16),
        grid=(b, s // tq),
        in_specs=[q_spec, kv_spec, kv_spec, small(DIFF_HEAD_DIM), small(DIFF_HEAD_DIM),
                  small(DIFF_HEAD_DIM), small(DIFF_HEAD_DIM),
                  pl.BlockSpec((DIFF_V_DIM, LANES), lambda bi, i: (0, 0))],
        out_specs=q_spec,
        scratch_shapes=[pltpu.VMEM((N_DIFF_HEADS * (DIFF_V_DIM + ATTN_ONES_ROWS), s), BF16),
                        pltpu.VMEM((2 * N_DIFF_HEADS, DIFF_V_DIM + ATTN_ONES_ROWS, tq), F32),
                        pltpu.VMEM((2 * N_DIFF_HEADS, tq, tq), BF16)],
        compiler_params=_params(("arbitrary",) * 2),
        name="attn",
    )(q, k, v, vec(lq1), vec(lk1), vec(lq2), vec(lk2), sw_cols)


def _hgrn_kernel(q_ref, f_ref, i_ref, g_ref, lbl_ref, gw_ref, o_ref, qd_s, oi_s, u_s, d_s, *, layer):
    seq = q_ref.shape[0]
    ck, sub = HGRN_CHUNK, HGRN_SUB
    lg = lbl_ref[...]
    e = jnp.exp(lg - jnp.max(lg, axis=0, keepdims=True))
    lb = jnp.sum(e[:layer + 1], axis=0, keepdims=True) / jnp.sum(e, axis=0, keepdims=True)
    gw = gw_ref[...]
    causal = (lax.broadcasted_iota(jnp.int32, (ck, ck), 1)
              <= lax.broadcasted_iota(jnp.int32, (ck, ck), 0))
    tril = causal.astype(BF16)
    nt = (((1,), (1,)), ((), ()))
    tn = (((0,), (0,)), ((), ()))
    w = HGRN_EXPAND
    nsub = ck // sub

    def phase1(c2, _):
        chunks = [c2 * HGRN_UNROLL + u for u in range(HGRN_UNROLL)]
        rows = [pl.ds(pl.multiple_of(c * ck, ck), ck) for c in chunks]
        kins, splits = [], []
        for r in rows:
            forget = lb + (1.0 - lb) * jax.nn.sigmoid(f_ref[r, :])
            kins.append(1.0 - forget)
            g = jnp.log(forget)
            g_hi = g.astype(BF16)
            r1 = g - g_hi.astype(F32)
            g_mid = r1.astype(BF16)
            g_lo = (r1 - g_mid.astype(F32)).astype(BF16)
            splits.append(jnp.concatenate([g_hi, g_mid, g_lo], axis=1))
        sums = [jnp.dot(tril, x, preferred_element_type=F32) for x in splits]

        q_blk, k_blk, kdecs = [], [], []
        for c, r, kin, cs in zip(chunks, rows, kins, sums):
            bsum = cs[:, :w] + cs[:, w:2 * w] + cs[:, 2 * w:]
            qf = _silu(q_ref[r, :].astype(F32))
            qd_s[r, :] = (qf * jnp.exp(bsum)).astype(BF16)
            q_rows, k_cols = [], []
            for i in range(nsub):
                lo, hi = i * sub, (i + 1) * sub
                ref_pt = bsum[lo:lo + 1, :]
                qh = (qf[lo:hi] * jnp.exp(bsum[lo:hi] - ref_pt)).astype(BF16)
                kh = (kin[:hi] * jnp.exp(ref_pt - bsum[:hi])).astype(BF16)
                pieces = [qh if j == i else jnp.zeros((sub, w), BF16) for j in range(nsub)]
                q_rows.append(jnp.concatenate(pieces, axis=1))
                k_cols.append(kh if hi == ck else jnp.concatenate([kh, jnp.zeros((ck - hi, w), BF16)], axis=0))
            q_blk.append(jnp.concatenate(q_rows, axis=0))
            k_blk.append(jnp.concatenate(k_cols, axis=1))
            last = bsum[ck - 1:ck, :]
            kdecs.append((kin * jnp.exp(last - bsum)).astype(BF16))
            d_s[c] = jnp.exp(last)
        vbs = [i_ref[r, :] for r in rows]
        atts = [lax.dot_general(qb, kb, nt, preferred_element_type=F32) for qb, kb in zip(q_blk, k_blk)]
        for c, vb, kdec in zip(chunks, vbs, kdecs):
            u_s[c] = lax.dot_general(vb, kdec, tn, preferred_element_type=F32)
        probs = [jnp.where(causal, att, 0.0).astype(BF16) for att in atts]
        for r, p, vb in zip(rows, probs, vbs):
            oi_s[r, :] = jnp.dot(p, vb, preferred_element_type=F32)
        return 0

    lax.fori_loop(0, seq // (ck * HGRN_UNROLL), phase1, 0)

    def phase2(c2, state_t):
        for u in range(HGRN_UNROLL):
            c = c2 * HGRN_UNROLL + u
            rows = pl.ds(pl.multiple_of(c * ck, ck), ck)
            o = oi_s[rows, :] + lax.dot_general(qd_s[rows, :], state_t.astype(BF16), nt,
                                                preferred_element_type=F32)
            o = _rms(o, RMS_EPS) * gw * _silu(g_ref[rows, :].astype(F32))
            o_ref[rows, :] = o.astype(BF16)
            state_t = state_t * d_s[c] + u_s[c]
        return state_t

    lax.fori_loop(0, seq // (ck * HGRN_UNROLL), phase2, jnp.zeros((HGRN_V_DIM, HGRN_EXPAND), F32))


def _hgrn(hq, hf, hi, hg, lb_logits, gnorm_w, layer):
    b, s, _ = hq.shape
    w = HGRN_EXPAND
    blk = pl.BlockSpec((None, s, w), lambda bi, h: (bi, 0, h))
    return pl.pallas_call(
        functools.partial(_hgrn_kernel, layer=layer),
        out_shape=jax.ShapeDtypeStruct(hq.shape, BF16),
        grid=(b, N_HGRN_HEADS),
        in_specs=[blk, blk, blk, blk,
                  pl.BlockSpec((lb_logits.shape[0], w), lambda bi, h: (0, h)),
                  pl.BlockSpec((1, w), lambda bi, h: (0, 0))],
        out_specs=blk,
        scratch_shapes=[pltpu.VMEM((s, w), BF16), pltpu.VMEM((s, HGRN_V_DIM), F32),
                        pltpu.VMEM((s // HGRN_CHUNK, HGRN_V_DIM, w), F32),
                        pltpu.VMEM((s // HGRN_CHUNK, 1, w), F32)],
        compiler_params=_params(("arbitrary",) * 2),
        name="hgrn",
    )(hq, hf, hi, hg, lb_logits, gnorm_w.reshape(1, w))


def _outproj_kernel(a_ref, r_ref, x_ref, g1_ref, sc_ref, sh_ref, nw_ref, wo_ref, wr_ref, br_ref,
                    x1_ref, h2_ref, route_ref, cnt_ref, run_ref):
    i = pl.program_id(0)
    tm = x_ref.shape[0]
    half = a_ref.shape[1]

    @pl.when(i == 0)
    def _():
        run_ref[...] = jnp.zeros_like(run_ref)

    mix = (jnp.dot(a_ref[...], wo_ref[:half, :], preferred_element_type=F32)
           + jnp.dot(r_ref[...], wo_ref[half:, :], preferred_element_type=F32))
    x1 = x_ref[...] + g1_ref[...] * mix
    x1_ref[...] = x1
    h2 = _rms(x1, RMS_EPS) * nw_ref[...]
    h2 = h2 * (1.0 + sc_ref[...]) + sh_ref[...]
    h2_ref[...] = h2

    h_hi = h2.astype(BF16)
    h_lo = (h2 - h_hi.astype(F32)).astype(BF16)
    part = (jnp.dot(h_hi, wr_ref[...], preferred_element_type=F32)
            + jnp.dot(h_lo, wr_ref[...], preferred_element_type=F32))
    logits = part[:, :LANES] + part[:, LANES:] + br_ref[...]
    lane = lax.broadcasted_iota(jnp.int32, logits.shape, 1)
    big = jnp.int32(LANES)
    neg = -jnp.inf
    is_g = lane < N_GROUPS
    gl = jnp.where(is_g, logits, neg)
    gmax = jnp.max(gl, axis=-1, keepdims=True)
    gsel = jnp.min(jnp.where(gl == gmax, lane, big), axis=-1, keepdims=True)
    pg = 1.0 / jnp.sum(jnp.where(is_g, jnp.exp(logits - gmax), 0.0), axis=-1, keepdims=True)
    lo = ROUTER_LANE0 + EXPERTS_PER_GROUP * gsel
    el = jnp.where((lane >= lo) & (lane < lo + EXPERTS_PER_GROUP), logits, neg)
    v0 = jnp.max(el, axis=-1, keepdims=True)
    i0 = jnp.min(jnp.where(el == v0, lane, big), axis=-1, keepdims=True)
    el = jnp.where(lane == i0, neg, el)
    v1 = jnp.max(el, axis=-1, keepdims=True)
    i1 = jnp.min(jnp.where(el == v1, lane, big), axis=-1, keepdims=True)
    t = jnp.exp(v1 - v0)
    w0 = pg / (1.0 + t)
    w1 = pg * t / (1.0 + t)
    hit0 = lane == i0
    hit1 = lane == i1
    onehot = jnp.where(hit0 | hit1, 1.0, 0.0)
    r_i = lax.broadcasted_iota(jnp.int32, (tm, tm), 0)
    c_i = lax.broadcasted_iota(jnp.int32, (tm, tm), 1)
    before = jnp.where(c_i < r_i, 1.0, 0.0).astype(BF16)
    prefix = jnp.dot(before, onehot.astype(BF16), preferred_element_type=F32) + run_ref[...]
    rank0 = jnp.sum(jnp.where(hit0, prefix, 0.0), axis=-1, keepdims=True)
    rank1 = jnp.sum(jnp.where(hit1, prefix, 0.0), axis=-1, keepdims=True)
    run = run_ref[...] + jnp.sum(onehot, axis=0, keepdims=True)
    run_ref[...] = run
    cnt_ref[...] = run
    e0 = (i0 - ROUTER_LANE0).astype(F32)
    e1 = (i1 - ROUTER_LANE0).astype(F32)
    slab = jnp.zeros(logits.shape, F32)
    for ln, val in enumerate((e0, e1, w0, w1, rank0, rank1)):
        slab = jnp.where(lane == ln, val, slab)
    route_ref[...] = slab


def _outproj(a2, r2, x2, mod4, norm_w, w_out_bf, w_router, b_router, seq):
    n, d = x2.shape
    tm = min(OUTPROJ_TILE, seq)
    tpb = seq // tm
    row = lambda i: (i, 0)
    fixed = lambda i: (0, 0)
    modspec = lambda j: pl.BlockSpec((None, None, 1, d), lambda i: (i // tpb, j, 0, 0))
    return pl.pallas_call(
        _outproj_kernel,
        out_shape=(jax.ShapeDtypeStruct((n, d), F32), jax.ShapeDtypeStruct((n, d), F32),
                   jax.ShapeDtypeStruct((n, LANES), F32), jax.ShapeDtypeStruct((1, LANES), F32)),
        grid=(n // tm,),
        in_specs=[pl.BlockSpec((tm, a2.shape[1]), row), pl.BlockSpec((tm, r2.shape[1]), row),
                  pl.BlockSpec((tm, d), row), modspec(2), modspec(4), modspec(3),
                  pl.BlockSpec((1, d), fixed), pl.BlockSpec(w_out_bf.shape, fixed),
                  pl.BlockSpec(w_router.shape, fixed), pl.BlockSpec((1, LANES), fixed)],
        out_specs=(pl.BlockSpec((tm, d), row), pl.BlockSpec((tm, d), row),
                   pl.BlockSpec((tm, LANES), row), pl.BlockSpec((1, LANES), fixed)),
        scratch_shapes=[pltpu.VMEM((1, LANES), F32)],
        compiler_params=_params(("arbitrary",)),
        name="outproj",
    )(a2, r2, x2, mod4, mod4, mod4, norm_w.reshape(1, d), w_out_bf, w_router, b_router)


def _row_copy(src_ref, src_row, dst_ref, dst_row, sem):
    return pltpu.make_async_copy(src_ref.at[pl.ds(src_row, 1), :], dst_ref.at[pl.ds(dst_row, 1), :], sem)


def _dispatch_kernel(pad_ref, len_ref, nu_ref, dest_ref, h_ref, xs_ref, zero_s, sem):
    tm = h_ref.shape[0]

    @pl.when(pl.program_id(0) == 0)
    def _():
        zero_s[...] = jnp.zeros_like(zero_s)
        tails = []
        for e in range(N_EXPERTS):
            pos = pad_ref[e]
            for piece in PAD_PIECES:
                hit = (len_ref[e] & piece) != 0
                tails.append((hit, pltpu.make_async_copy(
                    zero_s.at[pl.ds(0, piece), :],
                    xs_ref.at[pl.ds(pl.multiple_of(pos, SUBLANES), piece), :], sem)))
                pos = pos + jnp.where(hit, piece, 0)
        n_tiles = xs_ref.shape[0] // EXPERT_TILE
        for t in range(N_EXPERTS):
            tile = nu_ref[0] + t
            row0 = pl.multiple_of(jnp.minimum(tile, n_tiles - 1) * EXPERT_TILE, EXPERT_TILE)
            tails.append((tile < n_tiles, pltpu.make_async_copy(
                zero_s, xs_ref.at[pl.ds(row0, EXPERT_TILE), :], sem)))
        for hit, cp in tails:
            pl.when(hit)(cp.start)
        for hit, cp in tails:
            pl.when(hit)(cp.wait)

    def issue(g, _):
        for u in range(DMA_UNROLL):
            r = g * DMA_UNROLL + u
            _row_copy(h_ref, r, xs_ref, dest_ref[0, r], sem).start()
            _row_copy(h_ref, r, xs_ref, dest_ref[0, tm + r], sem).start()
        return 0

    lax.fori_loop(0, tm // DMA_UNROLL, issue, 0)

    def drain(g, _):
        for _u in range(2 * DMA_UNROLL):
            _row_copy(h_ref, 0, xs_ref, 0, sem).wait()
        return 0

    lax.fori_loop(0, tm // DMA_UNROLL, drain, 0)


def _dispatch(plan, h2):
    n, d = h2.shape
    dest3 = plan["dest3"]
    tm = dest3.shape[2] // 2
    rows = plan["tile_expert"].shape[0] * EXPERT_TILE
    grid_spec = pltpu.PrefetchScalarGridSpec(
        num_scalar_prefetch=3,
        grid=(n // tm,),
        in_specs=[pl.BlockSpec((None, 1, 2 * tm), lambda i, pad, ln, nu: (i, 0, 0), memory_space=pltpu.SMEM),
                  pl.BlockSpec((tm, d), lambda i, pad, ln, nu: (i, 0))],
        out_specs=pl.BlockSpec(memory_space=pl.ANY),
        scratch_shapes=[pltpu.VMEM((EXPERT_TILE, d), F32), pltpu.SemaphoreType.DMA],
    )
    return pl.pallas_call(
        _dispatch_kernel,
        out_shape=jax.ShapeDtypeStruct((rows, d), F32),
        grid_spec=grid_spec,
        compiler_params=_params(("arbitrary",), has_side_effects=True),
        name="dispatch",
    )(plan["pad_start"], plan["pad_len"], plan["n_used"], dest3, h2)


def _experts_kernel(te_ref, nu_ref, seg_ref, nxt_ref, xs_ref, wg_hbm, wu_hbm, wd_hbm, ys_ref,
                    wg_f, wu_f, wd_f, wg_s, wu_s, wd_s, sem):
    i = pl.program_id(0)

    def fetch(e, sl):
        return [pltpu.make_async_copy(src.at[e], dst.at[sl], sem.at[sl])
                for src, dst in ((wg_hbm, wg_f), (wu_hbm, wu_f), (wd_hbm, wd_f))]

    @pl.when(i < nu_ref[0])
    def _():
        prev = te_ref[jnp.maximum(i - 1, 0)]

        @pl.when(i == 0)
        def _():
            for cp in fetch(te_ref[0], 0):
                cp.start()

        @pl.when((i == 0) | (te_ref[i] != prev))
        def _():
            sl = seg_ref[i] % 2
            for cp in fetch(te_ref[i], sl):
                cp.wait()

            @pl.when(nxt_ref[i] >= 0)
            def _():
                for cp in fetch(nxt_ref[i], 1 - sl):
                    cp.start()

            wg_s[...] = wg_f[sl].astype(BF16)
            wu_s[...] = wu_f[sl].astype(BF16)
            wd_s[...] = wd_f[sl].astype(BF16)

        x = xs_ref[...].astype(BF16)
        a = jnp.dot(x, wg_s[...], preferred_element_type=F32)
        u = jnp.dot(x, wu_s[...], preferred_element_type=F32)
        hid = (_silu(a) * u).astype(BF16)
        ys_ref[...] = jnp.dot(hid, wd_s[...], preferred_element_type=F32)

    @pl.when(i >= nu_ref[0])
    def _():
        ys_ref[...] = jnp.zeros_like(ys_ref)


def _experts(plan, xs, w_gate, w_up, w_down):
    d = xs.shape[1]
    e, _, f = w_gate.shape
    tm = EXPERT_TILE
    n_tiles = plan["tile_expert"].shape[0]
    row = lambda i, te, nu, seg, nxt: (jnp.minimum(i, nu[0] - 1), 0)
    hbm = pl.BlockSpec(memory_space=pl.ANY)
    grid_spec = pltpu.PrefetchScalarGridSpec(
        num_scalar_prefetch=4,
        grid=(n_tiles,),
        in_specs=[pl.BlockSpec((tm, d), row), hbm, hbm, hbm],
        out_specs=pl.BlockSpec((tm, d), lambda i, te, nu, seg, nxt: (i, 0)),
        scratch_shapes=[pltpu.VMEM((2, d, f), F32), pltpu.VMEM((2, d, f), F32), pltpu.VMEM((2, f, d), F32),
                        pltpu.VMEM((d, f), BF16), pltpu.VMEM((d, f), BF16), pltpu.VMEM((f, d), BF16),
                        pltpu.SemaphoreType.DMA((2,))],
    )
    return pl.pallas_call(
        _experts_kernel,
        out_shape=jax.ShapeDtypeStruct((n_tiles * tm, d), F32),
        grid_spec=grid_spec,
        compiler_params=_params(("arbitrary",)),
        name="experts",
    )(plan["tile_expert"], plan["n_used"], plan["segment"], plan["next_expert"], xs, w_gate, w_up, w_down)


def _combine_kernel(dest_ref, dnext_ref, ys_ref, x1_ref, route_ref, g2_ref, fw_ref, o_ref, y0_s, y1_s, sem):
    tm = x1_ref.shape[0]
    i = pl.program_id(0)
    slot = i % 2

    def gather(idx_ref, sl):
        def issue(g, _):
            for u in range(DMA_UNROLL):
                r = g * DMA_UNROLL + u
                _row_copy(ys_ref, idx_ref[0, r], y0_s.at[sl], r, sem.at[sl]).start()
                _row_copy(ys_ref, idx_ref[0, tm + r], y1_s.at[sl], r, sem.at[sl]).start()
            return 0

        lax.fori_loop(0, tm // DMA_UNROLL, issue, 0)

    @pl.when(i == 0)
    def _():
        gather(dest_ref, 0)

    @pl.when(i + 1 < pl.num_programs(0))
    def _():
        gather(dnext_ref, 1 - slot)

    def drain(g, _):
        for _u in range(DMA_UNROLL):
            _row_copy(ys_ref, 0, y0_s.at[slot], 0, sem.at[slot]).wait()
            _row_copy(ys_ref, 0, y1_s.at[slot], 0, sem.at[slot]).wait()
        return 0

    lax.fori_loop(0, tm // DMA_UNROLL, drain, 0)
    route = route_ref[...]
    w0 = route[:, 2:3]
    w1 = route[:, 3:4]
    x = x1_ref[...] + g2_ref[...] * (w0 * y0_s[slot] + w1 * y1_s[slot])
    o_ref[...] = _rms(x, RMS_EPS) * fw_ref[...]


def _combine(dest3, ys, x1, route, mod4, final_w, seq):
    n, d = x1.shape
    tm = dest3.shape[2] // 2
    tpb = seq // tm
    row = lambda i: (i, 0)
    steps = n // tm
    return pl.pallas_call(
        _combine_kernel,
        out_shape=jax.ShapeDtypeStruct((n, d), F32),
        grid=(steps,),
        in_specs=[pl.BlockSpec((None, 1, 2 * tm), lambda i: (i, 0, 0), memory_space=pltpu.SMEM),
                  pl.BlockSpec((None, 1, 2 * tm), lambda i: (jnp.minimum(i + 1, steps - 1), 0, 0),
                               memory_space=pltpu.SMEM),
                  pl.BlockSpec(memory_space=pl.ANY),
                  pl.BlockSpec((tm, d), row), pl.BlockSpec((tm, LANES), row),
                  pl.BlockSpec((None, None, 1, d), lambda i: (i // tpb, 5, 0, 0)),
                  pl.BlockSpec((1, d), lambda i: (0, 0))],
        out_specs=pl.BlockSpec((tm, d), row),
        scratch_shapes=[pltpu.VMEM((2, tm, d), F32), pltpu.VMEM((2, tm, d), F32),
                        pltpu.SemaphoreType.DMA((2,))],
        compiler_params=_params(("arbitrary",)),
        name="combine",
    )(dest3, dest3, ys, x1, route, mod4, final_w.reshape(1, d))


def _dest_kernel(route_ref, offs_ref, o_ref):
    route = route_ref[...]
    lane = lax.broadcasted_iota(jnp.int32, route.shape, 1)
    offs = offs_ref[...]
    slab = jnp.zeros(route.shape, F32)
    for k in range(2):
        e_lane = route[:, k:k + 1].astype(jnp.int32) + ROUTER_LANE0
        off = jnp.sum(jnp.where(lane == e_lane, offs, 0.0), axis=-1, keepdims=True)
        slab = jnp.where(lane == k, off + route[:, 4 + k:5 + k], slab)
    o_ref[...] = slab.T[:8, :].astype(jnp.int32)


def _dest(route, offs_row, tm):
    n = route.shape[0]
    blk = min(n, 2048)
    out = pl.pallas_call(
        _dest_kernel,
        out_shape=jax.ShapeDtypeStruct((8, n), jnp.int32),
        grid=(n // blk,),
        in_specs=[pl.BlockSpec((blk, LANES), lambda i: (i, 0)), pl.BlockSpec((1, LANES), lambda i: (0, 0))],
        out_specs=pl.BlockSpec((8, blk), lambda i: (0, i)),
        compiler_params=_params(("arbitrary",)),
        name="dest",
    )(route, offs_row)
    return jnp.concatenate([out[0].reshape(n // tm, 1, tm), out[1].reshape(n // tm, 1, tm)], axis=2)


def _plan(route, counts_row, seq):
    n = route.shape[0]
    tm = min(ROW_TILE, seq)
    counts = counts_row[0, ROUTER_LANE0:ROUTER_LANE0 + N_EXPERTS].astype(jnp.int32)
    tiles = (counts + EXPERT_TILE - 1) // EXPERT_TILE
    tile_end = jnp.cumsum(tiles)
    offs = (tile_end - tiles) * EXPERT_TILE
    offs_row = jnp.zeros((1, LANES), F32).at[0, ROUTER_LANE0:ROUTER_LANE0 + N_EXPERTS].set(offs.astype(F32))
    dest3 = _dest(route, offs_row, tm)
    n_tiles = (2 * n) // EXPERT_TILE + N_EXPERTS
    n_used = tile_end[-1]
    tile_ids = jnp.minimum(jnp.arange(n_tiles, dtype=jnp.int32), n_used - 1)
    tile_expert = jnp.sum(tile_ids[:, None] >= tile_end[None, :], axis=1).astype(jnp.int32)
    first = jnp.concatenate([jnp.ones((1,), jnp.int32),
                             (tile_expert[1:] != tile_expert[:-1]).astype(jnp.int32)])
    segment = jnp.cumsum(first) - 1
    ids = jnp.arange(N_EXPERTS, dtype=jnp.int32)
    later = (ids[None, :] > ids[:, None]) & (tiles[None, :] > 0)
    next_of = jnp.min(jnp.where(later, ids[None, :], N_EXPERTS), axis=1)
    next_of = jnp.where(next_of == N_EXPERTS, -1, next_of)
    pad_start = (offs + counts) // SUBLANES * SUBLANES
    return {"dest3": dest3, "tile_expert": tile_expert, "n_used": n_used.reshape(1).astype(jnp.int32),
            "segment": segment.astype(jnp.int32), "next_expert": next_of[tile_expert].astype(jnp.int32),
            "pad_start": pad_start.astype(jnp.int32),
            "pad_len": (tile_end * EXPERT_TILE - pad_start).astype(jnp.int32)}


def kernel(x, c, positions, w_ada, b_ada, norm1_w, w_in, lambda_q1, lambda_k1, lambda_q2, lambda_k2, subln_w, hgrn_lb_logits, gnorm_w, w_out, norm2_w, w_group_router, b_group_router, w_expert_router, b_expert_router, w_gate, w_up, w_down, final_norm_w):
    b, s, d = x.shape
    n = b * s
    depth = w_ada.shape[0]
    assert depth == 1, "the combine kernel applies the final norm, so exactly one layer is supported"
    half = DIFF_HEAD_DIM // 2

    inv_freq = ROPE_THETA ** (-jnp.arange(half, dtype=F32) / half)
    ang = positions.astype(F32).reshape(n, 1) * inv_freq[None, :]
    cos_t, sin_t = _trig(ang.reshape(n * half // LANES, LANES))
    cos_t = cos_t.reshape(n, half)
    sin_t = sin_t.reshape(n, half)
    cos = jnp.tile(cos_t, (1, LANES // half))
    sin = jnp.tile(jnp.concatenate([-sin_t, sin_t], axis=1), (1, LANES // DIFF_HEAD_DIM))

    xf = x.reshape(n, d)
    for l in range(depth):
        lambda_init = 0.8 - 0.6 * math.exp(-0.3 * l)
        mod4 = _mod(c, w_ada[l], b_ada[l]).reshape(b, 6, 1, d)
        q, k, v, hq, hf, hi, hg = _inproj(xf, mod4, norm1_w[l], w_in[l].astype(BF16), cos, sin, s)
        to3 = lambda t: t.reshape(b, s, SEG)
        a = _attn(to3(q), to3(k), to3(v), lambda_q1[l], lambda_k1[l], lambda_q2[l], lambda_k2[l],
                  subln_w[l], lambda_init)
        r = _hgrn(to3(hq), to3(hf), to3(hi), to3(hg), hgrn_lb_logits, gnorm_w[l], l)

        w_router = jnp.zeros((d, LANES), F32)
        w_router = w_router.at[:, :N_GROUPS].set(w_group_router[l])
        w_router = w_router.at[:, ROUTER_LANE0:ROUTER_LANE0 + N_EXPERTS].set(w_expert_router[l])
        b_router = jnp.zeros((1, LANES), F32)
        b_router = b_router.at[0, :N_GROUPS].set(b_group_router[l])
        b_router = b_router.at[0, ROUTER_LANE0:ROUTER_LANE0 + N_EXPERTS].set(b_expert_router[l].reshape(-1))
        wr_hi = w_router.astype(BF16)
        wr_split = jnp.concatenate([wr_hi, (w_router - wr_hi.astype(F32)).astype(BF16)], axis=1)
        x1, h2, route, counts = _outproj(a.reshape(n, SEG), r.reshape(n, SEG), xf, mod4, norm2_w[l],
                                         w_out[l].astype(BF16), wr_split, b_router, s)

        plan = _plan(route, counts, s)
        xs = _dispatch(plan, h2)
        e_w = lambda w: w.reshape((N_EXPERTS,) + w.shape[2:])
        ys = _experts(plan, xs, e_w(w_gate[l]), e_w(w_up[l]), e_w(w_down[l]))
        xf = _combine(plan["dest3"], ys, x1, route, mod4, final_norm_w, s)
    return xf.reshape(b, s, d)
```

```python
import functools
import math

import jax
import jax.numpy as jnp
from jax import lax
from jax.experimental import pallas as pl
from jax.experimental.pallas import tpu as pltpu

F32 = jnp.float32
BF16 = jnp.bfloat16
HIGHEST = lax.Precision.HIGHEST

N_DIFF_HEADS = 4
DIFF_HEAD_DIM = 64
DIFF_V_DIM = 2 * DIFF_HEAD_DIM
N_HGRN_HEADS = 4
HGRN_EXPAND = 128
HGRN_V_DIM = 128
ROPE_THETA = 10000.0
N_GROUPS = 4
EXPERTS_PER_GROUP = 8
N_EXPERTS = N_GROUPS * EXPERTS_PER_GROUP
RMS_EPS = 1e-6
SUBLN_EPS = 1e-5
SEG = 512
LANES = 128
SUBLANES = 8
ROUTER_LANE0 = N_GROUPS

ROW_TILE = 256
OUTPROJ_TILE = 512
ATTN_TILE = 256
ATTN_ONES_ROWS = 16
LOG2E = math.log2(math.e)
HGRN_CHUNK = 64
HGRN_SUB = 16
HGRN_UNROLL = 8
EXPERT_TILE = 256
PAD_PIECES = (256, 128, 64, 32, 16, 8)
DMA_UNROLL = 16
VMEM_LIMIT = 48 * 1024 * 1024


def _params(sem, **kw):
    return pltpu.CompilerParams(dimension_semantics=sem, vmem_limit_bytes=VMEM_LIMIT, **kw)


def _silu(x):
    return x * jax.nn.sigmoid(x)


def _rms(x, eps):
    return x * lax.rsqrt(jnp.mean(x * x, axis=-1, keepdims=True) + eps)


def _mod_kernel(c_ref, w_ref, b_ref, o_ref):
    ca = _silu(c_ref[...])
    o_ref[...] = jnp.dot(ca, w_ref[...], preferred_element_type=F32, precision=HIGHEST) + b_ref[...]


def _mod(c, w_ada, b_ada):
    bsz, d = c.shape
    n_out = w_ada.shape[1]
    return pl.pallas_call(
        _mod_kernel,
        out_shape=jax.ShapeDtypeStruct((bsz, n_out), F32),
        grid=(n_out // d,),
        in_specs=[pl.BlockSpec((bsz, d), lambda j: (0, 0)),
                  pl.BlockSpec((d, d), lambda j: (0, j)),
                  pl.BlockSpec((1, d), lambda j: (0, j))],
        out_specs=pl.BlockSpec((bsz, d), lambda j: (0, j)),
        compiler_params=_params(("arbitrary",)),
        name="mod",
    )(c, w_ada, b_ada.reshape(1, n_out))


def _trig_kernel(a_ref, cos_ref, sin_ref):
    a = a_ref[...]
    cos_ref[...] = jnp.cos(a)
    sin_ref[...] = jnp.sin(a)


def _trig(ang):
    rows = ang.shape[0]
    blk = min(rows, 512)
    spec = pl.BlockSpec((blk, LANES), lambda i: (i, 0))
    return pl.pallas_call(
        _trig_kernel,
        out_shape=(jax.ShapeDtypeStruct(ang.shape, F32),) * 2,
        grid=(rows // blk,),
        in_specs=[spec],
        out_specs=(spec, spec),
        compiler_params=_params(("arbitrary",)),
        name="trig",
    )(ang)


def _inproj_kernel(x_ref, sc_ref, sh_ref, nw_ref, w_ref, cos_ref, sin_ref,
                   q_ref, k_ref, v_ref, hq_ref, hf_ref, hi_ref, hg_ref):
    h = _rms(x_ref[...], RMS_EPS) * nw_ref[...]
    h = h * (1.0 + sc_ref[...]) + sh_ref[...]
    hb = h.astype(BF16)
    cos = cos_ref[...]
    sin = sin_ref[...]
    lane = lax.broadcasted_iota(jnp.int32, cos.shape, 1)
    first_half = (lane % DIFF_HEAD_DIM) < (DIFF_HEAD_DIM // 2)

    def seg(i):
        return jnp.dot(hb, w_ref[:, i * SEG:(i + 1) * SEG], preferred_element_type=F32)

    def rope(p, scale):
        outs = []
        for c in range(SEG // LANES):
            pc = p[:, c * LANES:(c + 1) * LANES]
            partner = jnp.where(first_half, pltpu.roll(pc, LANES - DIFF_HEAD_DIM // 2, 1),
                                pltpu.roll(pc, DIFF_HEAD_DIM // 2, 1))
            outs.append((pc * cos + partner * sin) * scale)
        return jnp.concatenate(outs, axis=1)

    q_ref[...] = rope(seg(0), DIFF_HEAD_DIM ** -0.5 * LOG2E).astype(BF16)
    k_ref[...] = rope(seg(1), 1.0).astype(BF16)
    v_ref[...] = seg(2).astype(BF16)
    hq_ref[...] = seg(3).astype(BF16)
    hf_ref[...] = seg(4)
    hi_ref[...] = seg(5).astype(BF16)
    hg_ref[...] = seg(6).astype(BF16)


def _inproj(x2, mod4, norm_w, w_in_bf, cos, sin, seq):
    n, d = x2.shape
    tm = min(ROW_TILE, seq)
    tpb = seq // tm
    row = lambda i: (i, 0)
    modspec = lambda j: pl.BlockSpec((None, None, 1, d), lambda i: (i // tpb, j, 0, 0))
    seg_spec = pl.BlockSpec((tm, SEG), row)
    bf = jax.ShapeDtypeStruct((n, SEG), BF16)
    return pl.pallas_call(
        _inproj_kernel,
        out_shape=(bf, bf, bf, bf, jax.ShapeDtypeStruct((n, SEG), F32), bf, bf),
        grid=(n // tm,),
        in_specs=[pl.BlockSpec((tm, d), row), modspec(1), modspec(0),
                  pl.BlockSpec((1, d), lambda i: (0, 0)),
                  pl.BlockSpec(w_in_bf.shape, lambda i: (0, 0)),
                  pl.BlockSpec((tm, LANES), row), pl.BlockSpec((tm, LANES), row)],
        out_specs=(seg_spec,) * 7,
        compiler_params=_params(("arbitrary",)),
        name="inproj",
    )(x2, mod4, mod4, norm_w.reshape(1, d), w_in_bf, cos, sin)


def _attn_kernel(q_ref, k_ref, v_ref, lq1_ref, lk1_ref, lq2_ref, lk2_ref, sw_ref, o_ref,
                 vt_s, acc_s, *, lambda_init):
    tq = q_ref.shape[0]
    seq = k_ref.shape[0]
    tk = tq
    vd = DIFF_V_DIM
    qi = pl.program_id(1)
    nt = (((1,), (1,)), ((), ()))
    chains = [(h, mp) for h in range(N_DIFF_HEADS) for mp in range(2)]

    va = vd + ATTN_ONES_ROWS

    @pl.when(qi == 0)
    def _():
        for h in range(N_DIFF_HEADS):
            vt_s[h * va + vd:(h + 1) * va, :] = jnp.ones((ATTN_ONES_ROWS, seq), BF16)
            for c in range(seq // tq):
                vt_s[h * va:h * va + vd, c * tq:(c + 1) * tq] = (
                    v_ref[c * tq:(c + 1) * tq, h * vd:(h + 1) * vd].astype(F32).T.astype(BF16))

    acc_s[...] = jnp.zeros_like(acc_s)
    lane = lax.broadcasted_iota(jnp.int32, (tq, vd), 1)
    qz = []
    for h, mp in chains:
        qh = q_ref[:, h * vd:(h + 1) * vd]
        keep = (lane < DIFF_HEAD_DIM) if mp == 0 else (lane >= DIFF_HEAD_DIM)
        qz.append(jnp.where(keep, qh, jnp.zeros_like(qh)))

    def step(j, ms, diag):
        row0 = pl.multiple_of(j * tk, tk)
        scores = [lax.dot_general(k_ref[pl.ds(row0, tk), h * vd:(h + 1) * vd], qz[c], nt,
                                  preferred_element_type=F32) for c, (h, _) in enumerate(chains)]
        new_ms, alphas, probs = [], [], []
        for c, s in enumerate(scores):
            if diag:
                kidx = lax.broadcasted_iota(jnp.int32, s.shape, 0)
                qidx = lax.broadcasted_iota(jnp.int32, s.shape, 1)
                s = jnp.where(kidx <= qidx, s, -jnp.inf)
            m_new = jnp.maximum(ms[c], jnp.max(s, axis=0, keepdims=True))
            alphas.append(jnp.exp2(ms[c] - m_new))
            probs.append(jnp.exp2(s - m_new).astype(BF16))
            new_ms.append(m_new)
        pvs = [jnp.dot(vt_s[h * va:(h + 1) * va, pl.ds(row0, tk)], probs[c], preferred_element_type=F32)
               for c, (h, _) in enumerate(chains)]
        for c, pv in enumerate(pvs):
            acc_s[c] = alphas[c] * acc_s[c] + pv
        return tuple(new_ms)

    init = tuple(jnp.full((1, tq), -jnp.inf, F32) for _ in chains)
    ms = lax.fori_loop(0, qi, lambda j, c: step(j, c, False), init)
    step(qi, ms, True)

    lam = (jnp.exp(jnp.sum(lq1_ref[...] * lk1_ref[...], axis=-1, keepdims=True))
           - jnp.exp(jnp.sum(lq2_ref[...] * lk2_ref[...], axis=-1, keepdims=True)) + lambda_init)
    sw = jnp.concatenate([sw_ref[...]] * (tq // LANES), axis=1)
    for h in range(N_DIFF_HEADS):
        inv1 = 1.0 / acc_s[2 * h, vd:vd + 1, :]
        inv2 = 1.0 / acc_s[2 * h + 1, vd:vd + 1, :]
        o = acc_s[2 * h, :vd, :] * inv1 - lam * (acc_s[2 * h + 1, :vd, :] * inv2)
        o = o * lax.rsqrt(jnp.mean(o * o, axis=0, keepdims=True) + SUBLN_EPS)
        o_ref[:, h * vd:(h + 1) * vd] = (o * sw * (1.0 - lambda_init)).T.astype(BF16)


def _attn(q, k, v, lq1, lk1, lq2, lk2, subln_w, lambda_init):
    b, s, width = q.shape
    tq = min(ATTN_TILE, s)
    vec = lambda a: a.reshape(1, -1)
    small = lambda w: pl.BlockSpec((1, w), lambda bi, i: (0, 0))
    kv_spec = pl.BlockSpec((None, s, width), lambda bi, i: (bi, 0, 0))
    q_spec = pl.BlockSpec((None, tq, width), lambda bi, i: (bi, i, 0))
    sw_cols = jnp.broadcast_to(subln_w.reshape(DIFF_V_DIM, 1), (DIFF_V_DIM, LANES))
    return pl.pallas_call(
        functools.partial(_attn_kernel, lambda_init=lambda_init),
        out_shape=jax.ShapeDtypeStruct(q.shape, BF16),
        grid=(b, s // tq),
        in_specs=[q_spec, kv_spec, kv_spec, small(DIFF_HEAD_DIM), small(DIFF_HEAD_DIM),
                  small(DIFF_HEAD_DIM), small(DIFF_HEAD_DIM),
                  pl.BlockSpec((DIFF_V_DIM, LANES), lambda bi, i: (0, 0))],
        out_specs=q_spec,
        scratch_shapes=[pltpu.VMEM((N_DIFF_HEADS * (DIFF_V_DIM + ATTN_ONES_ROWS), s), BF16),
                        pltpu.VMEM((2 * N_DIFF_HEADS, DIFF_V_DIM + ATTN_ONES_ROWS, tq), F32)],
        compiler_params=_params(("arbitrary",) * 2),
        name="attn",
    )(q, k, v, vec(lq1), vec(lk1), vec(lq2), vec(lk2), sw_cols)


def _hgrn_kernel(q_ref, f_ref, i_ref, g_ref, lbl_ref, gw_ref, o_ref, qd_s, oi_s, u_s, d_s, *, layer):
    seq = q_ref.shape[0]
    ck, sub = HGRN_CHUNK, HGRN_SUB
    lg = lbl_ref[...]
    e = jnp.exp(lg - jnp.max(lg, axis=0, keepdims=True))
    lb = jnp.sum(e[:layer + 1], axis=0, keepdims=True) / jnp.sum(e, axis=0, keepdims=True)
    gw = gw_ref[...]
    causal = (lax.broadcasted_iota(jnp.int32, (ck, ck), 1)
              <= lax.broadcasted_iota(jnp.int32, (ck, ck), 0))
    tril = causal.astype(BF16)
    nt = (((1,), (1,)), ((), ()))
    tn = (((0,), (0,)), ((), ()))
    w = HGRN_EXPAND
    nsub = ck // sub

    def phase1(c2, _):
        chunks = [c2 * HGRN_UNROLL + u for u in range(HGRN_UNROLL)]
        rows = [pl.ds(pl.multiple_of(c * ck, ck), ck) for c in chunks]
        kins, splits = [], []
        for r in rows:
            forget = lb + (1.0 - lb) * jax.nn.sigmoid(f_ref[r, :])
            kins.append(1.0 - forget)
            g = jnp.log(forget)
            g_hi = g.astype(BF16)
            r1 = g - g_hi.astype(F32)
            g_mid = r1.astype(BF16)
            g_lo = (r1 - g_mid.astype(F32)).astype(BF16)
            splits.append(jnp.concatenate([g_hi, g_mid, g_lo], axis=1))
        sums = [jnp.dot(tril, x, preferred_element_type=F32) for x in splits]

        q_blk, k_blk, kdecs = [], [], []
        for c, r, kin, cs in zip(chunks, rows, kins, sums):
            bsum = cs[:, :w] + cs[:, w:2 * w] + cs[:, 2 * w:]
            qf = _silu(q_ref[r, :].astype(F32))
            qd_s[r, :] = (qf * jnp.exp(bsum)).astype(BF16)
            q_rows, k_cols = [], []
            for i in range(nsub):
                lo, hi = i * sub, (i + 1) * sub
                ref_pt = bsum[lo:lo + 1, :]
                qh = (qf[lo:hi] * jnp.exp(bsum[lo:hi] - ref_pt)).astype(BF16)
                kh = (kin[:hi] * jnp.exp(ref_pt - bsum[:hi])).astype(BF16)
                pieces = [qh if j == i else jnp.zeros((sub, w), BF16) for j in range(nsub)]
                q_rows.append(jnp.concatenate(pieces, axis=1))
                k_cols.append(kh if hi == ck else jnp.concatenate([kh, jnp.zeros((ck - hi, w), BF16)], axis=0))
            q_blk.append(jnp.concatenate(q_rows, axis=0))
            k_blk.append(jnp.concatenate(k_cols, axis=1))
            last = bsum[ck - 1:ck, :]
            kdecs.append((kin * jnp.exp(last - bsum)).astype(BF16))
            d_s[c] = jnp.exp(last)
        vbs = [i_ref[r, :] for r in rows]
        atts = [lax.dot_general(qb, kb, nt, preferred_element_type=F32) for qb, kb in zip(q_blk, k_blk)]
        for c, vb, kdec in zip(chunks, vbs, kdecs):
            u_s[c] = lax.dot_general(vb, kdec, tn, preferred_element_type=F32)
        probs = [jnp.where(causal, att, 0.0).astype(BF16) for att in atts]
        for r, p, vb in zip(rows, probs, vbs):
            oi_s[r, :] = jnp.dot(p, vb, preferred_element_type=F32)
        return 0

    lax.fori_loop(0, seq // (ck * HGRN_UNROLL), phase1, 0)

    def phase2(c2, state_t):
        for u in range(HGRN_UNROLL):
            c = c2 * HGRN_UNROLL + u
            rows = pl.ds(pl.multiple_of(c * ck, ck), ck)
            o = oi_s[rows, :] + lax.dot_general(qd_s[rows, :], state_t.astype(BF16), nt,
                                                preferred_element_type=F32)
            o = _rms(o, RMS_EPS) * gw * _silu(g_ref[rows, :].astype(F32))
            o_ref[rows, :] = o.astype(BF16)
            state_t = state_t * d_s[c] + u_s[c]
        return state_t

    lax.fori_loop(0, seq // (ck * HGRN_UNROLL), phase2, jnp.zeros((HGRN_V_DIM, HGRN_EXPAND), F32))


def _hgrn(hq, hf, hi, hg, lb_logits, gnorm_w, layer):
    b, s, _ = hq.shape
    w = HGRN_EXPAND
    blk = pl.BlockSpec((None, s, w), lambda bi, h: (bi, 0, h))
    return pl.pallas_call(
        functools.partial(_hgrn_kernel, layer=layer),
        out_shape=jax.ShapeDtypeStruct(hq.shape, BF16),
        grid=(b, N_HGRN_HEADS),
        in_specs=[blk, blk, blk, blk,
                  pl.BlockSpec((lb_logits.shape[0], w), lambda bi, h: (0, h)),
                  pl.BlockSpec((1, w), lambda bi, h: (0, 0))],
        out_specs=blk,
        scratch_shapes=[pltpu.VMEM((s, w), BF16), pltpu.VMEM((s, HGRN_V_DIM), F32),
                        pltpu.VMEM((s // HGRN_CHUNK, HGRN_V_DIM, w), F32),
                        pltpu.VMEM((s // HGRN_CHUNK, 1, w), F32)],
        compiler_params=_params(("arbitrary",) * 2),
        name="hgrn",
    )(hq, hf, hi, hg, lb_logits, gnorm_w.reshape(1, w))


def _outproj_kernel(a_ref, r_ref, x_ref, g1_ref, sc_ref, sh_ref, nw_ref, wo_ref, wr_ref, br_ref,
                    x1_ref, h2_ref, route_ref, cnt_ref, run_ref):
    i = pl.program_id(0)
    tm = x_ref.shape[0]
    half = a_ref.shape[1]

    @pl.when(i == 0)
    def _():
        run_ref[...] = jnp.zeros_like(run_ref)

    mix = (jnp.dot(a_ref[...], wo_ref[:half, :], preferred_element_type=F32)
           + jnp.dot(r_ref[...], wo_ref[half:, :], preferred_element_type=F32))
    x1 = x_ref[...] + g1_ref[...] * mix
    x1_ref[...] = x1
    h2 = _rms(x1, RMS_EPS) * nw_ref[...]
    h2 = h2 * (1.0 + sc_ref[...]) + sh_ref[...]
    h2_ref[...] = h2

    h_hi = h2.astype(BF16)
    h_lo = (h2 - h_hi.astype(F32)).astype(BF16)
    part = (jnp.dot(h_hi, wr_ref[...], preferred_element_type=F32)
            + jnp.dot(h_lo, wr_ref[...], preferred_element_type=F32))
    logits = part[:, :LANES] + part[:, LANES:] + br_ref[...]
    lane = lax.broadcasted_iota(jnp.int32, logits.shape, 1)
    big = jnp.int32(LANES)
    neg = -jnp.inf
    is_g = lane < N_GROUPS
    gl = jnp.where(is_g, logits, neg)
    gmax = jnp.max(gl, axis=-1, keepdims=True)
    gsel = jnp.min(jnp.where(gl == gmax, lane, big), axis=-1, keepdims=True)
    pg = 1.0 / jnp.sum(jnp.where(is_g, jnp.exp(logits - gmax), 0.0), axis=-1, keepdims=True)
    lo = ROUTER_LANE0 + EXPERTS_PER_GROUP * gsel
    el = jnp.where((lane >= lo) & (lane < lo + EXPERTS_PER_GROUP), logits, neg)
    v0 = jnp.max(el, axis=-1, keepdims=True)
    i0 = jnp.min(jnp.where(el == v0, lane, big), axis=-1, keepdims=True)
    el = jnp.where(lane == i0, neg, el)
    v1 = jnp.max(el, axis=-1, keepdims=True)
    i1 = jnp.min(jnp.where(el == v1, lane, big), axis=-1, keepdims=True)
    t = jnp.exp(v1 - v0)
    w0 = pg / (1.0 + t)
    w1 = pg * t / (1.0 + t)
    hit0 = lane == i0
    hit1 = lane == i1
    onehot = jnp.where(hit0 | hit1, 1.0, 0.0)
    r_i = lax.broadcasted_iota(jnp.int32, (tm, tm), 0)
    c_i = lax.broadcasted_iota(jnp.int32, (tm, tm), 1)
    before = jnp.where(c_i < r_i, 1.0, 0.0).astype(BF16)
    prefix = jnp.dot(before, onehot.astype(BF16), preferred_element_type=F32) + run_ref[...]
    rank0 = jnp.sum(jnp.where(hit0, prefix, 0.0), axis=-1, keepdims=True)
    rank1 = jnp.sum(jnp.where(hit1, prefix, 0.0), axis=-1, keepdims=True)
    run = run_ref[...] + jnp.sum(onehot, axis=0, keepdims=True)
    run_ref[...] = run
    cnt_ref[...] = run
    e0 = (i0 - ROUTER_LANE0).astype(F32)
    e1 = (i1 - ROUTER_LANE0).astype(F32)
    slab = jnp.zeros(logits.shape, F32)
    for ln, val in enumerate((e0, e1, w0, w1, rank0, rank1)):
        slab = jnp.where(lane == ln, val, slab)
    route_ref[...] = slab


def _outproj(a2, r2, x2, mod4, norm_w, w_out_bf, w_router, b_router, seq):
    n, d = x2.shape
    tm = min(OUTPROJ_TILE, seq)
    tpb = seq // tm
    row = lambda i: (i, 0)
    fixed = lambda i: (0, 0)
    modspec = lambda j: pl.BlockSpec((None, None, 1, d), lambda i: (i // tpb, j, 0, 0))
    return pl.pallas_call(
        _outproj_kernel,
        out_shape=(jax.ShapeDtypeStruct((n, d), F32), jax.ShapeDtypeStruct((n, d), F32),
                   jax.ShapeDtypeStruct((n, LANES), F32), jax.ShapeDtypeStruct((1, LANES), F32)),
        grid=(n // tm,),
        in_specs=[pl.BlockSpec((tm, a2.shape[1]), row), pl.BlockSpec((tm, r2.shape[1]), row),
                  pl.BlockSpec((tm, d), row), modspec(2), modspec(4), modspec(3),
                  pl.BlockSpec((1, d), fixed), pl.BlockSpec(w_out_bf.shape, fixed),
                  pl.BlockSpec(w_router.shape, fixed), pl.BlockSpec((1, LANES), fixed)],
        out_specs=(pl.BlockSpec((tm, d), row), pl.BlockSpec((tm, d), row),
                   pl.BlockSpec((tm, LANES), row), pl.BlockSpec((1, LANES), fixed)),
        scratch_shapes=[pltpu.VMEM((1, LANES), F32)],
        compiler_params=_params(("arbitrary",)),
        name="outproj",
    )(a2, r2, x2, mod4, mod4, mod4, norm_w.reshape(1, d), w_out_bf, w_router, b_router)


def _row_copy(src_ref, src_row, dst_ref, dst_row, sem):
    return pltpu.make_async_copy(src_ref.at[pl.ds(src_row, 1), :], dst_ref.at[pl.ds(dst_row, 1), :], sem)


def _dispatch_kernel(pad_ref, len_ref, nu_ref, dest_ref, h_ref, xs_ref, zero_s, sem):
    tm = h_ref.shape[0]

    @pl.when(pl.program_id(0) == 0)
    def _():
        zero_s[...] = jnp.zeros_like(zero_s)
        tails = []
        for e in range(N_EXPERTS):
            pos = pad_ref[e]
            for piece in PAD_PIECES:
                hit = (len_ref[e] & piece) != 0
                tails.append((hit, pltpu.make_async_copy(
                    zero_s.at[pl.ds(0, piece), :],
                    xs_ref.at[pl.ds(pl.multiple_of(pos, SUBLANES), piece), :], sem)))
                pos = pos + jnp.where(hit, piece, 0)
        n_tiles = xs_ref.shape[0] // EXPERT_TILE
        for t in range(N_EXPERTS):
            tile = nu_ref[0] + t
            row0 = pl.multiple_of(jnp.minimum(tile, n_tiles - 1) * EXPERT_TILE, EXPERT_TILE)
            tails.append((tile < n_tiles, pltpu.make_async_copy(
                zero_s, xs_ref.at[pl.ds(row0, EXPERT_TILE), :], sem)))
        for hit, cp in tails:
            pl.when(hit)(cp.start)
        for hit, cp in tails:
            pl.when(hit)(cp.wait)

    def issue(g, _):
        for u in range(DMA_UNROLL):
            r = g * DMA_UNROLL + u
            _row_copy(h_ref, r, xs_ref, dest_ref[0, r], sem).start(priority=0)
            _row_copy(h_ref, r, xs_ref, dest_ref[0, tm + r], sem).start(priority=1)
        return 0

    lax.fori_loop(0, tm // DMA_UNROLL, issue, 0)

    def drain(g, _):
        for _u in range(2 * DMA_UNROLL):
            _row_copy(h_ref, 0, xs_ref, 0, sem).wait()
        return 0

    lax.fori_loop(0, tm // DMA_UNROLL, drain, 0)


def _dispatch(plan, h2):
    n, d = h2.shape
    dest3 = plan["dest3"]
    tm = dest3.shape[2] // 2
    rows = plan["tile_expert"].shape[0] * EXPERT_TILE
    grid_spec = pltpu.PrefetchScalarGridSpec(
        num_scalar_prefetch=3,
        grid=(n // tm,),
        in_specs=[pl.BlockSpec((None, 1, 2 * tm), lambda i, pad, ln, nu: (i, 0, 0), memory_space=pltpu.SMEM),
                  pl.BlockSpec((tm, d), lambda i, pad, ln, nu: (i, 0))],
        out_specs=pl.BlockSpec(memory_space=pl.ANY),
        scratch_shapes=[pltpu.VMEM((EXPERT_TILE, d), F32), pltpu.SemaphoreType.DMA],
    )
    return pl.pallas_call(
        _dispatch_kernel,
        out_shape=jax.ShapeDtypeStruct((rows, d), F32),
        grid_spec=grid_spec,
        compiler_params=_params(("arbitrary",), has_side_effects=True),
        name="dispatch",
    )(plan["pad_start"], plan["pad_len"], plan["n_used"], dest3, h2)


def _experts_kernel(te_ref, nu_ref, seg_ref, nxt_ref, xs_ref, wg_hbm, wu_hbm, wd_hbm, ys_ref,
                    wg_f, wu_f, wd_f, wg_s, wu_s, wd_s, sem):
    i = pl.program_id(0)

    def fetch(e, sl):
        return [pltpu.make_async_copy(src.at[e], dst.at[sl], sem.at[sl])
                for src, dst in ((wg_hbm, wg_f), (wu_hbm, wu_f), (wd_hbm, wd_f))]

    @pl.when(i < nu_ref[0])
    def _():
        prev = te_ref[jnp.maximum(i - 1, 0)]

        @pl.when(i == 0)
        def _():
            for cp in fetch(te_ref[0], 0):
                cp.start()

        @pl.when((i == 0) | (te_ref[i] != prev))
        def _():
            sl = seg_ref[i] % 2
            for cp in fetch(te_ref[i], sl):
                cp.wait()

            @pl.when(nxt_ref[i] >= 0)
            def _():
                for cp in fetch(nxt_ref[i], 1 - sl):
                    cp.start()

            wg_s[...] = wg_f[sl].astype(BF16)
            wu_s[...] = wu_f[sl].astype(BF16)
            wd_s[...] = wd_f[sl].astype(BF16)

        x = xs_ref[...].astype(BF16)
        a = jnp.dot(x, wg_s[...], preferred_element_type=F32)
        u = jnp.dot(x, wu_s[...], preferred_element_type=F32)
        hid = (_silu(a) * u).astype(BF16)
        ys_ref[...] = jnp.dot(hid, wd_s[...], preferred_element_type=F32)

    @pl.when(i >= nu_ref[0])
    def _():
        ys_ref[...] = jnp.zeros_like(ys_ref)


def _experts(plan, xs, w_gate, w_up, w_down):
    d = xs.shape[1]
    e, _, f = w_gate.shape
    tm = EXPERT_TILE
    n_tiles = plan["tile_expert"].shape[0]
    row = lambda i, te, nu, seg, nxt: (jnp.minimum(i, nu[0] - 1), 0)
    hbm = pl.BlockSpec(memory_space=pl.ANY)
    grid_spec = pltpu.PrefetchScalarGridSpec(
        num_scalar_prefetch=4,
        grid=(n_tiles,),
        in_specs=[pl.BlockSpec((tm, d), row), hbm, hbm, hbm],
        out_specs=pl.BlockSpec((tm, d), lambda i, te, nu, seg, nxt: (i, 0)),
        scratch_shapes=[pltpu.VMEM((2, d, f), F32), pltpu.VMEM((2, d, f), F32), pltpu.VMEM((2, f, d), F32),
                        pltpu.VMEM((d, f), BF16), pltpu.VMEM((d, f), BF16), pltpu.VMEM((f, d), BF16),
                        pltpu.SemaphoreType.DMA((2,))],
    )
    return pl.pallas_call(
        _experts_kernel,
        out_shape=jax.ShapeDtypeStruct((n_tiles * tm, d), F32),
        grid_spec=grid_spec,
        compiler_params=_params(("arbitrary",)),
        name="experts",
    )(plan["tile_expert"], plan["n_used"], plan["segment"], plan["next_expert"], xs, w_gate, w_up, w_down)


def _combine_kernel(dest_ref, dnext_ref, ys_ref, x1_ref, route_ref, g2_ref, fw_ref, o_ref, y0_s, y1_s, sem):
    tm = x1_ref.shape[0]
    i = pl.program_id(0)
    slot = i % 2

    def gather(idx_ref, sl):
        def issue(g, _):
            for u in range(DMA_UNROLL):
                r = g * DMA_UNROLL + u
                _row_copy(ys_ref, idx_ref[0, r], y0_s.at[sl], r, sem.at[sl]).start(priority=0)
                _row_copy(ys_ref, idx_ref[0, tm + r], y1_s.at[sl], r, sem.at[sl]).start(priority=1)
            return 0

        lax.fori_loop(0, tm // DMA_UNROLL, issue, 0)

    @pl.when(i == 0)
    def _():
        gather(dest_ref, 0)

    @pl.when(i + 1 < pl.num_programs(0))
    def _():
        gather(dnext_ref, 1 - slot)

    def drain(g, _):
        for _u in range(DMA_UNROLL):
            _row_copy(ys_ref, 0, y0_s.at[slot], 0, sem.at[slot]).wait()
            _row_copy(ys_ref, 0, y1_s.at[slot], 0, sem.at[slot]).wait()
        return 0

    lax.fori_loop(0, tm // DMA_UNROLL, drain, 0)
    route = route_ref[...]
    w0 = route[:, 2:3]
    w1 = route[:, 3:4]
    x = x1_ref[...] + g2_ref[...] * (w0 * y0_s[slot] + w1 * y1_s[slot])
    o_ref[...] = _rms(x, RMS_EPS) * fw_ref[...]


def _combine(dest3, ys, x1, route, mod4, final_w, seq):
    n, d = x1.shape
    tm = dest3.shape[2] // 2
    tpb = seq // tm
    row = lambda i: (i, 0)
    steps = n // tm
    return pl.pallas_call(
        _combine_kernel,
        out_shape=jax.ShapeDtypeStruct((n, d), F32),
        grid=(steps,),
        in_specs=[pl.BlockSpec((None, 1, 2 * tm), lambda i: (i, 0, 0), memory_space=pltpu.SMEM),
                  pl.BlockSpec((None, 1, 2 * tm), lambda i: (jnp.minimum(i + 1, steps - 1), 0, 0),
                               memory_space=pltpu.SMEM),
                  pl.BlockSpec(memory_space=pl.ANY),
                  pl.BlockSpec((tm, d), row), pl.BlockSpec((tm, LANES), row),
                  pl.BlockSpec((None, None, 1, d), lambda i: (i // tpb, 5, 0, 0)),
                  pl.BlockSpec((1, d), lambda i: (0, 0))],
        out_specs=pl.BlockSpec((tm, d), row),
        scratch_shapes=[pltpu.VMEM((2, tm, d), F32), pltpu.VMEM((2, tm, d), F32),
                        pltpu.SemaphoreType.DMA((2,))],
        compiler_params=_params(("arbitrary",)),
        name="combine",
    )(dest3, dest3, ys, x1, route, mod4, final_w.reshape(1, d))


def _dest_kernel(route_ref, offs_ref, o_ref):
    route = route_ref[...]
    lane = lax.broadcasted_iota(jnp.int32, route.shape, 1)
    offs = offs_ref[...]
    slab = jnp.zeros(route.shape, F32)
    for k in range(2):
        e_lane = route[:, k:k + 1].astype(jnp.int32) + ROUTER_LANE0
        off = jnp.sum(jnp.where(lane == e_lane, offs, 0.0), axis=-1, keepdims=True)
        slab = jnp.where(lane == k, off + route[:, 4 + k:5 + k], slab)
    o_ref[...] = slab.T[:8, :].astype(jnp.int32)


def _dest(route, offs_row, tm):
    n = route.shape[0]
    blk = min(n, 2048)
    out = pl.pallas_call(
        _dest_kernel,
        out_shape=jax.ShapeDtypeStruct((8, n), jnp.int32),
        grid=(n // blk,),
        in_specs=[pl.BlockSpec((blk, LANES), lambda i: (i, 0)), pl.BlockSpec((1, LANES), lambda i: (0, 0))],
        out_specs=pl.BlockSpec((8, blk), lambda i: (0, i)),
        compiler_params=_params(("arbitrary",)),
        name="dest",
    )(route, offs_row)
    return jnp.concatenate([out[0].reshape(n // tm, 1, tm), out[1].reshape(n // tm, 1, tm)], axis=2)


def _plan(route, counts_row, seq):
    n = route.shape[0]
    tm = min(ROW_TILE, seq)
    counts = counts_row[0, ROUTER_LANE0:ROUTER_LANE0 + N_EXPERTS].astype(jnp.int32)
    tiles = (counts + EXPERT_TILE - 1) // EXPERT_TILE
    tile_end = jnp.cumsum(tiles)
    offs = (tile_end - tiles) * EXPERT_TILE
    offs_row = jnp.zeros((1, LANES), F32).at[0, ROUTER_LANE0:ROUTER_LANE0 + N_EXPERTS].set(offs.astype(F32))
    dest3 = _dest(route, offs_row, tm)
    n_tiles = (2 * n) // EXPERT_TILE + N_EXPERTS
    n_used = tile_end[-1]
    tile_ids = jnp.minimum(jnp.arange(n_tiles, dtype=jnp.int32), n_used - 1)
    tile_expert = jnp.sum(tile_ids[:, None] >= tile_end[None, :], axis=1).astype(jnp.int32)
    first = jnp.concatenate([jnp.ones((1,), jnp.int32),
                             (tile_expert[1:] != tile_expert[:-1]).astype(jnp.int32)])
    segment = jnp.cumsum(first) - 1
    ids = jnp.arange(N_EXPERTS, dtype=jnp.int32)
    later = (ids[None, :] > ids[:, None]) & (tiles[None, :] > 0)
    next_of = jnp.min(jnp.where(later, ids[None, :], N_EXPERTS), axis=1)
    next_of = jnp.where(next_of == N_EXPERTS, -1, next_of)
    pad_start = (offs + counts) // SUBLANES * SUBLANES
    return {"dest3": dest3, "tile_expert": tile_expert, "n_used": n_used.reshape(1).astype(jnp.int32),
            "segment": segment.astype(jnp.int32), "next_expert": next_of[tile_expert].astype(jnp.int32),
            "pad_start": pad_start.astype(jnp.int32),
            "pad_len": (tile_end * EXPERT_TILE - pad_start).astype(jnp.int32)}


def kernel(x, c, positions, w_ada, b_ada, norm1_w, w_in, lambda_q1, lambda_k1, lambda_q2, lambda_k2, subln_w, hgrn_lb_logits, gnorm_w, w_out, norm2_w, w_group_router, b_group_router, w_expert_router, b_expert_router, w_gate, w_up, w_down, final_norm_w):
    b, s, d = x.shape
    n = b * s
    depth = w_ada.shape[0]
    assert depth == 1, "the combine kernel applies the final norm, so exactly one layer is supported"
    half = DIFF_HEAD_DIM // 2

    inv_freq = ROPE_THETA ** (-jnp.arange(half, dtype=F32) / half)
    ang = positions.astype(F32).reshape(n, 1) * inv_freq[None, :]
    cos_t, sin_t = _trig(ang.reshape(n * half // LANES, LANES))
    cos_t = cos_t.reshape(n, half)
    sin_t = sin_t.reshape(n, half)
    cos = jnp.tile(cos_t, (1, LANES // half))
    sin = jnp.tile(jnp.concatenate([-sin_t, sin_t], axis=1), (1, LANES // DIFF_HEAD_DIM))

    xf = x.reshape(n, d)
    for l in range(depth):
        lambda_init = 0.8 - 0.6 * math.exp(-0.3 * l)
        mod4 = _mod(c, w_ada[l], b_ada[l]).reshape(b, 6, 1, d)
        q, k, v, hq, hf, hi, hg = _inproj(xf, mod4, norm1_w[l], w_in[l].astype(BF16), cos, sin, s)
        to3 = lambda t: t.reshape(b, s, SEG)
        a = _attn(to3(q), to3(k), to3(v), lambda_q1[l], lambda_k1[l], lambda_q2[l], lambda_k2[l],
                  subln_w[l], lambda_init)
        r = _hgrn(to3(hq), to3(hf), to3(hi), to3(hg), hgrn_lb_logits, gnorm_w[l], l)

        w_router = jnp.zeros((d, LANES), F32)
        w_router = w_router.at[:, :N_GROUPS].set(w_group_router[l])
        w_router = w_router.at[:, ROUTER_LANE0:ROUTER_LANE0 + N_EXPERTS].set(w_expert_router[l])
        b_router = jnp.zeros((1, LANES), F32)
        b_router = b_router.at[0, :N_GROUPS].set(b_group_router[l])
        b_router = b_router.at[0, ROUTER_LANE0:ROUTER_LANE0 + N_EXPERTS].set(b_expert_router[l].reshape(-1))
        wr_hi = w_router.astype(BF16)
        wr_split = jnp.concatenate([wr_hi, (w_router - wr_hi.astype(F32)).astype(BF16)], axis=1)
        x1, h2, route, counts = _outproj(a.reshape(n, SEG), r.reshape(n, SEG), xf, mod4, norm2_w[l],
                                         w_out[l].astype(BF16), wr_split, b_router, s)

        plan = _plan(route, counts, s)
        xs = _dispatch(plan, h2)
        e_w = lambda w: w.reshape((N_EXPERTS,) + w.shape[2:])
        ys = _experts(plan, xs, e_w(w_gate[l]), e_w(w_up[l]), e_w(w_down[l]))
        xf = _combine(plan["dest3"], ys, x1, route, mod4, final_norm_w, s)
    return xf.reshape(b, s, d)
```

```python
import functools
import math

import jax
import jax.numpy as jnp
from jax import lax
from jax.experimental import pallas as pl
from jax.experimental.pallas import tpu as pltpu

F32 = jnp.float32
BF16 = jnp.bfloat16
HIGHEST = lax.Precision.HIGHEST

N_DIFF_HEADS = 4
DIFF_HEAD_DIM = 64
DIFF_V_DIM = 2 * DIFF_HEAD_DIM
N_HGRN_HEADS = 4
HGRN_EXPAND = 128
HGRN_V_DIM = 128
ROPE_THETA = 10000.0
N_GROUPS = 4
EXPERTS_PER_GROUP = 8
N_EXPERTS = N_GROUPS * EXPERTS_PER_GROUP
RMS_EPS = 1e-6
SUBLN_EPS = 1e-5
SEG = 512
LANES = 128
SUBLANES = 8
ROUTER_LANE0 = N_GROUPS

ROW_TILE = 256
OUTPROJ_TILE = 1024
OUTPROJ_CHAINS = 2
ATTN_TILE = 512
ATTN_KV_TILE = 256
ATTN_ONES_ROWS = 16
LOG2E = math.log2(math.e)
HGRN_CHUNK = 64
HGRN_SUB = 16
HGRN_UNROLL = 8
EXPERT_TILE = 256
PAD_PIECES = (256, 128, 64, 32, 16, 8)
DMA_UNROLL = 16
VMEM_LIMIT = 48 * 1024 * 1024


def _params(sem, **kw):
    return pltpu.CompilerParams(dimension_semantics=sem, vmem_limit_bytes=VMEM_LIMIT, **kw)


def _silu(x):
    return x * jax.nn.sigmoid(x)


def _rms(x, eps):
    return x * lax.rsqrt(jnp.mean(x * x, axis=-1, keepdims=True) + eps)


def _mod_kernel(c_ref, w_ref, b_ref, o_ref):
    ca = _silu(c_ref[...])
    o_ref[...] = jnp.dot(ca, w_ref[...], preferred_element_type=F32, precision=HIGHEST) + b_ref[...]


def _mod(c, w_ada, b_ada):
    bsz, d = c.shape
    n_out = w_ada.shape[1]
    return pl.pallas_call(
        _mod_kernel,
        out_shape=jax.ShapeDtypeStruct((bsz, n_out), F32),
        grid=(n_out // d,),
        in_specs=[pl.BlockSpec((bsz, d), lambda j: (0, 0)),
                  pl.BlockSpec((d, d), lambda j: (0, j)),
                  pl.BlockSpec((1, d), lambda j: (0, j))],
        out_specs=pl.BlockSpec((bsz, d), lambda j: (0, j)),
        compiler_params=_params(("arbitrary",)),
        name="mod",
    )(c, w_ada, b_ada.reshape(1, n_out))


def _inproj_kernel(x_ref, sc_ref, sh_ref, nw_ref, w_ref, pos_ref, freq_ref, sign_ref,
                   q_ref, k_ref, v_ref, hq_ref, hf_ref, hi_ref, hg_ref):
    h = _rms(x_ref[...], RMS_EPS) * nw_ref[...]
    h = h * (1.0 + sc_ref[...]) + sh_ref[...]
    hb = h.astype(BF16)
    ang = pos_ref[...] * freq_ref[...]
    cos = jnp.cos(ang)
    sin = jnp.sin(ang) * sign_ref[...]
    lane = lax.broadcasted_iota(jnp.int32, cos.shape, 1)
    first_half = (lane % DIFF_HEAD_DIM) < (DIFF_HEAD_DIM // 2)

    def seg(i):
        return jnp.dot(hb, w_ref[:, i * SEG:(i + 1) * SEG], preferred_element_type=F32)

    def rope(p, scale):
        outs = []
        for c in range(SEG // LANES):
            pc = p[:, c * LANES:(c + 1) * LANES]
            partner = jnp.where(first_half, pltpu.roll(pc, LANES - DIFF_HEAD_DIM // 2, 1),
                                pltpu.roll(pc, DIFF_HEAD_DIM // 2, 1))
            outs.append((pc * cos + partner * sin) * scale)
        return jnp.concatenate(outs, axis=1)

    q_ref[...] = rope(seg(0), DIFF_HEAD_DIM ** -0.5 * LOG2E).astype(BF16)
    k_ref[...] = rope(seg(1), 1.0).astype(BF16)
    v_ref[...] = seg(2).astype(BF16)
    hq_ref[...] = seg(3).astype(BF16)
    hf_ref[...] = seg(4)
    hi_ref[...] = seg(5).astype(BF16)
    hg_ref[...] = seg(6).astype(BF16)


def _inproj(x2, mod4, norm_w, w_in_bf, pos, freq, sign, seq):
    n, d = x2.shape
    tm = min(ROW_TILE, seq)
    tpb = seq // tm
    row = lambda i: (i, 0)
    modspec = lambda j: pl.BlockSpec((None, None, 1, d), lambda i: (i // tpb, j, 0, 0))
    seg_spec = pl.BlockSpec((tm, SEG), row)
    bf = jax.ShapeDtypeStruct((n, SEG), BF16)
    return pl.pallas_call(
        _inproj_kernel,
        out_shape=(bf, bf, bf, bf, jax.ShapeDtypeStruct((n, SEG), F32), bf, bf),
        grid=(n // tm,),
        in_specs=[pl.BlockSpec((tm, d), row), modspec(1), modspec(0),
                  pl.BlockSpec((1, d), lambda i: (0, 0)),
                  pl.BlockSpec(w_in_bf.shape, lambda i: (0, 0)),
                  pl.BlockSpec((tm, LANES), row),
                  pl.BlockSpec((1, LANES), lambda i: (0, 0)), pl.BlockSpec((1, LANES), lambda i: (0, 0))],
        out_specs=(seg_spec,) * 7,
        compiler_params=_params(("arbitrary",)),
        name="inproj",
    )(x2, mod4, mod4, norm_w.reshape(1, d), w_in_bf, pos, freq, sign)


def _attn_kernel(q_ref, k_ref, v_ref, lq1_ref, lk1_ref, lq2_ref, lk2_ref, sw_ref, o_ref,
                 vt_s, acc_s, *, lambda_init):
    tq = q_ref.shape[0]
    seq = k_ref.shape[0]
    tk = min(ATTN_KV_TILE, tq)
    vd = DIFF_V_DIM
    qi = pl.program_id(1)
    nt = (((1,), (1,)), ((), ()))
    chains = [(h, mp) for h in range(N_DIFF_HEADS) for mp in range(2)]

    va = vd + ATTN_ONES_ROWS

    @pl.when(qi == 0)
    def _():
        for h in range(N_DIFF_HEADS):
            vt_s[h * va + vd:(h + 1) * va, :] = jnp.ones((ATTN_ONES_ROWS, seq), BF16)
            for c in range(seq // tk):
                vt_s[h * va:h * va + vd, c * tk:(c + 1) * tk] = (
                    v_ref[c * tk:(c + 1) * tk, h * vd:(h + 1) * vd].astype(F32).T.astype(BF16))

    acc_s[...] = jnp.zeros_like(acc_s)
    lane = lax.broadcasted_iota(jnp.int32, (tq, vd), 1)
    qz = []
    for h, mp in chains:
        qh = q_ref[:, h * vd:(h + 1) * vd]
        keep = (lane < DIFF_HEAD_DIM) if mp == 0 else (lane >= DIFF_HEAD_DIM)
        qz.append(jnp.where(keep, qh, jnp.zeros_like(qh)))

    def step(j, ms, diag):
        row0 = pl.multiple_of(j * tk, tk)
        scores = [lax.dot_general(k_ref[pl.ds(row0, tk), h * vd:(h + 1) * vd], qz[c], nt,
                                  preferred_element_type=F32) for c, (h, _) in enumerate(chains)]
        new_ms, alphas, probs = [], [], []
        for c, s in enumerate(scores):
            if diag is not None:
                kidx = lax.broadcasted_iota(jnp.int32, s.shape, 0) + diag
                qidx = lax.broadcasted_iota(jnp.int32, s.shape, 1)
                s = jnp.where(kidx <= qidx, s, -jnp.inf)
            m_new = jnp.maximum(ms[c], jnp.max(s, axis=0, keepdims=True))
            alphas.append(jnp.exp2(ms[c] - m_new))
            probs.append(jnp.exp2(s - m_new).astype(BF16))
            new_ms.append(m_new)
        pvs = [jnp.dot(vt_s[h * va:(h + 1) * va, pl.ds(row0, tk)], probs[c], preferred_element_type=F32)
               for c, (h, _) in enumerate(chains)]
        for c, pv in enumerate(pvs):
            acc_s[c] = alphas[c] * acc_s[c] + pv
        return tuple(new_ms)

    init = tuple(jnp.full((1, tq), -jnp.inf, F32) for _ in chains)
    per_q = tq // tk
    ms = lax.fori_loop(0, qi * per_q, lambda j, c: step(j, c, None), init)
    for d in range(per_q):
        ms = step(qi * per_q + d, ms, d * tk)

    lam = (jnp.exp(jnp.sum(lq1_ref[...] * lk1_ref[...], axis=-1, keepdims=True))
           - jnp.exp(jnp.sum(lq2_ref[...] * lk2_ref[...], axis=-1, keepdims=True)) + lambda_init)
    sw = jnp.concatenate([sw_ref[...]] * (tq // LANES), axis=1)
    for h in range(N_DIFF_HEADS):
        inv1 = 1.0 / acc_s[2 * h, vd:vd + 1, :]
        inv2 = 1.0 / acc_s[2 * h + 1, vd:vd + 1, :]
        o = acc_s[2 * h, :vd, :] * inv1 - lam * (acc_s[2 * h + 1, :vd, :] * inv2)
        o = o * lax.rsqrt(jnp.mean(o * o, axis=0, keepdims=True) + SUBLN_EPS)
        o_ref[:, h * vd:(h + 1) * vd] = (o * sw * (1.0 - lambda_init)).T.astype(BF16)


def _attn(q, k, v, lq1, lk1, lq2, lk2, subln_w, lambda_init):
    b, s, width = q.shape
    tq = min(ATTN_TILE, s)
    vec = lambda a: a.reshape(1, -1)
    small = lambda w: pl.BlockSpec((1, w), lambda bi, i: (0, 0))
    kv_spec = pl.BlockSpec((None, s, width), lambda bi, i: (bi, 0, 0))
    q_spec = pl.BlockSpec((None, tq, width), lambda bi, i: (bi, i, 0))
    sw_cols = jnp.broadcast_to(subln_w.reshape(DIFF_V_DIM, 1), (DIFF_V_DIM, LANES))
    return pl.pallas_call(
        functools.partial(_attn_kernel, lambda_init=lambda_init),
        out_shape=jax.ShapeDtypeStruct(q.shape, BF16),
        grid=(b, s // tq),
        in_specs=[q_spec, kv_spec, kv_spec, small(DIFF_HEAD_DIM), small(DIFF_HEAD_DIM),
                  small(DIFF_HEAD_DIM), small(DIFF_HEAD_DIM),
                  pl.BlockSpec((DIFF_V_DIM, LANES), lambda bi, i: (0, 0))],
        out_specs=q_spec,
        scratch_shapes=[pltpu.VMEM((N_DIFF_HEADS * (DIFF_V_DIM + ATTN_ONES_ROWS), s), BF16),
                        pltpu.VMEM((2 * N_DIFF_HEADS, DIFF_V_DIM + ATTN_ONES_ROWS, tq), F32)],
        compiler_params=_params(("arbitrary",) * 2),
        name="attn",
    )(q, k, v, vec(lq1), vec(lk1), vec(lq2), vec(lk2), sw_cols)


def _hgrn_kernel(q_ref, f_ref, i_ref, g_ref, lbl_ref, gw_ref, o_ref, qd_s, oi_s, u_s, d_s, *, layer):
    seq = q_ref.shape[0]
    ck, sub = HGRN_CHUNK, HGRN_SUB
    lg = lbl_ref[...]
    e = jnp.exp(lg - jnp.max(lg, axis=0, keepdims=True))
    lb = jnp.sum(e[:layer + 1], axis=0, keepdims=True) / jnp.sum(e, axis=0, keepdims=True)
    gw = gw_ref[...]
    causal = (lax.broadcasted_iota(jnp.int32, (ck, ck), 1)
              <= lax.broadcasted_iota(jnp.int32, (ck, ck), 0))
    tril = causal.astype(BF16)
    nt = (((1,), (1,)), ((), ()))
    tn = (((0,), (0,)), ((), ()))
    w = HGRN_EXPAND
    nsub = ck // sub

    def phase1(c2, _):
        chunks = [c2 * HGRN_UNROLL + u for u in range(HGRN_UNROLL)]
        rows = [pl.ds(pl.multiple_of(c * ck, ck), ck) for c in chunks]
        kins, splits = [], []
        for r in rows:
            forget = lb + (1.0 - lb) * jax.nn.sigmoid(f_ref[r, :])
            kins.append(1.0 - forget)
            g = jnp.log(forget)
            g_hi = g.astype(BF16)
            r1 = g - g_hi.astype(F32)
            g_mid = r1.astype(BF16)
            g_lo = (r1 - g_mid.astype(F32)).astype(BF16)
            splits.append(jnp.concatenate([g_hi, g_mid, g_lo], axis=1))
        sums = [jnp.dot(tril, x, preferred_element_type=F32) for x in splits]

        q_blk, k_blk, kdecs = [], [], []
        for c, r, kin, cs in zip(chunks, rows, kins, sums):
            bsum = cs[:, :w] + cs[:, w:2 * w] + cs[:, 2 * w:]
            qf = _silu(q_ref[r, :].astype(F32))
            qd_s[r, :] = (qf * jnp.exp(bsum)).astype(BF16)
            q_rows, k_cols = [], []
            for i in range(nsub):
                lo, hi = i * sub, (i + 1) * sub
                ref_pt = bsum[lo:lo + 1, :]
                qh = (qf[lo:hi] * jnp.exp(bsum[lo:hi] - ref_pt)).astype(BF16)
                kh = (kin[:hi] * jnp.exp(ref_pt - bsum[:hi])).astype(BF16)
                pieces = [qh if j == i else jnp.zeros((sub, w), BF16) for j in range(nsub)]
                q_rows.append(jnp.concatenate(pieces, axis=1))
                k_cols.append(kh if hi == ck else jnp.concatenate([kh, jnp.zeros((ck - hi, w), BF16)], axis=0))
            q_blk.append(jnp.concatenate(q_rows, axis=0))
            k_blk.append(jnp.concatenate(k_cols, axis=1))
            last = bsum[ck - 1:ck, :]
            kdecs.append((kin * jnp.exp(last - bsum)).astype(BF16))
            d_s[c] = jnp.exp(last)
        vbs = [i_ref[r, :] for r in rows]
        atts = [lax.dot_general(qb, kb, nt, preferred_element_type=F32) for qb, kb in zip(q_blk, k_blk)]
        for c, vb, kdec in zip(chunks, vbs, kdecs):
            u_s[c] = lax.dot_general(vb, kdec, tn, preferred_element_type=F32)
        probs = [jnp.where(causal, att, 0.0).astype(BF16) for att in atts]
        for r, p, vb in zip(rows, probs, vbs):
            oi_s[r, :] = jnp.dot(p, vb, preferred_element_type=F32)
        return 0

    lax.fori_loop(0, seq // (ck * HGRN_UNROLL), phase1, 0)

    def phase2(c2, state_t):
        for u in range(HGRN_UNROLL):
            c = c2 * HGRN_UNROLL + u
            rows = pl.ds(pl.multiple_of(c * ck, ck), ck)
            o = oi_s[rows, :] + lax.dot_general(qd_s[rows, :], state_t.astype(BF16), nt,
                                                preferred_element_type=F32)
            o = _rms(o, RMS_EPS) * gw * _silu(g_ref[rows, :].astype(F32))
            o_ref[rows, :] = o.astype(BF16)
            state_t = state_t * d_s[c] + u_s[c]
        return state_t

    lax.fori_loop(0, seq // (ck * HGRN_UNROLL), phase2, jnp.zeros((HGRN_V_DIM, HGRN_EXPAND), F32))


def _hgrn(hq, hf, hi, hg, lb_logits, gnorm_w, layer):
    b, s, _ = hq.shape
    w = HGRN_EXPAND
    blk = pl.BlockSpec((None, s, w), lambda bi, h: (bi, 0, h))
    return pl.pallas_call(
        functools.partial(_hgrn_kernel, layer=layer),
        out_shape=jax.ShapeDtypeStruct(hq.shape, BF16),
        grid=(b, N_HGRN_HEADS),
        in_specs=[blk, blk, blk, blk,
                  pl.BlockSpec((lb_logits.shape[0], w), lambda bi, h: (0, h)),
                  pl.BlockSpec((1, w), lambda bi, h: (0, 0))],
        out_specs=blk,
        scratch_shapes=[pltpu.VMEM((s, w), BF16), pltpu.VMEM((s, HGRN_V_DIM), F32),
                        pltpu.VMEM((s // HGRN_CHUNK, HGRN_V_DIM, w), F32),
                        pltpu.VMEM((s // HGRN_CHUNK, 1, w), F32)],
        compiler_params=_params(("arbitrary",) * 2),
        name="hgrn",
    )(hq, hf, hi, hg, lb_logits, gnorm_w.reshape(1, w))


def _route(logits):
    lane = lax.broadcasted_iota(jnp.int32, logits.shape, 1)
    big = jnp.int32(LANES)
    neg = -jnp.inf
    is_g = lane < N_GROUPS
    gl = jnp.where(is_g, logits, neg)
    gmax = jnp.max(gl, axis=-1, keepdims=True)
    gsel = jnp.min(jnp.where(gl == gmax, lane, big), axis=-1, keepdims=True)
    pg = 1.0 / jnp.sum(jnp.where(is_g, jnp.exp(logits - gmax), 0.0), axis=-1, keepdims=True)
    lo = ROUTER_LANE0 + EXPERTS_PER_GROUP * gsel
    el = jnp.where((lane >= lo) & (lane < lo + EXPERTS_PER_GROUP), logits, neg)
    v0 = jnp.max(el, axis=-1, keepdims=True)
    i0 = jnp.min(jnp.where(el == v0, lane, big), axis=-1, keepdims=True)
    el = jnp.where(lane == i0, neg, el)
    v1 = jnp.max(el, axis=-1, keepdims=True)
    i1 = jnp.min(jnp.where(el == v1, lane, big), axis=-1, keepdims=True)
    t = jnp.exp(v1 - v0)
    return lane, i0, i1, pg / (1.0 + t), pg * t / (1.0 + t)


def _outproj_kernel(a_ref, r_ref, x_ref, g1_ref, sc_ref, sh_ref, nw_ref, wo_ref, wr_ref, br_ref,
                    x1_ref, h2_ref, route_ref, cnt_ref, run_ref):
    i = pl.program_id(0)
    sub = x_ref.shape[0] // OUTPROJ_CHAINS
    half = a_ref.shape[1]
    blocks = [pl.ds(u * sub, sub) for u in range(OUTPROJ_CHAINS)]

    @pl.when(i == 0)
    def _():
        run_ref[...] = jnp.zeros_like(run_ref)

    mixes = [jnp.dot(a_ref[rows, :], wo_ref[:half, :], preferred_element_type=F32)
             + jnp.dot(r_ref[rows, :], wo_ref[half:, :], preferred_element_type=F32) for rows in blocks]
    splits = []
    for rows, mix in zip(blocks, mixes):
        x1 = x_ref[rows, :] + g1_ref[...] * mix
        x1_ref[rows, :] = x1
        h2 = _rms(x1, RMS_EPS) * nw_ref[...]
        h2 = h2 * (1.0 + sc_ref[...]) + sh_ref[...]
        h2_ref[rows, :] = h2
        h_hi = h2.astype(BF16)
        splits.append((h_hi, (h2 - h_hi.astype(F32)).astype(BF16)))
    parts = [jnp.dot(h_hi, wr_ref[...], preferred_element_type=F32)
             + jnp.dot(h_lo, wr_ref[...], preferred_element_type=F32) for h_hi, h_lo in splits]
    routed = [_route(part[:, :LANES] + part[:, LANES:] + br_ref[...]) for part in parts]
    before = jnp.where(lax.broadcasted_iota(jnp.int32, (sub, sub), 1)
                       < lax.broadcasted_iota(jnp.int32, (sub, sub), 0), 1.0, 0.0).astype(BF16)
    onehots = [jnp.where((lane == i0) | (lane == i1), 1.0, 0.0) for lane, i0, i1, _, _ in routed]
    prefixes = [jnp.dot(before, oh.astype(BF16), preferred_element_type=F32) for oh in onehots]
    run = run_ref[...]
    for rows, (lane, i0, i1, w0, w1), onehot, prefix in zip(blocks, routed, onehots, prefixes):
        prefix = prefix + run
        rank0 = jnp.sum(jnp.where(lane == i0, prefix, 0.0), axis=-1, keepdims=True)
        rank1 = jnp.sum(jnp.where(lane == i1, prefix, 0.0), axis=-1, keepdims=True)
        run = run + jnp.sum(onehot, axis=0, keepdims=True)
        e0 = (i0 - ROUTER_LANE0).astype(F32)
        e1 = (i1 - ROUTER_LANE0).astype(F32)
        slab = jnp.zeros(lane.shape, F32)
        for ln, val in enumerate((e0, e1, w0, w1, rank0, rank1)):
            slab = jnp.where(lane == ln, val, slab)
        route_ref[rows, :] = slab
    run_ref[...] = run
    cnt_ref[...] = run


def _outproj(a2, r2, x2, mod4, norm_w, w_out_bf, w_router, b_router, seq):
    n, d = x2.shape
    tm = min(OUTPROJ_TILE, seq)
    tpb = seq // tm
    row = lambda i: (i, 0)
    fixed = lambda i: (0, 0)
    modspec = lambda j: pl.BlockSpec((None, None, 1, d), lambda i: (i // tpb, j, 0, 0))
    return pl.pallas_call(
        _outproj_kernel,
        out_shape=(jax.ShapeDtypeStruct((n, d), F32), jax.ShapeDtypeStruct((n, d), F32),
                   jax.ShapeDtypeStruct((n, LANES), F32), jax.ShapeDtypeStruct((1, LANES), F32)),
        grid=(n // tm,),
        in_specs=[pl.BlockSpec((tm, a2.shape[1]), row), pl.BlockSpec((tm, r2.shape[1]), row),
                  pl.BlockSpec((tm, d), row), modspec(2), modspec(4), modspec(3),
                  pl.BlockSpec((1, d), fixed), pl.BlockSpec(w_out_bf.shape, fixed),
                  pl.BlockSpec(w_router.shape, fixed), pl.BlockSpec((1, LANES), fixed)],
        out_specs=(pl.BlockSpec((tm, d), row), pl.BlockSpec((tm, d), row),
                   pl.BlockSpec((tm, LANES), row), pl.BlockSpec((1, LANES), fixed)),
        scratch_shapes=[pltpu.VMEM((1, LANES), F32)],
        compiler_params=_params(("arbitrary",)),
        name="outproj",
    )(a2, r2, x2, mod4, mod4, mod4, norm_w.reshape(1, d), w_out_bf, w_router, b_router)


def _row_copy(src_ref, src_row, dst_ref, dst_row, sem):
    return pltpu.make_async_copy(src_ref.at[pl.ds(src_row, 1), :], dst_ref.at[pl.ds(dst_row, 1), :], sem)


def _dispatch_kernel(pad_ref, len_ref, nu_ref, dest_ref, h_ref, xs_ref, zero_s, sem):
    tm = h_ref.shape[0]

    @pl.when(pl.program_id(0) == 0)
    def _():
        zero_s[...] = jnp.zeros_like(zero_s)
        tails = []
        for e in range(N_EXPERTS):
            pos = pad_ref[e]
            for piece in PAD_PIECES:
                hit = (len_ref[e] & piece) != 0
                tails.append((hit, pltpu.make_async_copy(
                    zero_s.at[pl.ds(0, piece), :],
                    xs_ref.at[pl.ds(pl.multiple_of(pos, SUBLANES), piece), :], sem)))
                pos = pos + jnp.where(hit, piece, 0)
        n_tiles = xs_ref.shape[0] // EXPERT_TILE
        for t in range(N_EXPERTS):
            tile = nu_ref[0] + t
            row0 = pl.multiple_of(jnp.minimum(tile, n_tiles - 1) * EXPERT_TILE, EXPERT_TILE)
            tails.append((tile < n_tiles, pltpu.make_async_copy(
                zero_s, xs_ref.at[pl.ds(row0, EXPERT_TILE), :], sem)))
        for hit, cp in tails:
            pl.when(hit)(cp.start)
        for hit, cp in tails:
            pl.when(hit)(cp.wait)

    def issue(g, _):
        for u in range(DMA_UNROLL):
            r = g * DMA_UNROLL + u
            _row_copy(h_ref, r, xs_ref, dest_ref[0, r], sem).start(priority=0)
            _row_copy(h_ref, r, xs_ref, dest_ref[0, tm + r], sem).start(priority=1)
        return 0

    lax.fori_loop(0, tm // DMA_UNROLL, issue, 0)

    def drain(g, _):
        for _u in range(2 * DMA_UNROLL):
            _row_copy(h_ref, 0, xs_ref, 0, sem).wait()
        return 0

    lax.fori_loop(0, tm // DMA_UNROLL, drain, 0)


def _dispatch(plan, h2):
    n, d = h2.shape
    dest3 = plan["dest3"]
    tm = dest3.shape[2] // 2
    rows = plan["tile_expert"].shape[0] * EXPERT_TILE
    grid_spec = pltpu.PrefetchScalarGridSpec(
        num_scalar_prefetch=3,
        grid=(n // tm,),
        in_specs=[pl.BlockSpec((None, 1, 2 * tm), lambda i, pad, ln, nu: (i, 0, 0), memory_space=pltpu.SMEM),
                  pl.BlockSpec((tm, d), lambda i, pad, ln, nu: (i, 0))],
        out_specs=pl.BlockSpec(memory_space=pl.ANY),
        scratch_shapes=[pltpu.VMEM((EXPERT_TILE, d), F32), pltpu.SemaphoreType.DMA],
    )
    return pl.pallas_call(
        _dispatch_kernel,
        out_shape=jax.ShapeDtypeStruct((rows, d), F32),
        grid_spec=grid_spec,
        compiler_params=_params(("arbitrary",), has_side_effects=True),
        name="dispatch",
    )(plan["pad_start"], plan["pad_len"], plan["n_used"], dest3, h2)


def _experts_kernel(te_ref, nu_ref, seg_ref, nxt_ref, xs_ref, wg_hbm, wu_hbm, wd_hbm, ys_ref,
                    wg_f, wu_f, wd_f, wg_s, wu_s, wd_s, sem):
    i = pl.program_id(0)

    def fetch(e, sl):
        return [pltpu.make_async_copy(src.at[e], dst.at[sl], sem.at[sl])
                for src, dst in ((wg_hbm, wg_f), (wu_hbm, wu_f), (wd_hbm, wd_f))]

    @pl.when(i < nu_ref[0])
    def _():
        prev = te_ref[jnp.maximum(i - 1, 0)]

        @pl.when(i == 0)
        def _():
            for cp in fetch(te_ref[0], 0):
                cp.start()

        @pl.when((i == 0) | (te_ref[i] != prev))
        def _():
            sl = seg_ref[i] % 2
            for cp in fetch(te_ref[i], sl):
                cp.wait()

            @pl.when(nxt_ref[i] >= 0)
            def _():
                for cp in fetch(nxt_ref[i], 1 - sl):
                    cp.start()

            wg_s[...] = wg_f[sl].astype(BF16)
            wu_s[...] = wu_f[sl].astype(BF16)
            wd_s[...] = wd_f[sl].astype(BF16)

        x = xs_ref[...].astype(BF16)
        a = jnp.dot(x, wg_s[...], preferred_element_type=F32)
        u = jnp.dot(x, wu_s[...], preferred_element_type=F32)
        hid = (_silu(a) * u).astype(BF16)
        ys_ref[...] = jnp.dot(hid, wd_s[...], preferred_element_type=F32)

    @pl.when(i >= nu_ref[0])
    def _():
        ys_ref[...] = jnp.zeros_like(ys_ref)


def _experts(plan, xs, w_gate, w_up, w_down):
    d = xs.shape[1]
    e, _, f = w_gate.shape
    tm = EXPERT_TILE
    n_tiles = plan["tile_expert"].shape[0]
    row = lambda i, te, nu, seg, nxt: (jnp.minimum(i, nu[0] - 1), 0)
    hbm = pl.BlockSpec(memory_space=pl.ANY)
    grid_spec = pltpu.PrefetchScalarGridSpec(
        num_scalar_prefetch=4,
        grid=(n_tiles,),
        in_specs=[pl.BlockSpec((tm, d), row), hbm, hbm, hbm],
        out_specs=pl.BlockSpec((tm, d), lambda i, te, nu, seg, nxt: (i, 0)),
        scratch_shapes=[pltpu.VMEM((2, d, f), F32), pltpu.VMEM((2, d, f), F32), pltpu.VMEM((2, f, d), F32),
                        pltpu.VMEM((d, f), BF16), pltpu.VMEM((d, f), BF16), pltpu.VMEM((f, d), BF16),
                        pltpu.SemaphoreType.DMA((2,))],
    )
    return pl.pallas_call(
        _experts_kernel,
        out_shape=jax.ShapeDtypeStruct((n_tiles * tm, d), F32),
        grid_spec=grid_spec,
        compiler_params=_params(("arbitrary",)),
        name="experts",
    )(plan["tile_expert"], plan["n_used"], plan["segment"], plan["next_expert"], xs, w_gate, w_up, w_down)


def _combine_kernel(dest_ref, dnext_ref, ys_ref, x1_ref, route_ref, g2_ref, fw_ref, o_ref, y0_s, y1_s, sem):
    tm = x1_ref.shape[0]
    i = pl.program_id(0)
    slot = i % 2

    def gather(idx_ref, sl):
        def issue(g, _):
            for u in range(DMA_UNROLL):
                r = g * DMA_UNROLL + u
                _row_copy(ys_ref, idx_ref[0, r], y0_s.at[sl], r, sem.at[sl]).start(priority=0)
                _row_copy(ys_ref, idx_ref[0, tm + r], y1_s.at[sl], r, sem.at[sl]).start(priority=1)
            return 0

        lax.fori_loop(0, tm // DMA_UNROLL, issue, 0)

    @pl.when(i == 0)
    def _():
        gather(dest_ref, 0)

    @pl.when(i + 1 < pl.num_programs(0))
    def _():
        gather(dnext_ref, 1 - slot)

    def drain(g, _):
        for _u in range(DMA_UNROLL):
            _row_copy(ys_ref, 0, y0_s.at[slot], 0, sem.at[slot]).wait()
            _row_copy(ys_ref, 0, y1_s.at[slot], 0, sem.at[slot]).wait()
        return 0

    lax.fori_loop(0, tm // DMA_UNROLL, drain, 0)
    route = route_ref[...]
    w0 = route[:, 2:3]
    w1 = route[:, 3:4]
    x = x1_ref[...] + g2_ref[...] * (w0 * y0_s[slot] + w1 * y1_s[slot])
    o_ref[...] = _rms(x, RMS_EPS) * fw_ref[...]


def _combine(dest3, ys, x1, route, mod4, final_w, seq):
    n, d = x1.shape
    tm = dest3.shape[2] // 2
    tpb = seq // tm
    row = lambda i: (i, 0)
    steps = n // tm
    return pl.pallas_call(
        _combine_kernel,
        out_shape=jax.ShapeDtypeStruct((n, d), F32),
        grid=(steps,),
        in_specs=[pl.BlockSpec((None, 1, 2 * tm), lambda i: (i, 0, 0), memory_space=pltpu.SMEM),
                  pl.BlockSpec((None, 1, 2 * tm), lambda i: (jnp.minimum(i + 1, steps - 1), 0, 0),
                               memory_space=pltpu.SMEM),
                  pl.BlockSpec(memory_space=pl.ANY),
                  pl.BlockSpec((tm, d), row), pl.BlockSpec((tm, LANES), row),
                  pl.BlockSpec((None, None, 1, d), lambda i: (i // tpb, 5, 0, 0)),
                  pl.BlockSpec((1, d), lambda i: (0, 0))],
        out_specs=pl.BlockSpec((tm, d), row),
        scratch_shapes=[pltpu.VMEM((2, tm, d), F32), pltpu.VMEM((2, tm, d), F32),
                        pltpu.SemaphoreType.DMA((2,))],
        compiler_params=_params(("arbitrary",)),
        name="combine",
    )(dest3, dest3, ys, x1, route, mod4, final_w.reshape(1, d))


def _dest_kernel(route_ref, offs_ref, o_ref):
    route = route_ref[...]
    lane = lax.broadcasted_iota(jnp.int32, route.shape, 1)
    offs = offs_ref[...]
    slab = jnp.zeros(route.shape, F32)
    for k in range(2):
        e_lane = route[:, k:k + 1].astype(jnp.int32) + ROUTER_LANE0
        off = jnp.sum(jnp.where(lane == e_lane, offs, 0.0), axis=-1, keepdims=True)
        slab = jnp.where(lane == k, off + route[:, 4 + k:5 + k], slab)
    o_ref[...] = slab.T[:8, :].astype(jnp.int32)


def _dest(route, offs_row, tm):
    n = route.shape[0]
    blk = min(n, 2048)
    out = pl.pallas_call(
        _dest_kernel,
        out_shape=jax.ShapeDtypeStruct((8, n), jnp.int32),
        grid=(n // blk,),
        in_specs=[pl.BlockSpec((blk, LANES), lambda i: (i, 0)), pl.BlockSpec((1, LANES), lambda i: (0, 0))],
        out_specs=pl.BlockSpec((8, blk), lambda i: (0, i)),
        compiler_params=_params(("arbitrary",)),
        name="dest",
    )(route, offs_row)
    return jnp.concatenate([out[0].reshape(n // tm, 1, tm), out[1].reshape(n // tm, 1, tm)], axis=2)


def _plan(route, counts_row, seq):
    n = route.shape[0]
    tm = min(ROW_TILE, seq)
    counts = counts_row[0, ROUTER_LANE0:ROUTER_LANE0 + N_EXPERTS].astype(jnp.int32)
    tiles = (counts + EXPERT_TILE - 1) // EXPERT_TILE
    tile_end = jnp.cumsum(tiles)
    offs = (tile_end - tiles) * EXPERT_TILE
    offs_row = jnp.zeros((1, LANES), F32).at[0, ROUTER_LANE0:ROUTER_LANE0 + N_EXPERTS].set(offs.astype(F32))
    dest3 = _dest(route, offs_row, tm)
    n_tiles = (2 * n) // EXPERT_TILE + N_EXPERTS
    n_used = tile_end[-1]
    tile_ids = jnp.minimum(jnp.arange(n_tiles, dtype=jnp.int32), n_used - 1)
    tile_expert = jnp.sum(tile_ids[:, None] >= tile_end[None, :], axis=1).astype(jnp.int32)
    first = jnp.concatenate([jnp.ones((1,), jnp.int32),
                             (tile_expert[1:] != tile_expert[:-1]).astype(jnp.int32)])
    segment = jnp.cumsum(first) - 1
    ids = jnp.arange(N_EXPERTS, dtype=jnp.int32)
    later = (ids[None, :] > ids[:, None]) & (tiles[None, :] > 0)
    next_of = jnp.min(jnp.where(later, ids[None, :], N_EXPERTS), axis=1)
    next_of = jnp.where(next_of == N_EXPERTS, -1, next_of)
    pad_start = (offs + counts) // SUBLANES * SUBLANES
    return {"dest3": dest3, "tile_expert": tile_expert, "n_used": n_used.reshape(1).astype(jnp.int32),
            "segment": segment.astype(jnp.int32), "next_expert": next_of[tile_expert].astype(jnp.int32),
            "pad_start": pad_start.astype(jnp.int32),
            "pad_len": (tile_end * EXPERT_TILE - pad_start).astype(jnp.int32)}


def kernel(x, c, positions, w_ada, b_ada, norm1_w, w_in, lambda_q1, lambda_k1, lambda_q2, lambda_k2, subln_w, hgrn_lb_logits, gnorm_w, w_out, norm2_w, w_group_router, b_group_router, w_expert_router, b_expert_router, w_gate, w_up, w_down, final_norm_w):
    b, s, d = x.shape
    n = b * s
    depth = w_ada.shape[0]
    assert depth == 1, "the combine kernel applies the final norm, so exactly one layer is supported"
    half = DIFF_HEAD_DIM // 2

    inv_freq = ROPE_THETA ** (-jnp.arange(half, dtype=F32) / half)
    pos = jnp.broadcast_to(positions.astype(F32).reshape(n, 1), (n, LANES))
    freq = jnp.tile(inv_freq, LANES // half).reshape(1, LANES)
    sign = jnp.tile(jnp.concatenate([-jnp.ones((half,), F32), jnp.ones((half,), F32)]),
                    LANES // DIFF_HEAD_DIM).reshape(1, LANES)

    xf = x.reshape(n, d)
    for l in range(depth):
        lambda_init = 0.8 - 0.6 * math.exp(-0.3 * l)
        mod4 = _mod(c, w_ada[l], b_ada[l]).reshape(b, 6, 1, d)
        q, k, v, hq, hf, hi, hg = _inproj(xf, mod4, norm1_w[l], w_in[l].astype(BF16), pos, freq, sign, s)
        to3 = lambda t: t.reshape(b, s, SEG)
        a = _attn(to3(q), to3(k), to3(v), lambda_q1[l], lambda_k1[l], lambda_q2[l], lambda_k2[l],
                  subln_w[l], lambda_init)
        r = _hgrn(to3(hq), to3(hf), to3(hi), to3(hg), hgrn_lb_logits, gnorm_w[l], l)

        w_router = jnp.zeros((d, LANES), F32)
        w_router = w_router.at[:, :N_GROUPS].set(w_group_router[l])
        w_router = w_router.at[:, ROUTER_LANE0:ROUTER_LANE0 + N_EXPERTS].set(w_expert_router[l])
        b_router = jnp.zeros((1, LANES), F32)
        b_router = b_router.at[0, :N_GROUPS].set(b_group_router[l])
        b_router = b_router.at[0, ROUTER_LANE0:ROUTER_LANE0 + N_EXPERTS].set(b_expert_router[l].reshape(-1))
        wr_hi = w_router.astype(BF16)
        wr_split = jnp.concatenate([wr_hi, (w_router - wr_hi.astype(F32)).astype(BF16)], axis=1)
        x1, h2, route, counts = _outproj(a.reshape(n, SEG), r.reshape(n, SEG), xf, mod4, norm2_w[l],
                                         w_out[l].astype(BF16), wr_split, b_router, s)

        plan = _plan(route, counts, s)
        xs = _dispatch(plan, h2)
        e_w = lambda w: w.reshape((N_EXPERTS,) + w.shape[2:])
        ys = _experts(plan, xs, e_w(w_gate[l]), e_w(w_up[l]), e_w(w_down[l]))
        xf = _combine(plan["dest3"], ys, x1, route, mod4, final_norm_w, s)
    return xf.reshape(b, s, d)
```

```python
import functools
import math

import jax
import jax.numpy as jnp
from jax import lax
from jax.experimental import pallas as pl
from jax.experimental.pallas import tpu as pltpu

F32 = jnp.float32
BF16 = jnp.bfloat16
HIGHEST = lax.Precision.HIGHEST

N_DIFF_HEADS = 4
DIFF_HEAD_DIM = 64
DIFF_V_DIM = 2 * DIFF_HEAD_DIM
N_HGRN_HEADS = 4
HGRN_EXPAND = 128
HGRN_V_DIM = 128
ROPE_THETA = 10000.0
N_GROUPS = 4
EXPERTS_PER_GROUP = 8
N_EXPERTS = N_GROUPS * EXPERTS_PER_GROUP
RMS_EPS = 1e-6
SUBLN_EPS = 1e-5
SEG = 512
LANES = 128
SUBLANES = 8
ROUTER_LANE0 = N_GROUPS

ROW_TILE = 256
OUTPROJ_TILE = 1024
OUTPROJ_CHAINS = 2
ATTN_TILE = 512
ATTN_KV_TILE = 256
ATTN_ONES_ROWS = 16
LOG2E = math.log2(math.e)
HGRN_CHUNK = 64
HGRN_SUB = 16
HGRN_UNROLL = 8
EXPERT_TILE = 256
PAD_PIECES = (256, 128, 64, 32, 16, 8)
DMA_UNROLL = 16
VMEM_LIMIT = 48 * 1024 * 1024


def _params(sem, **kw):
    return pltpu.CompilerParams(dimension_semantics=sem, vmem_limit_bytes=VMEM_LIMIT, **kw)


def _silu(x):
    return x * jax.nn.sigmoid(x)


def _rms(x, eps):
    return x * lax.rsqrt(jnp.mean(x * x, axis=-1, keepdims=True) + eps)


def _mod_kernel(c_ref, w_ref, b_ref, o_ref):
    ca = _silu(c_ref[...])
    o_ref[...] = jnp.dot(ca, w_ref[...], preferred_element_type=F32, precision=HIGHEST) + b_ref[...]


def _mod(c, w_ada, b_ada):
    bsz, d = c.shape
    n_out = w_ada.shape[1]
    return pl.pallas_call(
        _mod_kernel,
        out_shape=jax.ShapeDtypeStruct((bsz, n_out), F32),
        grid=(n_out // d,),
        in_specs=[pl.BlockSpec((bsz, d), lambda j: (0, 0)),
                  pl.BlockSpec((d, d), lambda j: (0, j)),
                  pl.BlockSpec((1, d), lambda j: (0, j))],
        out_specs=pl.BlockSpec((bsz, d), lambda j: (0, j)),
        compiler_params=_params(("arbitrary",)),
        name="mod",
    )(c, w_ada, b_ada.reshape(1, n_out))


def _inproj_kernel(x_ref, sc_ref, sh_ref, nw_ref, w_ref, pos_ref, freq_ref, sign_ref,
                   q_ref, k_ref, v_ref, hq_ref, hf_ref, hi_ref, hg_ref):
    h = _rms(x_ref[...], RMS_EPS) * nw_ref[...]
    h = h * (1.0 + sc_ref[...]) + sh_ref[...]
    hb = h.astype(BF16)
    ang = pos_ref[...] * freq_ref[...]
    cos = jnp.cos(ang)
    sin = jnp.sin(ang) * sign_ref[...]
    lane = lax.broadcasted_iota(jnp.int32, cos.shape, 1)
    first_half = (lane % DIFF_HEAD_DIM) < (DIFF_HEAD_DIM // 2)

    def seg(i):
        return jnp.dot(hb, w_ref[:, i * SEG:(i + 1) * SEG], preferred_element_type=F32)

    def rope(p, scale):
        outs = []
        for c in range(SEG // LANES):
            pc = p[:, c * LANES:(c + 1) * LANES]
            partner = jnp.where(first_half, pltpu.roll(pc, LANES - DIFF_HEAD_DIM // 2, 1),
                                pltpu.roll(pc, DIFF_HEAD_DIM // 2, 1))
            outs.append((pc * cos + partner * sin) * scale)
        return jnp.concatenate(outs, axis=1)

    q_ref[...] = rope(seg(0), DIFF_HEAD_DIM ** -0.5 * LOG2E).astype(BF16)
    k_ref[...] = rope(seg(1), 1.0).astype(BF16)
    v_ref[...] = seg(2).astype(BF16)
    hq_ref[...] = seg(3).astype(BF16)
    hf_ref[...] = seg(4)
    hi_ref[...] = seg(5).astype(BF16)
    hg_ref[...] = seg(6).astype(BF16)


def _inproj(x2, mod4, norm_w, w_in_bf, pos, freq, sign, seq):
    n, d = x2.shape
    tm = min(ROW_TILE, seq)
    tpb = seq // tm
    row = lambda i: (i, 0)
    modspec = lambda j: pl.BlockSpec((None, None, 1, d), lambda i: (i // tpb, j, 0, 0))
    seg_spec = pl.BlockSpec((tm, SEG), row)
    bf = jax.ShapeDtypeStruct((n, SEG), BF16)
    return pl.pallas_call(
        _inproj_kernel,
        out_shape=(bf, bf, bf, bf, jax.ShapeDtypeStruct((n, SEG), F32), bf, bf),
        grid=(n // tm,),
        in_specs=[pl.BlockSpec((tm, d), row), modspec(1), modspec(0),
                  pl.BlockSpec((1, d), lambda i: (0, 0)),
                  pl.BlockSpec(w_in_bf.shape, lambda i: (0, 0)),
                  pl.BlockSpec((tm, LANES), row),
                  pl.BlockSpec((1, LANES), lambda i: (0, 0)), pl.BlockSpec((1, LANES), lambda i: (0, 0))],
        out_specs=(seg_spec,) * 7,
        compiler_params=_params(("arbitrary",)),
        name="inproj",
    )(x2, mod4, mod4, norm_w.reshape(1, d), w_in_bf, pos, freq, sign)


def _attn_kernel(q_ref, k_ref, v_ref, lq1_ref, lk1_ref, lq2_ref, lk2_ref, sw_ref, o_ref,
                 vt_s, acc_s, *, lambda_init):
    tq = q_ref.shape[0]
    seq = k_ref.shape[0]
    tk = min(ATTN_KV_TILE, tq)
    vd = DIFF_V_DIM
    qi = pl.program_id(1)
    nt = (((1,), (1,)), ((), ()))
    chains = [(h, mp) for h in range(N_DIFF_HEADS) for mp in range(2)]

    va = vd + ATTN_ONES_ROWS

    @pl.when(qi == 0)
    def _():
        for h in range(N_DIFF_HEADS):
            vt_s[h * va + vd:(h + 1) * va, :] = jnp.ones((ATTN_ONES_ROWS, seq), BF16)
            for c in range(seq // tk):
                vt_s[h * va:h * va + vd, c * tk:(c + 1) * tk] = (
                    v_ref[c * tk:(c + 1) * tk, h * vd:(h + 1) * vd].astype(F32).T.astype(BF16))

    acc_s[...] = jnp.zeros_like(acc_s)
    lane = lax.broadcasted_iota(jnp.int32, (tq, vd), 1)
    qz = []
    for h, mp in chains:
        qh = q_ref[:, h * vd:(h + 1) * vd]
        keep = (lane < DIFF_HEAD_DIM) if mp == 0 else (lane >= DIFF_HEAD_DIM)
        qz.append(jnp.where(keep, qh, jnp.zeros_like(qh)))

    def step(j, ms, diag):
        row0 = pl.multiple_of(j * tk, tk)
        scores = [lax.dot_general(k_ref[pl.ds(row0, tk), h * vd:(h + 1) * vd], qz[c], nt,
                                  preferred_element_type=F32) for c, (h, _) in enumerate(chains)]
        new_ms, alphas, probs = [], [], []
        for c, s in enumerate(scores):
            if diag is not None:
                kidx = lax.broadcasted_iota(jnp.int32, s.shape, 0) + diag
                qidx = lax.broadcasted_iota(jnp.int32, s.shape, 1)
                s = jnp.where(kidx <= qidx, s, -jnp.inf)
            m_new = jnp.maximum(ms[c], jnp.max(s, axis=0, keepdims=True))
            alphas.append(jnp.exp2(ms[c] - m_new))
            probs.append(jnp.exp2(s - m_new).astype(BF16))
            new_ms.append(m_new)
        pvs = [jnp.dot(vt_s[h * va:(h + 1) * va, pl.ds(row0, tk)], probs[c], preferred_element_type=F32)
               for c, (h, _) in enumerate(chains)]
        for c, pv in enumerate(pvs):
            acc_s[c] = alphas[c] * acc_s[c] + pv
        return tuple(new_ms)

    init = tuple(jnp.full((1, tq), -jnp.inf, F32) for _ in chains)
    per_q = tq // tk
    ms = lax.fori_loop(0, qi * per_q, lambda j, c: step(j, c, None), init)
    for d in range(per_q):
        ms = step(qi * per_q + d, ms, d * tk)

    lam = (jnp.exp(jnp.sum(lq1_ref[...] * lk1_ref[...], axis=-1, keepdims=True))
           - jnp.exp(jnp.sum(lq2_ref[...] * lk2_ref[...], axis=-1, keepdims=True)) + lambda_init)
    sw = jnp.concatenate([sw_ref[...]] * (tq // LANES), axis=1)
    for h in range(N_DIFF_HEADS):
        inv1 = 1.0 / acc_s[2 * h, vd:vd + 1, :]
        inv2 = 1.0 / acc_s[2 * h + 1, vd:vd + 1, :]
        o = acc_s[2 * h, :vd, :] * inv1 - lam * (acc_s[2 * h + 1, :vd, :] * inv2)
        o = o * lax.rsqrt(jnp.mean(o * o, axis=0, keepdims=True) + SUBLN_EPS)
        o_ref[:, h * vd:(h + 1) * vd] = (o * sw * (1.0 - lambda_init)).T.astype(BF16)


def _attn(q, k, v, lq1, lk1, lq2, lk2, subln_w, lambda_init):
    b, s, width = q.shape
    tq = min(ATTN_TILE, s)
    vec = lambda a: a.reshape(1, -1)
    small = lambda w: pl.BlockSpec((1, w), lambda bi, i: (0, 0))
    kv_spec = pl.BlockSpec((None, s, width), lambda bi, i: (bi, 0, 0))
    q_spec = pl.BlockSpec((None, tq, width), lambda bi, i: (bi, i, 0))
    sw_cols = jnp.broadcast_to(subln_w.reshape(DIFF_V_DIM, 1), (DIFF_V_DIM, LANES))
    return pl.pallas_call(
        functools.partial(_attn_kernel, lambda_init=lambda_init),
        out_shape=jax.ShapeDtypeStruct(q.shape, BF16),
        grid=(b, s // tq),
        in_specs=[q_spec, kv_spec, kv_spec, small(DIFF_HEAD_DIM), small(DIFF_HEAD_DIM),
                  small(DIFF_HEAD_DIM), small(DIFF_HEAD_DIM),
                  pl.BlockSpec((DIFF_V_DIM, LANES), lambda bi, i: (0, 0))],
        out_specs=q_spec,
        scratch_shapes=[pltpu.VMEM((N_DIFF_HEADS * (DIFF_V_DIM + ATTN_ONES_ROWS), s), BF16),
                        pltpu.VMEM((2 * N_DIFF_HEADS, DIFF_V_DIM + ATTN_ONES_ROWS, tq), F32)],
        compiler_params=_params(("arbitrary",) * 2),
        name="attn",
    )(q, k, v, vec(lq1), vec(lk1), vec(lq2), vec(lk2), sw_cols)


def _hgrn_kernel(q_ref, f_ref, i_ref, g_ref, lbl_ref, gw_ref, o_ref, qd_s, oi_s, u_s, d_s, *, layer):
    seq = q_ref.shape[0]
    ck, sub = HGRN_CHUNK, HGRN_SUB
    lg = lbl_ref[...]
    e = jnp.exp(lg - jnp.max(lg, axis=0, keepdims=True))
    lb = jnp.sum(e[:layer + 1], axis=0, keepdims=True) / jnp.sum(e, axis=0, keepdims=True)
    gw = gw_ref[...]
    causal = (lax.broadcasted_iota(jnp.int32, (ck, ck), 1)
              <= lax.broadcasted_iota(jnp.int32, (ck, ck), 0))
    tril = causal.astype(BF16)
    nt = (((1,), (1,)), ((), ()))
    tn = (((0,), (0,)), ((), ()))
    w = HGRN_EXPAND
    nsub = ck // sub

    def phase1(c2, _):
        chunks = [c2 * HGRN_UNROLL + u for u in range(HGRN_UNROLL)]
        rows = [pl.ds(pl.multiple_of(c * ck, ck), ck) for c in chunks]
        kins, splits = [], []
        for r in rows:
            forget = lb + (1.0 - lb) * jax.nn.sigmoid(f_ref[r, :])
            kins.append(1.0 - forget)
            g = jnp.log2(forget)
            g_hi = g.astype(BF16)
            r1 = g - g_hi.astype(F32)
            g_mid = r1.astype(BF16)
            g_lo = (r1 - g_mid.astype(F32)).astype(BF16)
            splits.append(jnp.concatenate([g_hi, g_mid, g_lo], axis=1))
        sums = [jnp.dot(tril, x, preferred_element_type=F32) for x in splits]

        q_blk, k_blk, kdecs = [], [], []
        for c, r, kin, cs in zip(chunks, rows, kins, sums):
            bsum = cs[:, :w] + cs[:, w:2 * w] + cs[:, 2 * w:]
            qf = _silu(q_ref[r, :].astype(F32))
            qd_s[r, :] = (qf * jnp.exp2(bsum)).astype(BF16)
            q_rows, k_cols = [], []
            for i in range(nsub):
                lo, hi = i * sub, (i + 1) * sub
                ref_pt = bsum[lo:lo + 1, :]
                qh = (qf[lo:hi] * jnp.exp2(bsum[lo:hi] - ref_pt)).astype(BF16)
                kh = (kin[:hi] * jnp.exp2(ref_pt - bsum[:hi])).astype(BF16)
                pieces = [qh if j == i else jnp.zeros((sub, w), BF16) for j in range(nsub)]
                q_rows.append(jnp.concatenate(pieces, axis=1))
                k_cols.append(kh if hi == ck else jnp.concatenate([kh, jnp.zeros((ck - hi, w), BF16)], axis=0))
            q_blk.append(jnp.concatenate(q_rows, axis=0))
            k_blk.append(jnp.concatenate(k_cols, axis=1))
            last = bsum[ck - 1:ck, :]
            kdecs.append((kin * jnp.exp2(last - bsum)).astype(BF16))
            d_s[c] = jnp.exp2(last)
        vbs = [i_ref[r, :] for r in rows]
        atts = [lax.dot_general(qb, kb, nt, preferred_element_type=F32) for qb, kb in zip(q_blk, k_blk)]
        for c, vb, kdec in zip(chunks, vbs, kdecs):
            u_s[c] = lax.dot_general(vb, kdec, tn, preferred_element_type=F32)
        probs = [jnp.where(causal, att, 0.0).astype(BF16) for att in atts]
        for r, p, vb in zip(rows, probs, vbs):
            oi_s[r, :] = jnp.dot(p, vb, preferred_element_type=F32)
        return 0

    lax.fori_loop(0, seq // (ck * HGRN_UNROLL), phase1, 0)

    def phase2(c2, state_t):
        for u in range(HGRN_UNROLL):
            c = c2 * HGRN_UNROLL + u
            rows = pl.ds(pl.multiple_of(c * ck, ck), ck)
            o = oi_s[rows, :] + lax.dot_general(qd_s[rows, :], state_t.astype(BF16), nt,
                                                preferred_element_type=F32)
            o = _rms(o, RMS_EPS) * gw * _silu(g_ref[rows, :].astype(F32))
            o_ref[rows, :] = o.astype(BF16)
            state_t = state_t * d_s[c] + u_s[c]
        return state_t

    lax.fori_loop(0, seq // (ck * HGRN_UNROLL), phase2, jnp.zeros((HGRN_V_DIM, HGRN_EXPAND), F32))


def _hgrn(hq, hf, hi, hg, lb_logits, gnorm_w, layer):
    b, s, _ = hq.shape
    w = HGRN_EXPAND
    blk = pl.BlockSpec((None, s, w), lambda bi, h: (bi, 0, h))
    return pl.pallas_call(
        functools.partial(_hgrn_kernel, layer=layer),
        out_shape=jax.ShapeDtypeStruct(hq.shape, BF16),
        grid=(b, N_HGRN_HEADS),
        in_specs=[blk, blk, blk, blk,
                  pl.BlockSpec((lb_logits.shape[0], w), lambda bi, h: (0, h)),
                  pl.BlockSpec((1, w), lambda bi, h: (0, 0))],
        out_specs=blk,
        scratch_shapes=[pltpu.VMEM((s, w), BF16), pltpu.VMEM((s, HGRN_V_DIM), F32),
                        pltpu.VMEM((s // HGRN_CHUNK, HGRN_V_DIM, w), F32),
                        pltpu.VMEM((s // HGRN_CHUNK, 1, w), F32)],
        compiler_params=_params(("arbitrary",) * 2),
        name="hgrn",
    )(hq, hf, hi, hg, lb_logits, gnorm_w.reshape(1, w))


def _route(logits):
    lane = lax.broadcasted_iota(jnp.int32, logits.shape, 1)
    big = jnp.int32(LANES)
    neg = -jnp.inf
    is_g = lane < N_GROUPS
    gl = jnp.where(is_g, logits, neg)
    gmax = jnp.max(gl, axis=-1, keepdims=True)
    gsel = jnp.min(jnp.where(gl == gmax, lane, big), axis=-1, keepdims=True)
    pg = 1.0 / jnp.sum(jnp.where(is_g, jnp.exp(logits - gmax), 0.0), axis=-1, keepdims=True)
    lo = ROUTER_LANE0 + EXPERTS_PER_GROUP * gsel
    el = jnp.where((lane >= lo) & (lane < lo + EXPERTS_PER_GROUP), logits, neg)
    v0 = jnp.max(el, axis=-1, keepdims=True)
    i0 = jnp.min(jnp.where(el == v0, lane, big), axis=-1, keepdims=True)
    el = jnp.where(lane == i0, neg, el)
    v1 = jnp.max(el, axis=-1, keepdims=True)
    i1 = jnp.min(jnp.where(el == v1, lane, big), axis=-1, keepdims=True)
    t = jnp.exp(v1 - v0)
    return lane, i0, i1, pg / (1.0 + t), pg * t / (1.0 + t)


def _outproj_kernel(a_ref, r_ref, x_ref, g1_ref, sc_ref, sh_ref, nw_ref, wo_ref, wr_ref, br_ref,
                    x1_ref, h2_ref, route_ref, cnt_ref, run_ref):
    i = pl.program_id(0)
    sub = x_ref.shape[0] // OUTPROJ_CHAINS
    half = a_ref.shape[1]
    blocks = [pl.ds(u * sub, sub) for u in range(OUTPROJ_CHAINS)]

    @pl.when(i == 0)
    def _():
        run_ref[...] = jnp.zeros_like(run_ref)

    mixes = [jnp.dot(a_ref[rows, :], wo_ref[:half, :], preferred_element_type=F32)
             + jnp.dot(r_ref[rows, :], wo_ref[half:, :], preferred_element_type=F32) for rows in blocks]
    splits = []
    for rows, mix in zip(blocks, mixes):
        x1 = x_ref[rows, :] + g1_ref[...] * mix
        x1_ref[rows, :] = x1
        h2 = _rms(x1, RMS_EPS) * nw_ref[...]
        h2 = h2 * (1.0 + sc_ref[...]) + sh_ref[...]
        h2_ref[rows, :] = h2
        h_hi = h2.astype(BF16)
        splits.append((h_hi, (h2 - h_hi.astype(F32)).astype(BF16)))
    parts = [jnp.dot(h_hi, wr_ref[...], preferred_element_type=F32)
             + jnp.dot(h_lo, wr_ref[...], preferred_element_type=F32) for h_hi, h_lo in splits]
    routed = [_route(part[:, :LANES] + part[:, LANES:] + br_ref[...]) for part in parts]
    before = jnp.where(lax.broadcasted_iota(jnp.int32, (sub, sub), 1)
                       < lax.broadcasted_iota(jnp.int32, (sub, sub), 0), 1.0, 0.0).astype(BF16)
    onehots = [jnp.where((lane == i0) | (lane == i1), 1.0, 0.0) for lane, i0, i1, _, _ in routed]
    prefixes = [jnp.dot(before, oh.astype(BF16), preferred_element_type=F32) for oh in onehots]
    run = run_ref[...]
    for rows, (lane, i0, i1, w0, w1), onehot, prefix in zip(blocks, routed, onehots, prefixes):
        prefix = prefix + run
        rank0 = jnp.sum(jnp.where(lane == i0, prefix, 0.0), axis=-1, keepdims=True)
        rank1 = jnp.sum(jnp.where(lane == i1, prefix, 0.0), axis=-1, keepdims=True)
        run = run + jnp.sum(onehot, axis=0, keepdims=True)
        e0 = (i0 - ROUTER_LANE0).astype(F32)
        e1 = (i1 - ROUTER_LANE0).astype(F32)
        slab = jnp.zeros(lane.shape, F32)
        for ln, val in enumerate((e0, e1, w0, w1, rank0, rank1)):
            slab = jnp.where(lane == ln, val, slab)
        route_ref[rows, :] = slab
    run_ref[...] = run
    cnt_ref[...] = run


def _outproj(a2, r2, x2, mod4, norm_w, w_out_bf, w_router, b_router, seq):
    n, d = x2.shape
    tm = min(OUTPROJ_TILE, seq)
    tpb = seq // tm
    row = lambda i: (i, 0)
    fixed = lambda i: (0, 0)
    modspec = lambda j: pl.BlockSpec((None, None, 1, d), lambda i: (i // tpb, j, 0, 0))
    return pl.pallas_call(
        _outproj_kernel,
        out_shape=(jax.ShapeDtypeStruct((n, d), F32), jax.ShapeDtypeStruct((n, d), F32),
                   jax.ShapeDtypeStruct((n, LANES), F32), jax.ShapeDtypeStruct((1, LANES), F32)),
        grid=(n // tm,),
        in_specs=[pl.BlockSpec((tm, a2.shape[1]), row), pl.BlockSpec((tm, r2.shape[1]), row),
                  pl.BlockSpec((tm, d), row), modspec(2), modspec(4), modspec(3),
                  pl.BlockSpec((1, d), fixed), pl.BlockSpec(w_out_bf.shape, fixed),
                  pl.BlockSpec(w_router.shape, fixed), pl.BlockSpec((1, LANES), fixed)],
        out_specs=(pl.BlockSpec((tm, d), row), pl.BlockSpec((tm, d), row),
                   pl.BlockSpec((tm, LANES), row), pl.BlockSpec((1, LANES), fixed)),
        scratch_shapes=[pltpu.VMEM((1, LANES), F32)],
        compiler_params=_params(("arbitrary",)),
        name="outproj",
    )(a2, r2, x2, mod4, mod4, mod4, norm_w.reshape(1, d), w_out_bf, w_router, b_router)


def _row_copy(src_ref, src_row, dst_ref, dst_row, sem):
    return pltpu.make_async_copy(src_ref.at[pl.ds(src_row, 1), :], dst_ref.at[pl.ds(dst_row, 1), :], sem)


def _dispatch_kernel(pad_ref, len_ref, nu_ref, dest_ref, h_ref, xs_ref, zero_s, sem):
    tm = h_ref.shape[0]

    @pl.when(pl.program_id(0) == 0)
    def _():
        zero_s[...] = jnp.zeros_like(zero_s)
        tails = []
        for e in range(N_EXPERTS):
            pos = pad_ref[e]
            for piece in PAD_PIECES:
                hit = (len_ref[e] & piece) != 0
                tails.append((hit, pltpu.make_async_copy(
                    zero_s.at[pl.ds(0, piece), :],
                    xs_ref.at[pl.ds(pl.multiple_of(pos, SUBLANES), piece), :], sem)))
                pos = pos + jnp.where(hit, piece, 0)
        n_tiles = xs_ref.shape[0] // EXPERT_TILE
        for t in range(N_EXPERTS):
            tile = nu_ref[0] + t
            row0 = pl.multiple_of(jnp.minimum(tile, n_tiles - 1) * EXPERT_TILE, EXPERT_TILE)
            tails.append((tile < n_tiles, pltpu.make_async_copy(
                zero_s, xs_ref.at[pl.ds(row0, EXPERT_TILE), :], sem)))
        for hit, cp in tails:
            pl.when(hit)(cp.start)
        for hit, cp in tails:
            pl.when(hit)(cp.wait)

    def issue(g, _):
        for u in range(DMA_UNROLL):
            r = g * DMA_UNROLL + u
            _row_copy(h_ref, r, xs_ref, dest_ref[0, r], sem).start()
            _row_copy(h_ref, r, xs_ref, dest_ref[0, tm + r], sem).start()
        return 0

    lax.fori_loop(0, tm // DMA_UNROLL, issue, 0)

    def drain(g, _):
        for _u in range(2 * DMA_UNROLL):
            _row_copy(h_ref, 0, xs_ref, 0, sem).wait()
        return 0

    lax.fori_loop(0, tm // DMA_UNROLL, drain, 0)


def _dispatch(plan, h2):
    n, d = h2.shape
    dest3 = plan["dest3"]
    tm = dest3.shape[2] // 2
    rows = plan["tile_expert"].shape[0] * EXPERT_TILE
    grid_spec = pltpu.PrefetchScalarGridSpec(
        num_scalar_prefetch=3,
        grid=(n // tm,),
        in_specs=[pl.BlockSpec((None, 1, 2 * tm), lambda i, pad, ln, nu: (i, 0, 0), memory_space=pltpu.SMEM),
                  pl.BlockSpec((tm, d), lambda i, pad, ln, nu: (i, 0))],
        out_specs=pl.BlockSpec(memory_space=pl.ANY),
        scratch_shapes=[pltpu.VMEM((EXPERT_TILE, d), F32), pltpu.SemaphoreType.DMA],
    )
    return pl.pallas_call(
        _dispatch_kernel,
        out_shape=jax.ShapeDtypeStruct((rows, d), F32),
        grid_spec=grid_spec,
        compiler_params=_params(("arbitrary",), has_side_effects=True),
        name="dispatch",
    )(plan["pad_start"], plan["pad_len"], plan["n_used"], dest3, h2)


def _experts_kernel(te_ref, nu_ref, seg_ref, nxt_ref, xs_ref, wg_hbm, wu_hbm, wd_hbm, ys_ref,
                    wg_f, wu_f, wd_f, wg_s, wu_s, wd_s, sem):
    i = pl.program_id(0)

    def fetch(e, sl):
        return [pltpu.make_async_copy(src.at[e], dst.at[sl], sem.at[sl])
                for src, dst in ((wg_hbm, wg_f), (wu_hbm, wu_f), (wd_hbm, wd_f))]

    @pl.when(i < nu_ref[0])
    def _():
        prev = te_ref[jnp.maximum(i - 1, 0)]

        @pl.when(i == 0)
        def _():
            for cp in fetch(te_ref[0], 0):
                cp.start(priority=1)

        @pl.when((i == 0) | (te_ref[i] != prev))
        def _():
            sl = seg_ref[i] % 2
            for cp in fetch(te_ref[i], sl):
                cp.wait()

            @pl.when(nxt_ref[i] >= 0)
            def _():
                for cp in fetch(nxt_ref[i], 1 - sl):
                    cp.start(priority=1)

            wg_s[...] = wg_f[sl].astype(BF16)
            wu_s[...] = wu_f[sl].astype(BF16)
            wd_s[...] = wd_f[sl].astype(BF16)

        x = xs_ref[...].astype(BF16)
        a = jnp.dot(x, wg_s[...], preferred_element_type=F32)
        u = jnp.dot(x, wu_s[...], preferred_element_type=F32)
        hid = (_silu(a) * u).astype(BF16)
        ys_ref[...] = jnp.dot(hid, wd_s[...], preferred_element_type=F32)

    @pl.when(i >= nu_ref[0])
    def _():
        ys_ref[...] = jnp.zeros_like(ys_ref)


def _experts(plan, xs, w_gate, w_up, w_down):
    d = xs.shape[1]
    e, _, f = w_gate.shape
    tm = EXPERT_TILE
    n_tiles = plan["tile_expert"].shape[0]
    row = lambda i, te, nu, seg, nxt: (jnp.minimum(i, nu[0] - 1), 0)
    hbm = pl.BlockSpec(memory_space=pl.ANY)
    grid_spec = pltpu.PrefetchScalarGridSpec(
        num_scalar_prefetch=4,
        grid=(n_tiles,),
        in_specs=[pl.BlockSpec((tm, d), row), hbm, hbm, hbm],
        out_specs=pl.BlockSpec((tm, d), lambda i, te, nu, seg, nxt: (i, 0)),
        scratch_shapes=[pltpu.VMEM((2, d, f), F32), pltpu.VMEM((2, d, f), F32), pltpu.VMEM((2, f, d), F32),
                        pltpu.VMEM((d, f), BF16), pltpu.VMEM((d, f), BF16), pltpu.VMEM((f, d), BF16),
                        pltpu.SemaphoreType.DMA((2,))],
    )
    return pl.pallas_call(
        _experts_kernel,
        out_shape=jax.ShapeDtypeStruct((n_tiles * tm, d), F32),
        grid_spec=grid_spec,
        compiler_params=_params(("arbitrary",)),
        name="experts",
    )(plan["tile_expert"], plan["n_used"], plan["segment"], plan["next_expert"], xs, w_gate, w_up, w_down)


def _combine_kernel(dest_ref, dnext_ref, ys_ref, x1_ref, route_ref, g2_ref, fw_ref, o_ref, y0_s, y1_s, sem):
    tm = x1_ref.shape[0]
    i = pl.program_id(0)
    slot = i % 2

    def gather(idx_ref, sl):
        def issue(g, _):
            for u in range(DMA_UNROLL):
                r = g * DMA_UNROLL + u
                _row_copy(ys_ref, idx_ref[0, r], y0_s.at[sl], r, sem.at[sl]).start()
                _row_copy(ys_ref, idx_ref[0, tm + r], y1_s.at[sl], r, sem.at[sl]).start()
            return 0

        lax.fori_loop(0, tm // DMA_UNROLL, issue, 0)

    @pl.when(i == 0)
    def _():
        gather(dest_ref, 0)

    @pl.when(i + 1 < pl.num_programs(0))
    def _():
        gather(dnext_ref, 1 - slot)

    def drain(g, _):
        for _u in range(DMA_UNROLL):
            _row_copy(ys_ref, 0, y0_s.at[slot], 0, sem.at[slot]).wait()
            _row_copy(ys_ref, 0, y1_s.at[slot], 0, sem.at[slot]).wait()
        return 0

    lax.fori_loop(0, tm // DMA_UNROLL, drain, 0)
    route = route_ref[...]
    w0 = route[:, 2:3]
    w1 = route[:, 3:4]
    x = x1_ref[...] + g2_ref[...] * (w0 * y0_s[slot] + w1 * y1_s[slot])
    o_ref[...] = _rms(x, RMS_EPS) * fw_ref[...]


def _combine(dest3, ys, x1, route, mod4, final_w, seq):
    n, d = x1.shape
    tm = dest3.shape[2] // 2
    tpb = seq // tm
    row = lambda i: (i, 0)
    steps = n // tm
    return pl.pallas_call(
        _combine_kernel,
        out_shape=jax.ShapeDtypeStruct((n, d), F32),
        grid=(steps,),
        in_specs=[pl.BlockSpec((None, 1, 2 * tm), lambda i: (i, 0, 0), memory_space=pltpu.SMEM),
                  pl.BlockSpec((None, 1, 2 * tm), lambda i: (jnp.minimum(i + 1, steps - 1), 0, 0),
                               memory_space=pltpu.SMEM),
                  pl.BlockSpec(memory_space=pl.ANY),
                  pl.BlockSpec((tm, d), row), pl.BlockSpec((tm, LANES), row),
                  pl.BlockSpec((None, None, 1, d), lambda i: (i // tpb, 5, 0, 0)),
                  pl.BlockSpec((1, d), lambda i: (0, 0))],
        out_specs=pl.BlockSpec((tm, d), row),
        scratch_shapes=[pltpu.VMEM((2, tm, d), F32), pltpu.VMEM((2, tm, d), F32),
                        pltpu.SemaphoreType.DMA((2,))],
        compiler_params=_params(("arbitrary",)),
        name="combine",
    )(dest3, dest3, ys, x1, route, mod4, final_w.reshape(1, d))


PLAN_LANES = 2 * LANES


def _plan_kernel(cnt_ref, route_ref, dest_ref, plan_ref):
    cnt = cnt_ref[...]
    tiles = jnp.floor((cnt + (EXPERT_TILE - 1.0)) * (1.0 / EXPERT_TILE))
    sq = (LANES, LANES)
    row = lax.broadcasted_iota(jnp.int32, sq, 0)
    col = lax.broadcasted_iota(jnp.int32, sq, 1)
    tiles_b = jnp.broadcast_to(tiles, sq)
    tiles_col = jnp.sum(jnp.where(col == row, tiles_b, 0.0), axis=1, keepdims=True)
    end_col = jnp.sum(jnp.where(col <= row, tiles_b, 0.0), axis=1, keepdims=True)
    end_row = jnp.sum(jnp.where(row <= col, jnp.broadcast_to(tiles_col, sq), 0.0), axis=0, keepdims=True)
    start_row = end_row - tiles

    route_t = route_ref[...].T
    blk = route_t.shape[1]
    sub = lax.broadcasted_iota(jnp.int32, (LANES, blk), 0)
    start8 = jnp.broadcast_to(start_row, (SUBLANES, LANES)).astype(BF16)
    dest_ref[...] = jnp.zeros_like(dest_ref)
    for k in range(2):
        e_sub = route_t[k:k + 1, :].astype(jnp.int32) + ROUTER_LANE0
        onehot = jnp.where(sub == e_sub, 1.0, 0.0).astype(BF16)
        start = jnp.dot(start8, onehot, preferred_element_type=F32)[0:1, :]
        dest_ref[k:k + 1, :] = (route_t[4 + k:5 + k, :] + EXPERT_TILE * start).astype(jnp.int32)

    @pl.when(pl.program_id(0) == 0)
    def _():
        wide = (LANES, PLAN_LANES)
        t = lax.broadcasted_iota(jnp.int32, wide, 1).astype(F32)
        e_id = lax.broadcasted_iota(jnp.int32, wide, 0) - ROUTER_LANE0
        is_e = (e_id >= 0) & (e_id < N_EXPERTS)
        n_used = jnp.max(end_row, axis=1, keepdims=True)
        t = jnp.minimum(t, n_used - 1.0)
        end_w = jnp.broadcast_to(end_col, wide)
        has_w = is_e & (jnp.broadcast_to(tiles_col, wide) > 0.0)
        te = jnp.sum(jnp.where(is_e & (end_w <= t), 1, 0), axis=0, keepdims=True)
        seg = jnp.sum(jnp.where(has_w & (e_id < te), 1, 0), axis=0, keepdims=True)
        nxt = jnp.min(jnp.where(has_w & (e_id > te), e_id, N_EXPERTS), axis=0, keepdims=True)
        nxt = jnp.where(nxt == N_EXPERTS, -1, nxt)
        pad_start = jnp.floor((start_row * EXPERT_TILE + cnt) * (1.0 / SUBLANES)) * SUBLANES
        pad_len = end_row * EXPERT_TILE - pad_start
        zeros = jnp.zeros((1, PLAN_LANES - LANES), jnp.int32)
        plan_ref[...] = jnp.zeros_like(plan_ref)
        plan_ref[0:1, :] = te
        plan_ref[1:2, :] = seg
        plan_ref[2:3, :] = nxt
        plan_ref[3:4, :] = jnp.concatenate([pad_start.astype(jnp.int32), zeros], axis=1)
        plan_ref[4:5, :] = jnp.concatenate([pad_len.astype(jnp.int32), zeros], axis=1)
        plan_ref[5:6, :] = jnp.broadcast_to(n_used, (1, PLAN_LANES)).astype(jnp.int32)


def _plan(route, counts_row, seq):
    n = route.shape[0]
    tm = min(ROW_TILE, seq)
    n_tiles = (2 * n) // EXPERT_TILE + N_EXPERTS
    assert n_tiles <= PLAN_LANES
    blk = min(n, 2048)
    dest, plan = pl.pallas_call(
        _plan_kernel,
        out_shape=(jax.ShapeDtypeStruct((SUBLANES, n), jnp.int32),
                   jax.ShapeDtypeStruct((SUBLANES, PLAN_LANES), jnp.int32)),
        grid=(n // blk,),
        in_specs=[pl.BlockSpec((1, LANES), lambda i: (0, 0)), pl.BlockSpec((blk, LANES), lambda i: (i, 0))],
        out_specs=(pl.BlockSpec((SUBLANES, blk), lambda i: (0, i)),
                   pl.BlockSpec((SUBLANES, PLAN_LANES), lambda i: (0, 0))),
        compiler_params=_params(("arbitrary",)),
        name="plan",
    )(counts_row, route)
    dest3 = jnp.concatenate([dest[0].reshape(n // tm, 1, tm), dest[1].reshape(n // tm, 1, tm)], axis=2)
    experts = slice(ROUTER_LANE0, ROUTER_LANE0 + N_EXPERTS)
    return {"dest3": dest3, "tile_expert": plan[0, :n_tiles], "segment": plan[1, :n_tiles],
            "next_expert": plan[2, :n_tiles], "pad_start": plan[3, experts], "pad_len": plan[4, experts],
            "n_used": plan[5, :1]}


def kernel(x, c, positions, w_ada, b_ada, norm1_w, w_in, lambda_q1, lambda_k1, lambda_q2, lambda_k2, subln_w, hgrn_lb_logits, gnorm_w, w_out, norm2_w, w_group_router, b_group_router, w_expert_router, b_expert_router, w_gate, w_up, w_down, final_norm_w):
    b, s, d = x.shape
    n = b * s
    depth = w_ada.shape[0]
    assert depth == 1, "the combine kernel applies the final norm, so exactly one layer is supported"
    half = DIFF_HEAD_DIM // 2

    inv_freq = ROPE_THETA ** (-jnp.arange(half, dtype=F32) / half)
    pos = jnp.broadcast_to(positions.astype(F32).reshape(n, 1), (n, LANES))
    freq = jnp.tile(inv_freq, LANES // half).reshape(1, LANES)
    sign = jnp.tile(jnp.concatenate([-jnp.ones((half,), F32), jnp.ones((half,), F32)]),
                    LANES // DIFF_HEAD_DIM).reshape(1, LANES)

    xf = x.reshape(n, d)
    for l in range(depth):
        lambda_init = 0.8 - 0.6 * math.exp(-0.3 * l)
        mod4 = _mod(c, w_ada[l], b_ada[l]).reshape(b, 6, 1, d)
        q, k, v, hq, hf, hi, hg = _inproj(xf, mod4, norm1_w[l], w_in[l].astype(BF16), pos, freq, sign, s)
        to3 = lambda t: t.reshape(b, s, SEG)
        a = _attn(to3(q), to3(k), to3(v), lambda_q1[l], lambda_k1[l], lambda_q2[l], lambda_k2[l],
                  subln_w[l], lambda_init)
        r = _hgrn(to3(hq), to3(hf), to3(hi), to3(hg), hgrn_lb_logits, gnorm_w[l], l)

        w_router = jnp.zeros((d, LANES), F32)
        w_router = w_router.at[:, :N_GROUPS].set(w_group_router[l])
        w_router = w_router.at[:, ROUTER_LANE0:ROUTER_LANE0 + N_EXPERTS].set(w_expert_router[l])
        b_router = jnp.zeros((1, LANES), F32)
        b_router = b_router.at[0, :N_GROUPS].set(b_group_router[l])
        b_router = b_router.at[0, ROUTER_LANE0:ROUTER_LANE0 + N_EXPERTS].set(b_expert_router[l].reshape(-1))
        wr_hi = w_router.astype(BF16)
        wr_split = jnp.concatenate([wr_hi, (w_router - wr_hi.astype(F32)).astype(BF16)], axis=1)
        x1, h2, route, counts = _outproj(a.reshape(n, SEG), r.reshape(n, SEG), xf, mod4, norm2_w[l],
                                         w_out[l].astype(BF16), wr_split, b_router, s)

        plan = _plan(route, counts, s)
        xs = _dispatch(plan, h2)
        e_w = lambda w: w.reshape((N_EXPERTS,) + w.shape[2:])
        ys = _experts(plan, xs, e_w(w_gate[l]), e_w(w_up[l]), e_w(w_down[l]))
        xf = _combine(plan["dest3"], ys, x1, route, mod4, final_norm_w, s)
    return xf.reshape(b, s, d)
```

```python
import functools
import math

import jax
import jax.numpy as jnp
from jax import lax
from jax.experimental import pallas as pl
from jax.experimental.pallas import tpu as pltpu

F32 = jnp.float32
BF16 = jnp.bfloat16
HIGHEST = lax.Precision.HIGHEST

N_DIFF_HEADS = 4
DIFF_HEAD_DIM = 64
DIFF_V_DIM = 2 * DIFF_HEAD_DIM
N_HGRN_HEADS = 4
HGRN_EXPAND = 128
HGRN_V_DIM = 128
ROPE_THETA = 10000.0
N_GROUPS = 4
EXPERTS_PER_GROUP = 8
N_EXPERTS = N_GROUPS * EXPERTS_PER_GROUP
RMS_EPS = 1e-6
SUBLN_EPS = 1e-5
SEG = 512
LANES = 128
SUBLANES = 8
ROUTER_LANE0 = N_GROUPS

ROW_TILE = 256
OUTPROJ_TILE = 1024
OUTPROJ_CHAINS = 4
ATTN_TILE = 512
ATTN_KV_TILE = 256
ATTN_ONES_ROWS = 16
LOG2E = math.log2(math.e)
HGRN_CHUNK = 64
HGRN_SUB = 16
HGRN_UNROLL = 8
EXPERT_TILE = 256
PAD_PIECES = (256, 128, 64, 32, 16, 8)
DMA_UNROLL = 16
VMEM_LIMIT = 48 * 1024 * 1024


def _params(sem, **kw):
    return pltpu.CompilerParams(dimension_semantics=sem, vmem_limit_bytes=VMEM_LIMIT, **kw)


def _silu(x):
    return x * jax.nn.sigmoid(x)


def _rms(x, eps):
    return x * lax.rsqrt(jnp.mean(x * x, axis=-1, keepdims=True) + eps)


def _mod_kernel(c_ref, w_ref, b_ref, o_ref):
    ca = _silu(c_ref[...])
    o_ref[...] = jnp.dot(ca, w_ref[...], preferred_element_type=F32, precision=HIGHEST) + b_ref[...]


def _mod(c, w_ada, b_ada):
    bsz, d = c.shape
    n_out = w_ada.shape[1]
    return pl.pallas_call(
        _mod_kernel,
        out_shape=jax.ShapeDtypeStruct((bsz, n_out), F32),
        grid=(n_out // d,),
        in_specs=[pl.BlockSpec((bsz, d), lambda j: (0, 0)),
                  pl.BlockSpec((d, d), lambda j: (0, j)),
                  pl.BlockSpec((1, d), lambda j: (0, j))],
        out_specs=pl.BlockSpec((bsz, d), lambda j: (0, j)),
        compiler_params=_params(("arbitrary",)),
        name="mod",
    )(c, w_ada, b_ada.reshape(1, n_out))


def _inproj_kernel(x_ref, sc_ref, sh_ref, nw_ref, w_ref, pos_ref, freq_ref, sign_ref,
                   q_ref, k_ref, v_ref, hq_ref, hf_ref, hi_ref, hg_ref):
    h = _rms(x_ref[...], RMS_EPS) * nw_ref[...]
    h = h * (1.0 + sc_ref[...]) + sh_ref[...]
    hb = h.astype(BF16)
    ang = pos_ref[...] * freq_ref[...]
    cos = jnp.cos(ang)
    sin = jnp.sin(ang) * sign_ref[...]
    lane = lax.broadcasted_iota(jnp.int32, cos.shape, 1)
    first_half = (lane % DIFF_HEAD_DIM) < (DIFF_HEAD_DIM // 2)

    def seg(i):
        return jnp.dot(hb, w_ref[:, i * SEG:(i + 1) * SEG], preferred_element_type=F32)

    def rope(p, scale):
        outs = []
        for c in range(SEG // LANES):
            pc = p[:, c * LANES:(c + 1) * LANES]
            partner = jnp.where(first_half, pltpu.roll(pc, LANES - DIFF_HEAD_DIM // 2, 1),
                                pltpu.roll(pc, DIFF_HEAD_DIM // 2, 1))
            outs.append((pc * cos + partner * sin) * scale)
        return jnp.concatenate(outs, axis=1)

    q_ref[...] = rope(seg(0), DIFF_HEAD_DIM ** -0.5 * LOG2E).astype(BF16)
    k_ref[...] = rope(seg(1), 1.0).astype(BF16)
    v_ref[...] = seg(2).astype(BF16)
    hq_ref[...] = seg(3).astype(BF16)
    hf_ref[...] = seg(4)
    hi_ref[...] = seg(5).astype(BF16)
    hg_ref[...] = seg(6).astype(BF16)


def _inproj(x2, mod4, norm_w, w_in_bf, pos, freq, sign, seq):
    n, d = x2.shape
    tm = min(ROW_TILE, seq)
    tpb = seq // tm
    row = lambda i: (i, 0)
    modspec = lambda j: pl.BlockSpec((None, None, 1, d), lambda i: (i // tpb, j, 0, 0))
    seg_spec = pl.BlockSpec((tm, SEG), row)
    bf = jax.ShapeDtypeStruct((n, SEG), BF16)
    return pl.pallas_call(
        _inproj_kernel,
        out_shape=(bf, bf, bf, bf, jax.ShapeDtypeStruct((n, SEG), F32), bf, bf),
        grid=(n // tm,),
        in_specs=[pl.BlockSpec((tm, d), row), modspec(1), modspec(0),
                  pl.BlockSpec((1, d), lambda i: (0, 0)),
                  pl.BlockSpec(w_in_bf.shape, lambda i: (0, 0)),
                  pl.BlockSpec((tm, LANES), row),
                  pl.BlockSpec((1, LANES), lambda i: (0, 0)), pl.BlockSpec((1, LANES), lambda i: (0, 0))],
        out_specs=(seg_spec,) * 7,
        compiler_params=_params(("arbitrary",)),
        name="inproj",
    )(x2, mod4, mod4, norm_w.reshape(1, d), w_in_bf, pos, freq, sign)


def _attn_kernel(q_ref, k_ref, v_ref, lq1_ref, lk1_ref, lq2_ref, lk2_ref, sw_ref, o_ref,
                 vt_s, acc_s, *, lambda_init):
    tq = q_ref.shape[0]
    seq = k_ref.shape[0]
    tk = min(ATTN_KV_TILE, tq)
    vd = DIFF_V_DIM
    qi = pl.program_id(1)
    nt = (((1,), (1,)), ((), ()))
    chains = [(h, mp) for h in range(N_DIFF_HEADS) for mp in range(2)]

    va = vd + ATTN_ONES_ROWS

    @pl.when(qi == 0)
    def _():
        for h in range(N_DIFF_HEADS):
            vt_s[h * va + vd:(h + 1) * va, :] = jnp.ones((ATTN_ONES_ROWS, seq), BF16)
            for c in range(seq // tk):
                vt_s[h * va:h * va + vd, c * tk:(c + 1) * tk] = (
                    v_ref[c * tk:(c + 1) * tk, h * vd:(h + 1) * vd].astype(F32).T.astype(BF16))

    acc_s[...] = jnp.zeros_like(acc_s)
    lane = lax.broadcasted_iota(jnp.int32, (tq, vd), 1)
    qz = []
    for h, mp in chains:
        qh = q_ref[:, h * vd:(h + 1) * vd]
        keep = (lane < DIFF_HEAD_DIM) if mp == 0 else (lane >= DIFF_HEAD_DIM)
        qz.append(jnp.where(keep, qh, jnp.zeros_like(qh)))

    def step(j, ms, c0):
        row0 = pl.multiple_of(j * tk, tk)
        q0 = c0 or 0
        scores = [lax.dot_general(k_ref[pl.ds(row0, tk), h * vd:(h + 1) * vd], qz[c][q0:], nt,
                                  preferred_element_type=F32) for c, (h, _) in enumerate(chains)]
        new_ms, alphas, probs = [], [], []
        for c, s in enumerate(scores):
            if c0 is not None:
                kidx = lax.broadcasted_iota(jnp.int32, s.shape, 0)
                qidx = lax.broadcasted_iota(jnp.int32, s.shape, 1)
                s = jnp.where(kidx <= qidx, s, -jnp.inf)
            m_old = ms[c][:, q0:]
            m_new = jnp.maximum(m_old, jnp.max(s, axis=0, keepdims=True))
            alphas.append(jnp.exp2(m_old - m_new))
            probs.append(jnp.exp2(s - m_new).astype(BF16))
            new_ms.append(m_new if q0 == 0 else jnp.concatenate([ms[c][:, :q0], m_new], axis=1))
        pvs = [jnp.dot(vt_s[h * va:(h + 1) * va, pl.ds(row0, tk)], probs[c], preferred_element_type=F32)
               for c, (h, _) in enumerate(chains)]
        for c, pv in enumerate(pvs):
            acc_s[c, :, q0:] = alphas[c] * acc_s[c, :, q0:] + pv
        return tuple(new_ms)

    init = tuple(jnp.full((1, tq), -jnp.inf, F32) for _ in chains)
    per_q = tq // tk
    ms = lax.fori_loop(0, qi * per_q, lambda j, c: step(j, c, None), init)
    for d in range(per_q):
        ms = step(qi * per_q + d, ms, d * tk)

    lam = (jnp.exp(jnp.sum(lq1_ref[...] * lk1_ref[...], axis=-1, keepdims=True))
           - jnp.exp(jnp.sum(lq2_ref[...] * lk2_ref[...], axis=-1, keepdims=True)) + lambda_init)
    sw = jnp.concatenate([sw_ref[...]] * (tq // LANES), axis=1)
    for h in range(N_DIFF_HEADS):
        inv1 = 1.0 / acc_s[2 * h, vd:vd + 1, :]
        inv2 = 1.0 / acc_s[2 * h + 1, vd:vd + 1, :]
        o = acc_s[2 * h, :vd, :] * inv1 - lam * (acc_s[2 * h + 1, :vd, :] * inv2)
        o = o * lax.rsqrt(jnp.mean(o * o, axis=0, keepdims=True) + SUBLN_EPS)
        o_ref[:, h * vd:(h + 1) * vd] = (o * sw * (1.0 - lambda_init)).T.astype(BF16)


def _attn(q, k, v, lq1, lk1, lq2, lk2, subln_w, lambda_init):
    b, s, width = q.shape
    tq = min(ATTN_TILE, s)
    vec = lambda a: a.reshape(1, -1)
    small = lambda w: pl.BlockSpec((1, w), lambda bi, i: (0, 0))
    kv_spec = pl.BlockSpec((None, s, width), lambda bi, i: (bi, 0, 0))
    q_spec = pl.BlockSpec((None, tq, width), lambda bi, i: (bi, i, 0))
    sw_cols = jnp.broadcast_to(subln_w.reshape(DIFF_V_DIM, 1), (DIFF_V_DIM, LANES))
    return pl.pallas_call(
        functools.partial(_attn_kernel, lambda_init=lambda_init),
        out_shape=jax.ShapeDtypeStruct(q.shape, BF16),
        grid=(b, s // tq),
        in_specs=[q_spec, kv_spec, kv_spec, small(DIFF_HEAD_DIM), small(DIFF_HEAD_DIM),
                  small(DIFF_HEAD_DIM), small(DIFF_HEAD_DIM),
                  pl.BlockSpec((DIFF_V_DIM, LANES), lambda bi, i: (0, 0))],
        out_specs=q_spec,
        scratch_shapes=[pltpu.VMEM((N_DIFF_HEADS * (DIFF_V_DIM + ATTN_ONES_ROWS), s), BF16),
                        pltpu.VMEM((2 * N_DIFF_HEADS, DIFF_V_DIM + ATTN_ONES_ROWS, tq), F32)],
        compiler_params=_params(("arbitrary",) * 2),
        name="attn",
    )(q, k, v, vec(lq1), vec(lk1), vec(lq2), vec(lk2), sw_cols)


def _hgrn_kernel(q_ref, f_ref, i_ref, g_ref, lbl_ref, gw_ref, o_ref, qd_s, oi_s, u_s, d_s, *, layer):
    seq = q_ref.shape[0]
    ck, sub = HGRN_CHUNK, HGRN_SUB
    lg = lbl_ref[...]
    e = jnp.exp(lg - jnp.max(lg, axis=0, keepdims=True))
    lb = jnp.sum(e[:layer + 1], axis=0, keepdims=True) / jnp.sum(e, axis=0, keepdims=True)
    gw = gw_ref[...]
    causal = (lax.broadcasted_iota(jnp.int32, (ck, ck), 1)
              <= lax.broadcasted_iota(jnp.int32, (ck, ck), 0))
    tril = causal.astype(BF16)
    nt = (((1,), (1,)), ((), ()))
    tn = (((0,), (0,)), ((), ()))
    w = HGRN_EXPAND
    nsub = ck // sub

    def phase1(c2, _):
        chunks = [c2 * HGRN_UNROLL + u for u in range(HGRN_UNROLL)]
        rows = [pl.ds(pl.multiple_of(c * ck, ck), ck) for c in chunks]
        kins, splits = [], []
        for r in rows:
            forget = lb + (1.0 - lb) * jax.nn.sigmoid(f_ref[r, :])
            kins.append(1.0 - forget)
            g = jnp.log2(forget)
            g_hi = g.astype(BF16)
            r1 = g - g_hi.astype(F32)
            g_mid = r1.astype(BF16)
            g_lo = (r1 - g_mid.astype(F32)).astype(BF16)
            splits.append(jnp.concatenate([g_hi, g_mid, g_lo], axis=1))
        sums = [jnp.dot(tril, x, preferred_element_type=F32) for x in splits]

        q_blk, k_blk, kdecs = [], [], []
        for c, r, kin, cs in zip(chunks, rows, kins, sums):
            bsum = cs[:, :w] + cs[:, w:2 * w] + cs[:, 2 * w:]
            qf = _silu(q_ref[r, :].astype(F32))
            qd_s[r, :] = (qf * jnp.exp2(bsum)).astype(BF16)
            q_rows, k_cols = [], []
            for i in range(nsub):
                lo, hi = i * sub, (i + 1) * sub
                ref_pt = bsum[lo:lo + 1, :]
                qh = (qf[lo:hi] * jnp.exp2(bsum[lo:hi] - ref_pt)).astype(BF16)
                kh = (kin[:hi] * jnp.exp2(ref_pt - bsum[:hi])).astype(BF16)
                pieces = [qh if j == i else jnp.zeros((sub, w), BF16) for j in range(nsub)]
                q_rows.append(jnp.concatenate(pieces, axis=1))
                k_cols.append(kh if hi == ck else jnp.concatenate([kh, jnp.zeros((ck - hi, w), BF16)], axis=0))
            q_blk.append(jnp.concatenate(q_rows, axis=0))
            k_blk.append(jnp.concatenate(k_cols, axis=1))
            last = bsum[ck - 1:ck, :]
            kdecs.append((kin * jnp.exp2(last - bsum)).astype(BF16))
            d_s[c] = jnp.exp2(last)
        vbs = [i_ref[r, :] for r in rows]
        atts = [lax.dot_general(qb, kb, nt, preferred_element_type=F32) for qb, kb in zip(q_blk, k_blk)]
        for c, vb, kdec in zip(chunks, vbs, kdecs):
            u_s[c] = lax.dot_general(vb, kdec, tn, preferred_element_type=F32)
        probs = [jnp.where(causal, att, 0.0).astype(BF16) for att in atts]
        for r, p, vb in zip(rows, probs, vbs):
            oi_s[r, :] = jnp.dot(p, vb, preferred_element_type=F32)
        return 0

    lax.fori_loop(0, seq // (ck * HGRN_UNROLL), phase1, 0)

    def phase2(c2, state_t):
        for u in range(HGRN_UNROLL):
            c = c2 * HGRN_UNROLL + u
            rows = pl.ds(pl.multiple_of(c * ck, ck), ck)
            o = oi_s[rows, :] + lax.dot_general(qd_s[rows, :], state_t.astype(BF16), nt,
                                                preferred_element_type=F32)
            o = _rms(o, RMS_EPS) * gw * _silu(g_ref[rows, :].astype(F32))
            o_ref[rows, :] = o.astype(BF16)
            state_t = state_t * d_s[c] + u_s[c]
        return state_t

    lax.fori_loop(0, seq // (ck * HGRN_UNROLL), phase2, jnp.zeros((HGRN_V_DIM, HGRN_EXPAND), F32))


def _hgrn(hq, hf, hi, hg, lb_logits, gnorm_w, layer):
    b, s, _ = hq.shape
    w = HGRN_EXPAND
    blk = pl.BlockSpec((None, s, w), lambda bi, h: (bi, 0, h))
    return pl.pallas_call(
        functools.partial(_hgrn_kernel, layer=layer),
        out_shape=jax.ShapeDtypeStruct(hq.shape, BF16),
        grid=(b, N_HGRN_HEADS),
        in_specs=[blk, blk, blk, blk,
                  pl.BlockSpec((lb_logits.shape[0], w), lambda bi, h: (0, h)),
                  pl.BlockSpec((1, w), lambda bi, h: (0, 0))],
        out_specs=blk,
        scratch_shapes=[pltpu.VMEM((s, w), BF16), pltpu.VMEM((s, HGRN_V_DIM), F32),
                        pltpu.VMEM((s // HGRN_CHUNK, HGRN_V_DIM, w), F32),
                        pltpu.VMEM((s // HGRN_CHUNK, 1, w), F32)],
        compiler_params=_params(("arbitrary",) * 2),
        name="hgrn",
    )(hq, hf, hi, hg, lb_logits, gnorm_w.reshape(1, w))


def _route(logits):
    lane = lax.broadcasted_iota(jnp.int32, logits.shape, 1)
    big = jnp.int32(LANES)
    neg = -jnp.inf
    is_g = lane < N_GROUPS
    gl = jnp.where(is_g, logits, neg)
    gmax = jnp.max(gl, axis=-1, keepdims=True)
    gsel = jnp.min(jnp.where(gl == gmax, lane, big), axis=-1, keepdims=True)
    pg = 1.0 / jnp.sum(jnp.where(is_g, jnp.exp(logits - gmax), 0.0), axis=-1, keepdims=True)
    lo = ROUTER_LANE0 + EXPERTS_PER_GROUP * gsel
    el = jnp.where((lane >= lo) & (lane < lo + EXPERTS_PER_GROUP), logits, neg)
    v0 = jnp.max(el, axis=-1, keepdims=True)
    i0 = jnp.min(jnp.where(el == v0, lane, big), axis=-1, keepdims=True)
    el = jnp.where(lane == i0, neg, el)
    v1 = jnp.max(el, axis=-1, keepdims=True)
    i1 = jnp.min(jnp.where(el == v1, lane, big), axis=-1, keepdims=True)
    t = jnp.exp(v1 - v0)
    return lane, i0, i1, pg / (1.0 + t), pg * t / (1.0 + t)


def _outproj_kernel(a_ref, r_ref, x_ref, g1_ref, sc_ref, sh_ref, nw_ref, wo_ref, wr_ref, br_ref,
                    x1_ref, h2_ref, route_ref, cnt_ref, run_ref):
    i = pl.program_id(0)
    sub = x_ref.shape[0] // OUTPROJ_CHAINS
    half = a_ref.shape[1]
    blocks = [pl.ds(u * sub, sub) for u in range(OUTPROJ_CHAINS)]

    @pl.when(i == 0)
    def _():
        run_ref[...] = jnp.zeros_like(run_ref)

    mixes = [jnp.dot(a_ref[rows, :], wo_ref[:half, :], preferred_element_type=F32)
             + jnp.dot(r_ref[rows, :], wo_ref[half:, :], preferred_element_type=F32) for rows in blocks]
    splits = []
    for rows, mix in zip(blocks, mixes):
        x1 = x_ref[rows, :] + g1_ref[...] * mix
        x1_ref[rows, :] = x1
        h2 = _rms(x1, RMS_EPS) * nw_ref[...]
        h2 = h2 * (1.0 + sc_ref[...]) + sh_ref[...]
        h2_ref[rows, :] = h2
        h_hi = h2.astype(BF16)
        splits.append((h_hi, (h2 - h_hi.astype(F32)).astype(BF16)))
    parts = [jnp.dot(h_hi, wr_ref[...], preferred_element_type=F32)
             + jnp.dot(h_lo, wr_ref[...], preferred_element_type=F32) for h_hi, h_lo in splits]
    routed = [_route(part[:, :LANES] + part[:, LANES:] + br_ref[...]) for part in parts]
    before = jnp.where(lax.broadcasted_iota(jnp.int32, (sub, sub), 1)
                       < lax.broadcasted_iota(jnp.int32, (sub, sub), 0), 1.0, 0.0).astype(BF16)
    onehots = [jnp.where((lane == i0) | (lane == i1), 1.0, 0.0) for lane, i0, i1, _, _ in routed]
    prefixes = [jnp.dot(before, oh.astype(BF16), preferred_element_type=F32) for oh in onehots]
    run = run_ref[...]
    for rows, (lane, i0, i1, w0, w1), onehot, prefix in zip(blocks, routed, onehots, prefixes):
        prefix = prefix + run
        rank0 = jnp.sum(jnp.where(lane == i0, prefix, 0.0), axis=-1, keepdims=True)
        rank1 = jnp.sum(jnp.where(lane == i1, prefix, 0.0), axis=-1, keepdims=True)
        run = run + jnp.sum(onehot, axis=0, keepdims=True)
        e0 = (i0 - ROUTER_LANE0).astype(F32)
        e1 = (i1 - ROUTER_LANE0).astype(F32)
        slab = jnp.zeros(lane.shape, F32)
        for ln, val in enumerate((e0, e1, w0, w1, rank0, rank1)):
            slab = jnp.where(lane == ln, val, slab)
        route_ref[rows, :] = slab
    run_ref[...] = run
    cnt_ref[...] = run


def _outproj(a2, r2, x2, mod4, norm_w, w_out_bf, w_router, b_router, seq):
    n, d = x2.shape
    tm = min(OUTPROJ_TILE, seq)
    tpb = seq // tm
    row = lambda i: (i, 0)
    fixed = lambda i: (0, 0)
    modspec = lambda j: pl.BlockSpec((None, None, 1, d), lambda i: (i // tpb, j, 0, 0))
    return pl.pallas_call(
        _outproj_kernel,
        out_shape=(jax.ShapeDtypeStruct((n, d), F32), jax.ShapeDtypeStruct((n, d), F32),
                   jax.ShapeDtypeStruct((n, LANES), F32), jax.ShapeDtypeStruct((1, LANES), F32)),
        grid=(n // tm,),
        in_specs=[pl.BlockSpec((tm, a2.shape[1]), row), pl.BlockSpec((tm, r2.shape[1]), row),
                  pl.BlockSpec((tm, d), row), modspec(2), modspec(4), modspec(3),
                  pl.BlockSpec((1, d), fixed), pl.BlockSpec(w_out_bf.shape, fixed),
                  pl.BlockSpec(w_router.shape, fixed), pl.BlockSpec((1, LANES), fixed)],
        out_specs=(pl.BlockSpec((tm, d), row), pl.BlockSpec((tm, d), row),
                   pl.BlockSpec((tm, LANES), row), pl.BlockSpec((1, LANES), fixed)),
        scratch_shapes=[pltpu.VMEM((1, LANES), F32)],
        compiler_params=_params(("arbitrary",)),
        name="outproj",
    )(a2, r2, x2, mod4, mod4, mod4, norm_w.reshape(1, d), w_out_bf, w_router, b_router)


def _row_copy(src_ref, src_row, dst_ref, dst_row, sem):
    return pltpu.make_async_copy(src_ref.at[pl.ds(src_row, 1), :], dst_ref.at[pl.ds(dst_row, 1), :], sem)


def _dispatch_kernel(pad_ref, len_ref, nu_ref, dest_ref, h_ref, xs_ref, zero_s, sem):
    tm = h_ref.shape[0]

    @pl.when(pl.program_id(0) == 0)
    def _():
        zero_s[...] = jnp.zeros_like(zero_s)
        tails = []
        for e in range(N_EXPERTS):
            pos = pad_ref[e]
            for piece in PAD_PIECES:
                hit = (len_ref[e] & piece) != 0
                tails.append((hit, pltpu.make_async_copy(
                    zero_s.at[pl.ds(0, piece), :],
                    xs_ref.at[pl.ds(pl.multiple_of(pos, SUBLANES), piece), :], sem)))
                pos = pos + jnp.where(hit, piece, 0)
        n_tiles = xs_ref.shape[0] // EXPERT_TILE
        for t in range(N_EXPERTS):
            tile = nu_ref[0] + t
            row0 = pl.multiple_of(jnp.minimum(tile, n_tiles - 1) * EXPERT_TILE, EXPERT_TILE)
            tails.append((tile < n_tiles, pltpu.make_async_copy(
                zero_s, xs_ref.at[pl.ds(row0, EXPERT_TILE), :], sem)))
        for hit, cp in tails:
            pl.when(hit)(cp.start)
        for hit, cp in tails:
            pl.when(hit)(cp.wait)

    def issue(g, _):
        for u in range(DMA_UNROLL):
            r = g * DMA_UNROLL + u
            _row_copy(h_ref, r, xs_ref, dest_ref[0, r], sem).start()
            _row_copy(h_ref, r, xs_ref, dest_ref[0, tm + r], sem).start()
        return 0

    lax.fori_loop(0, tm // DMA_UNROLL, issue, 0)

    def drain(g, _):
        for _u in range(2 * DMA_UNROLL):
            _row_copy(h_ref, 0, xs_ref, 0, sem).wait()
        return 0

    lax.fori_loop(0, tm // DMA_UNROLL, drain, 0)


def _dispatch(plan, h2):
    n, d = h2.shape
    dest3 = plan["dest3"]
    tm = dest3.shape[2] // 2
    rows = plan["tile_expert"].shape[0] * EXPERT_TILE
    grid_spec = pltpu.PrefetchScalarGridSpec(
        num_scalar_prefetch=3,
        grid=(n // tm,),
        in_specs=[pl.BlockSpec((None, 1, 2 * tm), lambda i, pad, ln, nu: (i, 0, 0), memory_space=pltpu.SMEM),
                  pl.BlockSpec((tm, d), lambda i, pad, ln, nu: (i, 0))],
        out_specs=pl.BlockSpec(memory_space=pl.ANY),
        scratch_shapes=[pltpu.VMEM((EXPERT_TILE, d), F32), pltpu.SemaphoreType.DMA],
    )
    return pl.pallas_call(
        _dispatch_kernel,
        out_shape=jax.ShapeDtypeStruct((rows, d), F32),
        grid_spec=grid_spec,
        compiler_params=_params(("arbitrary",), has_side_effects=True),
        name="dispatch",
    )(plan["pad_start"], plan["pad_len"], plan["n_used"], dest3, h2)


def _experts_kernel(te_ref, nu_ref, seg_ref, nxt_ref, xs_ref, wg_hbm, wu_hbm, wd_hbm, ys_ref,
                    wg_f, wu_f, wd_f, wg_s, wu_s, wd_s, sem):
    i = pl.program_id(0)

    def fetch(e, sl):
        return [pltpu.make_async_copy(src.at[e], dst.at[sl], sem.at[sl])
                for src, dst in ((wg_hbm, wg_f), (wu_hbm, wu_f), (wd_hbm, wd_f))]

    @pl.when(i < nu_ref[0])
    def _():
        prev = te_ref[jnp.maximum(i - 1, 0)]

        @pl.when(i == 0)
        def _():
            for cp in fetch(te_ref[0], 0):
                cp.start(priority=1)

        @pl.when((i == 0) | (te_ref[i] != prev))
        def _():
            sl = seg_ref[i] % 2
            for cp in fetch(te_ref[i], sl):
                cp.wait()

            @pl.when(nxt_ref[i] >= 0)
            def _():
                for cp in fetch(nxt_ref[i], 1 - sl):
                    cp.start(priority=1)

            wg_s[...] = wg_f[sl].astype(BF16)
            wu_s[...] = wu_f[sl].astype(BF16)
            wd_s[...] = wd_f[sl].astype(BF16)

        x = xs_ref[...].astype(BF16)
        a = jnp.dot(x, wg_s[...], preferred_element_type=F32)
        u = jnp.dot(x, wu_s[...], preferred_element_type=F32)
        hid = (_silu(a) * u).astype(BF16)
        ys_ref[...] = jnp.dot(hid, wd_s[...], preferred_element_type=F32)

    @pl.when(i >= nu_ref[0])
    def _():
        ys_ref[...] = jnp.zeros_like(ys_ref)


def _experts(plan, xs, w_gate, w_up, w_down):
    d = xs.shape[1]
    e, _, f = w_gate.shape
    tm = EXPERT_TILE
    n_tiles = plan["tile_expert"].shape[0]
    row = lambda i, te, nu, seg, nxt: (jnp.minimum(i, nu[0] - 1), 0)
    hbm = pl.BlockSpec(memory_space=pl.ANY)
    grid_spec = pltpu.PrefetchScalarGridSpec(
        num_scalar_prefetch=4,
        grid=(n_tiles,),
        in_specs=[pl.BlockSpec((tm, d), row), hbm, hbm, hbm],
        out_specs=pl.BlockSpec((tm, d), lambda i, te, nu, seg, nxt: (i, 0)),
        scratch_shapes=[pltpu.VMEM((2, d, f), F32), pltpu.VMEM((2, d, f), F32), pltpu.VMEM((2, f, d), F32),
                        pltpu.VMEM((d, f), BF16), pltpu.VMEM((d, f), BF16), pltpu.VMEM((f, d), BF16),
                        pltpu.SemaphoreType.DMA((2,))],
    )
    return pl.pallas_call(
        _experts_kernel,
        out_shape=jax.ShapeDtypeStruct((n_tiles * tm, d), F32),
        grid_spec=grid_spec,
        compiler_params=_params(("arbitrary",)),
        name="experts",
    )(plan["tile_expert"], plan["n_used"], plan["segment"], plan["next_expert"], xs, w_gate, w_up, w_down)


def _combine_kernel(dest_ref, dnext_ref, ys_ref, x1_ref, route_ref, g2_ref, fw_ref, o_ref, y0_s, y1_s, sem):
    tm = x1_ref.shape[0]
    i = pl.program_id(0)
    slot = i % 2

    def gather(idx_ref, sl):
        def issue(g, _):
            for u in range(DMA_UNROLL):
                r = g * DMA_UNROLL + u
                _row_copy(ys_ref, idx_ref[0, r], y0_s.at[sl], r, sem.at[sl]).start()
                _row_copy(ys_ref, idx_ref[0, tm + r], y1_s.at[sl], r, sem.at[sl]).start()
            return 0

        lax.fori_loop(0, tm // DMA_UNROLL, issue, 0)

    @pl.when(i == 0)
    def _():
        gather(dest_ref, 0)

    @pl.when(i + 1 < pl.num_programs(0))
    def _():
        gather(dnext_ref, 1 - slot)

    def drain(g, _):
        for _u in range(DMA_UNROLL):
            _row_copy(ys_ref, 0, y0_s.at[slot], 0, sem.at[slot]).wait()
            _row_copy(ys_ref, 0, y1_s.at[slot], 0, sem.at[slot]).wait()
        return 0

    lax.fori_loop(0, tm // DMA_UNROLL, drain, 0)
    route = route_ref[...]
    w0 = route[:, 2:3]
    w1 = route[:, 3:4]
    x = x1_ref[...] + g2_ref[...] * (w0 * y0_s[slot] + w1 * y1_s[slot])
    o_ref[...] = _rms(x, RMS_EPS) * fw_ref[...]


def _combine(dest3, ys, x1, route, mod4, final_w, seq):
    n, d = x1.shape
    tm = dest3.shape[2] // 2
    tpb = seq // tm
    row = lambda i: (i, 0)
    steps = n // tm
    return pl.pallas_call(
        _combine_kernel,
        out_shape=jax.ShapeDtypeStruct((n, d), F32),
        grid=(steps,),
        in_specs=[pl.BlockSpec((None, 1, 2 * tm), lambda i: (i, 0, 0), memory_space=pltpu.SMEM),
                  pl.BlockSpec((None, 1, 2 * tm), lambda i: (jnp.minimum(i + 1, steps - 1), 0, 0),
                               memory_space=pltpu.SMEM),
                  pl.BlockSpec(memory_space=pl.ANY),
                  pl.BlockSpec((tm, d), row), pl.BlockSpec((tm, LANES), row),
                  pl.BlockSpec((None, None, 1, d), lambda i: (i // tpb, 5, 0, 0)),
                  pl.BlockSpec((1, d), lambda i: (0, 0))],
        out_specs=pl.BlockSpec((tm, d), row),
        scratch_shapes=[pltpu.VMEM((2, tm, d), F32), pltpu.VMEM((2, tm, d), F32),
                        pltpu.SemaphoreType.DMA((2,))],
        compiler_params=_params(("arbitrary",)),
        name="combine",
    )(dest3, dest3, ys, x1, route, mod4, final_w.reshape(1, d))


PLAN_LANES = 2 * LANES


def _plan_kernel(cnt_ref, route_ref, dest_ref, plan_ref):
    cnt = cnt_ref[...]
    tiles = jnp.floor((cnt + (EXPERT_TILE - 1.0)) * (1.0 / EXPERT_TILE))
    sq = (LANES, LANES)
    row = lax.broadcasted_iota(jnp.int32, sq, 0)
    col = lax.broadcasted_iota(jnp.int32, sq, 1)
    tiles_b = jnp.broadcast_to(tiles, sq)
    tiles_col = jnp.sum(jnp.where(col == row, tiles_b, 0.0), axis=1, keepdims=True)
    end_col = jnp.sum(jnp.where(col <= row, tiles_b, 0.0), axis=1, keepdims=True)
    end_row = jnp.sum(jnp.where(row <= col, jnp.broadcast_to(tiles_col, sq), 0.0), axis=0, keepdims=True)
    start_row = end_row - tiles

    route_t = route_ref[...].T
    blk = route_t.shape[1]
    sub = lax.broadcasted_iota(jnp.int32, (LANES, blk), 0)
    start8 = jnp.broadcast_to(start_row, (SUBLANES, LANES)).astype(BF16)
    dest_ref[...] = jnp.zeros_like(dest_ref)
    for k in range(2):
        e_sub = route_t[k:k + 1, :].astype(jnp.int32) + ROUTER_LANE0
        onehot = jnp.where(sub == e_sub, 1.0, 0.0).astype(BF16)
        start = jnp.dot(start8, onehot, preferred_element_type=F32)[0:1, :]
        dest_ref[k:k + 1, :] = (route_t[4 + k:5 + k, :] + EXPERT_TILE * start).astype(jnp.int32)

    @pl.when(pl.program_id(0) == 0)
    def _():
        wide = (LANES, PLAN_LANES)
        t = lax.broadcasted_iota(jnp.int32, wide, 1).astype(F32)
        e_id = lax.broadcasted_iota(jnp.int32, wide, 0) - ROUTER_LANE0
        is_e = (e_id >= 0) & (e_id < N_EXPERTS)
        n_used = jnp.max(end_row, axis=1, keepdims=True)
        t = jnp.minimum(t, n_used - 1.0)
        end_w = jnp.broadcast_to(end_col, wide)
        has_w = is_e & (jnp.broadcast_to(tiles_col, wide) > 0.0)
        te = jnp.sum(jnp.where(is_e & (end_w <= t), 1, 0), axis=0, keepdims=True)
        seg = jnp.sum(jnp.where(has_w & (e_id < te), 1, 0), axis=0, keepdims=True)
        nxt = jnp.min(jnp.where(has_w & (e_id > te), e_id, N_EXPERTS), axis=0, keepdims=True)
        nxt = jnp.where(nxt == N_EXPERTS, -1, nxt)
        pad_start = jnp.floor((start_row * EXPERT_TILE + cnt) * (1.0 / SUBLANES)) * SUBLANES
        pad_len = end_row * EXPERT_TILE - pad_start
        zeros = jnp.zeros((1, PLAN_LANES - LANES), jnp.int32)
        plan_ref[...] = jnp.zeros_like(plan_ref)
        plan_ref[0:1, :] = te
        plan_ref[1:2, :] = seg
        plan_ref[2:3, :] = nxt
        plan_ref[3:4, :] = jnp.concatenate([pad_start.astype(jnp.int32), zeros], axis=1)
        plan_ref[4:5, :] = jnp.concatenate([pad_len.astype(jnp.int32), zeros], axis=1)
        plan_ref[5:6, :] = jnp.broadcast_to(n_used, (1, PLAN_LANES)).astype(jnp.int32)


def _plan(route, counts_row, seq):
    n = route.shape[0]
    tm = min(ROW_TILE, seq)
    n_tiles = (2 * n) // EXPERT_TILE + N_EXPERTS
    assert n_tiles <= PLAN_LANES
    blk = min(n, 2048)
    dest, plan = pl.pallas_call(
        _plan_kernel,
        out_shape=(jax.ShapeDtypeStruct((SUBLANES, n), jnp.int32),
                   jax.ShapeDtypeStruct((SUBLANES, PLAN_LANES), jnp.int32)),
        grid=(n // blk,),
        in_specs=[pl.BlockSpec((1, LANES), lambda i: (0, 0)), pl.BlockSpec((blk, LANES), lambda i: (i, 0))],
        out_specs=(pl.BlockSpec((SUBLANES, blk), lambda i: (0, i)),
                   pl.BlockSpec((SUBLANES, PLAN_LANES), lambda i: (0, 0))),
        compiler_params=_params(("arbitrary",)),
        name="plan",
    )(counts_row, route)
    dest3 = jnp.concatenate([dest[0].reshape(n // tm, 1, tm), dest[1].reshape(n // tm, 1, tm)], axis=2)
    experts = slice(ROUTER_LANE0, ROUTER_LANE0 + N_EXPERTS)
    return {"dest3": dest3, "tile_expert": plan[0, :n_tiles], "segment": plan[1, :n_tiles],
            "next_expert": plan[2, :n_tiles], "pad_start": plan[3, experts], "pad_len": plan[4, experts],
            "n_used": plan[5, :1]}


def kernel(x, c, positions, w_ada, b_ada, norm1_w, w_in, lambda_q1, lambda_k1, lambda_q2, lambda_k2, subln_w, hgrn_lb_logits, gnorm_w, w_out, norm2_w, w_group_router, b_group_router, w_expert_router, b_expert_router, w_gate, w_up, w_down, final_norm_w):
    b, s, d = x.shape
    n = b * s
    depth = w_ada.shape[0]
    assert depth == 1, "the combine kernel applies the final norm, so exactly one layer is supported"
    half = DIFF_HEAD_DIM // 2

    inv_freq = ROPE_THETA ** (-jnp.arange(half, dtype=F32) / half)
    pos = jnp.broadcast_to(positions.astype(F32).reshape(n, 1), (n, LANES))
    freq = jnp.tile(inv_freq, LANES // half).reshape(1, LANES)
    sign = jnp.tile(jnp.concatenate([-jnp.ones((half,), F32), jnp.ones((half,), F32)]),
                    LANES // DIFF_HEAD_DIM).reshape(1, LANES)

    xf = x.reshape(n, d)
    for l in range(depth):
        lambda_init = 0.8 - 0.6 * math.exp(-0.3 * l)
        mod4 = _mod(c, w_ada[l], b_ada[l]).reshape(b, 6, 1, d)
        q, k, v, hq, hf, hi, hg = _inproj(xf, mod4, norm1_w[l], w_in[l].astype(BF16), pos, freq, sign, s)
        to3 = lambda t: t.reshape(b, s, SEG)
        a = _attn(to3(q), to3(k), to3(v), lambda_q1[l], lambda_k1[l], lambda_q2[l], lambda_k2[l],
                  subln_w[l], lambda_init)
        r = _hgrn(to3(hq), to3(hf), to3(hi), to3(hg), hgrn_lb_logits, gnorm_w[l], l)

        w_router = jnp.zeros((d, LANES), F32)
        w_router = w_router.at[:, :N_GROUPS].set(w_group_router[l])
        w_router = w_router.at[:, ROUTER_LANE0:ROUTER_LANE0 + N_EXPERTS].set(w_expert_router[l])
        b_router = jnp.zeros((1, LANES), F32)
        b_router = b_router.at[0, :N_GROUPS].set(b_group_router[l])
        b_router = b_router.at[0, ROUTER_LANE0:ROUTER_LANE0 + N_EXPERTS].set(b_expert_router[l].reshape(-1))
        wr_hi = w_router.astype(BF16)
        wr_split = jnp.concatenate([wr_hi, (w_router - wr_hi.astype(F32)).astype(BF16)], axis=1)
        x1, h2, route, counts = _outproj(a.reshape(n, SEG), r.reshape(n, SEG), xf, mod4, norm2_w[l],
                                         w_out[l].astype(BF16), wr_split, b_router, s)

        plan = _plan(route, counts, s)
        xs = _dispatch(plan, h2)
        e_w = lambda w: w.reshape((N_EXPERTS,) + w.shape[2:])
        ys = _experts(plan, xs, e_w(w_gate[l]), e_w(w_up[l]), e_w(w_down[l]))
        xf = _combine(plan["dest3"], ys, x1, route, mod4, final_norm_w, s)
    return xf.reshape(b, s, d)
```

```python
import functools
import math

import jax
import jax.numpy as jnp
from jax import lax
from jax.experimental import pallas as pl
from jax.experimental.pallas import tpu as pltpu

F32 = jnp.float32
BF16 = jnp.bfloat16
HIGHEST = lax.Precision.HIGHEST

N_DIFF_HEADS = 4
DIFF_HEAD_DIM = 64
DIFF_V_DIM = 2 * DIFF_HEAD_DIM
N_HGRN_HEADS = 4
HGRN_EXPAND = 128
HGRN_V_DIM = 128
ROPE_THETA = 10000.0
N_GROUPS = 4
EXPERTS_PER_GROUP = 8
N_EXPERTS = N_GROUPS * EXPERTS_PER_GROUP
RMS_EPS = 1e-6
SUBLN_EPS = 1e-5
SEG = 512
LANES = 128
SUBLANES = 8
ROUTER_LANE0 = N_GROUPS

ROW_TILE = 512
MOVE_TILE = 512
OUTPROJ_TILE = 1024
OUTPROJ_CHAINS = 4
ATTN_TILE = 512
ATTN_KV_TILE = 256
ATTN_ONES_ROWS = 16
LOG2E = math.log2(math.e)
HGRN_CHUNK = 64
HGRN_SUB = 16
HGRN_UNROLL = 8
EXPERT_TILE = 256
EXPERT_STEP_TILES = 2
PAD_PIECES = (256, 128, 64, 32, 16, 8)
DMA_UNROLL = 16
VMEM_LIMIT = 48 * 1024 * 1024


def _params(sem, **kw):
    return pltpu.CompilerParams(dimension_semantics=sem, vmem_limit_bytes=VMEM_LIMIT, **kw)


def _silu(x):
    return x * jax.nn.sigmoid(x)


def _rms(x, eps):
    return x * lax.rsqrt(jnp.mean(x * x, axis=-1, keepdims=True) + eps)


def _mod_kernel(c_ref, w_ref, b_ref, o_ref):
    ca = _silu(c_ref[...])
    o_ref[...] = jnp.dot(ca, w_ref[...], preferred_element_type=F32, precision=HIGHEST) + b_ref[...]


def _mod(c, w_ada, b_ada):
    bsz, d = c.shape
    n_out = w_ada.shape[1]
    return pl.pallas_call(
        _mod_kernel,
        out_shape=jax.ShapeDtypeStruct((bsz, n_out), F32),
        grid=(n_out // d,),
        in_specs=[pl.BlockSpec((bsz, d), lambda j: (0, 0)),
                  pl.BlockSpec((d, d), lambda j: (0, j)),
                  pl.BlockSpec((1, d), lambda j: (0, j))],
        out_specs=pl.BlockSpec((bsz, d), lambda j: (0, j)),
        compiler_params=_params(("arbitrary",)),
        name="mod",
    )(c, w_ada, b_ada.reshape(1, n_out))


def _inproj_kernel(x_ref, sc_ref, sh_ref, nw_ref, w_ref, pos_ref, freq_ref, sign_ref,
                   q_ref, k_ref, v_ref, hq_ref, hf_ref, hi_ref, hg_ref):
    h = _rms(x_ref[...], RMS_EPS) * nw_ref[...]
    h = h * (1.0 + sc_ref[...]) + sh_ref[...]
    hb = h.astype(BF16)
    ang = pos_ref[...] * freq_ref[...]
    cos = jnp.cos(ang)
    sin = jnp.sin(ang) * sign_ref[...]
    lane = lax.broadcasted_iota(jnp.int32, cos.shape, 1)
    first_half = (lane % DIFF_HEAD_DIM) < (DIFF_HEAD_DIM // 2)

    def seg(i):
        return jnp.dot(hb, w_ref[:, i * SEG:(i + 1) * SEG], preferred_element_type=F32)

    def rope(p, scale):
        outs = []
        for c in range(SEG // LANES):
            pc = p[:, c * LANES:(c + 1) * LANES]
            partner = jnp.where(first_half, pltpu.roll(pc, LANES - DIFF_HEAD_DIM // 2, 1),
                                pltpu.roll(pc, DIFF_HEAD_DIM // 2, 1))
            outs.append((pc * cos + partner * sin) * scale)
        return jnp.concatenate(outs, axis=1)

    q_ref[...] = rope(seg(0), DIFF_HEAD_DIM ** -0.5 * LOG2E).astype(BF16)
    k_ref[...] = rope(seg(1), 1.0).astype(BF16)
    v_ref[...] = seg(2).astype(BF16)
    hq_ref[...] = seg(3).astype(BF16)
    hf_ref[...] = seg(4)
    hi_ref[...] = seg(5).astype(BF16)
    hg_ref[...] = seg(6).astype(BF16)


def _inproj(x2, mod4, norm_w, w_in_bf, pos, freq, sign, seq):
    n, d = x2.shape
    tm = min(ROW_TILE, seq)
    tpb = seq // tm
    row = lambda i: (i, 0)
    modspec = lambda j: pl.BlockSpec((None, None, 1, d), lambda i: (i // tpb, j, 0, 0))
    seg_spec = pl.BlockSpec((tm, SEG), row)
    bf = jax.ShapeDtypeStruct((n, SEG), BF16)
    return pl.pallas_call(
        _inproj_kernel,
        out_shape=(bf, bf, bf, bf, jax.ShapeDtypeStruct((n, SEG), F32), bf, bf),
        grid=(n // tm,),
        in_specs=[pl.BlockSpec((tm, d), row), modspec(1), modspec(0),
                  pl.BlockSpec((1, d), lambda i: (0, 0)),
                  pl.BlockSpec(w_in_bf.shape, lambda i: (0, 0)),
                  pl.BlockSpec((tm, LANES), row),
                  pl.BlockSpec((1, LANES), lambda i: (0, 0)), pl.BlockSpec((1, LANES), lambda i: (0, 0))],
        out_specs=(seg_spec,) * 7,
        compiler_params=_params(("arbitrary",)),
        name="inproj",
    )(x2, mod4, mod4, norm_w.reshape(1, d), w_in_bf, pos, freq, sign)


def _attn_kernel(q_ref, k_ref, v_ref, lq1_ref, lk1_ref, lq2_ref, lk2_ref, sw_ref, o_ref,
                 vt_s, acc_s, *, lambda_init):
    tq = q_ref.shape[0]
    seq = k_ref.shape[0]
    tk = min(ATTN_KV_TILE, tq)
    vd = DIFF_V_DIM
    qi = pl.program_id(1)
    nt = (((1,), (1,)), ((), ()))
    chains = [(h, mp) for h in range(N_DIFF_HEADS) for mp in range(2)]

    va = vd + ATTN_ONES_ROWS

    @pl.when(qi == 0)
    def _():
        for h in range(N_DIFF_HEADS):
            vt_s[h * va + vd:(h + 1) * va, :] = jnp.ones((ATTN_ONES_ROWS, seq), BF16)
            for c in range(seq // tk):
                vt_s[h * va:h * va + vd, c * tk:(c + 1) * tk] = (
                    v_ref[c * tk:(c + 1) * tk, h * vd:(h + 1) * vd].astype(F32).T.astype(BF16))

    acc_s[...] = jnp.zeros_like(acc_s)
    lane = lax.broadcasted_iota(jnp.int32, (tq, vd), 1)
    qz = []
    for h, mp in chains:
        qh = q_ref[:, h * vd:(h + 1) * vd]
        keep = (lane < DIFF_HEAD_DIM) if mp == 0 else (lane >= DIFF_HEAD_DIM)
        qz.append(jnp.where(keep, qh, jnp.zeros_like(qh)))

    def step(j, ms, c0):
        row0 = pl.multiple_of(j * tk, tk)
        q0 = c0 or 0
        scores = [lax.dot_general(k_ref[pl.ds(row0, tk), h * vd:(h + 1) * vd], qz[c][q0:], nt,
                                  preferred_element_type=F32) for c, (h, _) in enumerate(chains)]
        new_ms, alphas, probs = [], [], []
        for c, s in enumerate(scores):
            if c0 is not None:
                kidx = lax.broadcasted_iota(jnp.int32, s.shape, 0)
                qidx = lax.broadcasted_iota(jnp.int32, s.shape, 1)
                s = jnp.where(kidx <= qidx, s, -jnp.inf)
            m_old = ms[c][:, q0:]
            m_new = jnp.maximum(m_old, jnp.max(s, axis=0, keepdims=True))
            alphas.append(jnp.exp2(m_old - m_new))
            probs.append(jnp.exp2(s - m_new).astype(BF16))
            new_ms.append(m_new if q0 == 0 else jnp.concatenate([ms[c][:, :q0], m_new], axis=1))
        pvs = [jnp.dot(vt_s[h * va:(h + 1) * va, pl.ds(row0, tk)], probs[c], preferred_element_type=F32)
               for c, (h, _) in enumerate(chains)]
        for c, pv in enumerate(pvs):
            acc_s[c, :, q0:] = alphas[c] * acc_s[c, :, q0:] + pv
        return tuple(new_ms)

    init = tuple(jnp.full((1, tq), -jnp.inf, F32) for _ in chains)
    per_q = tq // tk
    ms = lax.fori_loop(0, qi * per_q, lambda j, c: step(j, c, None), init)
    for d in range(per_q):
        ms = step(qi * per_q + d, ms, d * tk)

    lam = (jnp.exp(jnp.sum(lq1_ref[...] * lk1_ref[...], axis=-1, keepdims=True))
           - jnp.exp(jnp.sum(lq2_ref[...] * lk2_ref[...], axis=-1, keepdims=True)) + lambda_init)
    sw = jnp.concatenate([sw_ref[...]] * (tq // LANES), axis=1)
    for h in range(N_DIFF_HEADS):
        inv1 = 1.0 / acc_s[2 * h, vd:vd + 1, :]
        inv2 = 1.0 / acc_s[2 * h + 1, vd:vd + 1, :]
        o = acc_s[2 * h, :vd, :] * inv1 - lam * (acc_s[2 * h + 1, :vd, :] * inv2)
        o = o * lax.rsqrt(jnp.mean(o * o, axis=0, keepdims=True) + SUBLN_EPS)
        o_ref[:, h * vd:(h + 1) * vd] = (o * sw * (1.0 - lambda_init)).T.astype(BF16)


def _attn(q, k, v, lq1, lk1, lq2, lk2, subln_w, lambda_init):
    b, s, width = q.shape
    tq = min(ATTN_TILE, s)
    vec = lambda a: a.reshape(1, -1)
    small = lambda w: pl.BlockSpec((1, w), lambda bi, i: (0, 0))
    kv_spec = pl.BlockSpec((None, s, width), lambda bi, i: (bi, 0, 0))
    q_spec = pl.BlockSpec((None, tq, width), lambda bi, i: (bi, i, 0))
    sw_cols = jnp.broadcast_to(subln_w.reshape(DIFF_V_DIM, 1), (DIFF_V_DIM, LANES))
    return pl.pallas_call(
        functools.partial(_attn_kernel, lambda_init=lambda_init),
        out_shape=jax.ShapeDtypeStruct(q.shape, BF16),
        grid=(b, s // tq),
        in_specs=[q_spec, kv_spec, kv_spec, small(DIFF_HEAD_DIM), small(DIFF_HEAD_DIM),
                  small(DIFF_HEAD_DIM), small(DIFF_HEAD_DIM),
                  pl.BlockSpec((DIFF_V_DIM, LANES), lambda bi, i: (0, 0))],
        out_specs=q_spec,
        scratch_shapes=[pltpu.VMEM((N_DIFF_HEADS * (DIFF_V_DIM + ATTN_ONES_ROWS), s), BF16),
                        pltpu.VMEM((2 * N_DIFF_HEADS, DIFF_V_DIM + ATTN_ONES_ROWS, tq), F32)],
        compiler_params=_params(("arbitrary",) * 2),
        name="attn",
    )(q, k, v, vec(lq1), vec(lk1), vec(lq2), vec(lk2), sw_cols)


def _hgrn_kernel(q_ref, f_ref, i_ref, g_ref, lbl_ref, gw_ref, o_ref, qd_s, oi_s, u_s, d_s, *, layer):
    seq = q_ref.shape[0]
    ck, sub = HGRN_CHUNK, HGRN_SUB
    lg = lbl_ref[...]
    e = jnp.exp(lg - jnp.max(lg, axis=0, keepdims=True))
    lb = jnp.sum(e[:layer + 1], axis=0, keepdims=True) / jnp.sum(e, axis=0, keepdims=True)
    gw = gw_ref[...]
    causal = (lax.broadcasted_iota(jnp.int32, (ck, ck), 1)
              <= lax.broadcasted_iota(jnp.int32, (ck, ck), 0))
    tril = causal.astype(BF16)
    nt = (((1,), (1,)), ((), ()))
    tn = (((0,), (0,)), ((), ()))
    w = HGRN_EXPAND
    nsub = ck // sub

    def phase1(c2, _):
        chunks = [c2 * HGRN_UNROLL + u for u in range(HGRN_UNROLL)]
        rows = [pl.ds(pl.multiple_of(c * ck, ck), ck) for c in chunks]
        kins, splits = [], []
        for r in rows:
            forget = lb + (1.0 - lb) * jax.nn.sigmoid(f_ref[r, :])
            kins.append(1.0 - forget)
            g = jnp.log2(forget)
            g_hi = g.astype(BF16)
            r1 = g - g_hi.astype(F32)
            g_mid = r1.astype(BF16)
            g_lo = (r1 - g_mid.astype(F32)).astype(BF16)
            splits.append(jnp.concatenate([g_hi, g_mid, g_lo], axis=1))
        sums = [jnp.dot(tril, x, preferred_element_type=F32) for x in splits]

        q_blk, k_blk, kdecs = [], [], []
        for c, r, kin, cs in zip(chunks, rows, kins, sums):
            bsum = cs[:, :w] + cs[:, w:2 * w] + cs[:, 2 * w:]
            qf = _silu(q_ref[r, :].astype(F32))
            qd_s[r, :] = (qf * jnp.exp2(bsum)).astype(BF16)
            q_rows, k_cols = [], []
            for i in range(nsub):
                lo, hi = i * sub, (i + 1) * sub
                ref_pt = bsum[lo:lo + 1, :]
                qh = (qf[lo:hi] * jnp.exp2(bsum[lo:hi] - ref_pt)).astype(BF16)
                kh = (kin[:hi] * jnp.exp2(ref_pt - bsum[:hi])).astype(BF16)
                pieces = [qh if j == i else jnp.zeros((sub, w), BF16) for j in range(nsub)]
                q_rows.append(jnp.concatenate(pieces, axis=1))
                k_cols.append(kh if hi == ck else jnp.concatenate([kh, jnp.zeros((ck - hi, w), BF16)], axis=0))
            q_blk.append(jnp.concatenate(q_rows, axis=0))
            k_blk.append(jnp.concatenate(k_cols, axis=1))
            last = bsum[ck - 1:ck, :]
            kdecs.append((kin * jnp.exp2(last - bsum)).astype(BF16))
            d_s[c] = jnp.exp2(last)
        vbs = [i_ref[r, :] for r in rows]
        atts = [lax.dot_general(qb, kb, nt, preferred_element_type=F32) for qb, kb in zip(q_blk, k_blk)]
        for c, vb, kdec in zip(chunks, vbs, kdecs):
            u_s[c] = lax.dot_general(vb, kdec, tn, preferred_element_type=F32)
        probs = [jnp.where(causal, att, 0.0).astype(BF16) for att in atts]
        for r, p, vb in zip(rows, probs, vbs):
            oi_s[r, :] = jnp.dot(p, vb, preferred_element_type=F32)
        return 0

    lax.fori_loop(0, seq // (ck * HGRN_UNROLL), phase1, 0)

    def phase2(c2, state_t):
        for u in range(HGRN_UNROLL):
            c = c2 * HGRN_UNROLL + u
            rows = pl.ds(pl.multiple_of(c * ck, ck), ck)
            o = oi_s[rows, :] + lax.dot_general(qd_s[rows, :], state_t.astype(BF16), nt,
                                                preferred_element_type=F32)
            o = _rms(o, RMS_EPS) * gw * _silu(g_ref[rows, :].astype(F32))
            o_ref[rows, :] = o.astype(BF16)
            state_t = state_t * d_s[c] + u_s[c]
        return state_t

    lax.fori_loop(0, seq // (ck * HGRN_UNROLL), phase2, jnp.zeros((HGRN_V_DIM, HGRN_EXPAND), F32))


def _hgrn(hq, hf, hi, hg, lb_logits, gnorm_w, layer):
    b, s, _ = hq.shape
    w = HGRN_EXPAND
    blk = pl.BlockSpec((None, s, w), lambda bi, h: (bi, 0, h))
    return pl.pallas_call(
        functools.partial(_hgrn_kernel, layer=layer),
        out_shape=jax.ShapeDtypeStruct(hq.shape, BF16),
        grid=(b, N_HGRN_HEADS),
        in_specs=[blk, blk, blk, blk,
                  pl.BlockSpec((lb_logits.shape[0], w), lambda bi, h: (0, h)),
                  pl.BlockSpec((1, w), lambda bi, h: (0, 0))],
        out_specs=blk,
        scratch_shapes=[pltpu.VMEM((s, w), BF16), pltpu.VMEM((s, HGRN_V_DIM), F32),
                        pltpu.VMEM((s // HGRN_CHUNK, HGRN_V_DIM, w), F32),
                        pltpu.VMEM((s // HGRN_CHUNK, 1, w), F32)],
        compiler_params=_params(("arbitrary",) * 2),
        name="hgrn",
    )(hq, hf, hi, hg, lb_logits, gnorm_w.reshape(1, w))


def _route(logits):
    lane = lax.broadcasted_iota(jnp.int32, logits.shape, 1)
    big = jnp.int32(LANES)
    neg = -jnp.inf
    is_g = lane < N_GROUPS
    gl = jnp.where(is_g, logits, neg)
    gmax = jnp.max(gl, axis=-1, keepdims=True)
    gsel = jnp.min(jnp.where(gl == gmax, lane, big), axis=-1, keepdims=True)
    pg = 1.0 / jnp.sum(jnp.where(is_g, jnp.exp(logits - gmax), 0.0), axis=-1, keepdims=True)
    lo = ROUTER_LANE0 + EXPERTS_PER_GROUP * gsel
    el = jnp.where((lane >= lo) & (lane < lo + EXPERTS_PER_GROUP), logits, neg)
    v0 = jnp.max(el, axis=-1, keepdims=True)
    i0 = jnp.min(jnp.where(el == v0, lane, big), axis=-1, keepdims=True)
    el = jnp.where(lane == i0, neg, el)
    v1 = jnp.max(el, axis=-1, keepdims=True)
    i1 = jnp.min(jnp.where(el == v1, lane, big), axis=-1, keepdims=True)
    t = jnp.exp(v1 - v0)
    return lane, i0, i1, pg / (1.0 + t), pg * t / (1.0 + t)


def _outproj_kernel(a_ref, r_ref, x_ref, g1_ref, sc_ref, sh_ref, nw_ref, wo_ref, wr_ref, br_ref,
                    x1_ref, h2_ref, route_ref, cnt_ref, run_ref):
    i = pl.program_id(0)
    sub = x_ref.shape[0] // OUTPROJ_CHAINS
    half = a_ref.shape[1]
    blocks = [pl.ds(u * sub, sub) for u in range(OUTPROJ_CHAINS)]

    @pl.when(i == 0)
    def _():
        run_ref[...] = jnp.zeros_like(run_ref)

    mixes = [jnp.dot(a_ref[rows, :], wo_ref[:half, :], preferred_element_type=F32)
             + jnp.dot(r_ref[rows, :], wo_ref[half:, :], preferred_element_type=F32) for rows in blocks]
    splits = []
    for rows, mix in zip(blocks, mixes):
        x1 = x_ref[rows, :] + g1_ref[...] * mix
        x1_ref[rows, :] = x1
        h2 = _rms(x1, RMS_EPS) * nw_ref[...]
        h2 = h2 * (1.0 + sc_ref[...]) + sh_ref[...]
        h2_ref[rows, :] = h2
        h_hi = h2.astype(BF16)
        splits.append((h_hi, (h2 - h_hi.astype(F32)).astype(BF16)))
    parts = [jnp.dot(h_hi, wr_ref[...], preferred_element_type=F32)
             + jnp.dot(h_lo, wr_ref[...], preferred_element_type=F32) for h_hi, h_lo in splits]
    routed = [_route(part[:, :LANES] + part[:, LANES:] + br_ref[...]) for part in parts]
    before = jnp.where(lax.broadcasted_iota(jnp.int32, (sub, sub), 1)
                       < lax.broadcasted_iota(jnp.int32, (sub, sub), 0), 1.0, 0.0).astype(BF16)
    onehots = [jnp.where((lane == i0) | (lane == i1), 1.0, 0.0) for lane, i0, i1, _, _ in routed]
    prefixes = [jnp.dot(before, oh.astype(BF16), preferred_element_type=F32) for oh in onehots]
    run = run_ref[...]
    for rows, (lane, i0, i1, w0, w1), onehot, prefix in zip(blocks, routed, onehots, prefixes):
        prefix = prefix + run
        rank0 = jnp.sum(jnp.where(lane == i0, prefix, 0.0), axis=-1, keepdims=True)
        rank1 = jnp.sum(jnp.where(lane == i1, prefix, 0.0), axis=-1, keepdims=True)
        run = run + jnp.sum(onehot, axis=0, keepdims=True)
        e0 = (i0 - ROUTER_LANE0).astype(F32)
        e1 = (i1 - ROUTER_LANE0).astype(F32)
        slab = jnp.zeros(lane.shape, F32)
        for ln, val in enumerate((e0, e1, w0, w1, rank0, rank1)):
            slab = jnp.where(lane == ln, val, slab)
        route_ref[rows, :] = slab
    run_ref[...] = run
    cnt_ref[...] = run


def _outproj(a2, r2, x2, mod4, norm_w, w_out_bf, w_router, b_router, seq):
    n, d = x2.shape
    tm = min(OUTPROJ_TILE, seq)
    tpb = seq // tm
    row = lambda i: (i, 0)
    fixed = lambda i: (0, 0)
    modspec = lambda j: pl.BlockSpec((None, None, 1, d), lambda i: (i // tpb, j, 0, 0))
    return pl.pallas_call(
        _outproj_kernel,
        out_shape=(jax.ShapeDtypeStruct((n, d), F32), jax.ShapeDtypeStruct((n, d), F32),
                   jax.ShapeDtypeStruct((n, LANES), F32), jax.ShapeDtypeStruct((1, LANES), F32)),
        grid=(n // tm,),
        in_specs=[pl.BlockSpec((tm, a2.shape[1]), row), pl.BlockSpec((tm, r2.shape[1]), row),
                  pl.BlockSpec((tm, d), row), modspec(2), modspec(4), modspec(3),
                  pl.BlockSpec((1, d), fixed), pl.BlockSpec(w_out_bf.shape, fixed),
                  pl.BlockSpec(w_router.shape, fixed), pl.BlockSpec((1, LANES), fixed)],
        out_specs=(pl.BlockSpec((tm, d), row), pl.BlockSpec((tm, d), row),
                   pl.BlockSpec((tm, LANES), row), pl.BlockSpec((1, LANES), fixed)),
        scratch_shapes=[pltpu.VMEM((1, LANES), F32)],
        compiler_params=_params(("arbitrary",)),
        name="outproj",
    )(a2, r2, x2, mod4, mod4, mod4, norm_w.reshape(1, d), w_out_bf, w_router, b_router)


def _row_copy(src_ref, src_row, dst_ref, dst_row, sem):
    return pltpu.make_async_copy(src_ref.at[pl.ds(src_row, 1), :], dst_ref.at[pl.ds(dst_row, 1), :], sem)


def _dispatch_kernel(pad_ref, len_ref, nu_ref, dest_ref, h_ref, xs_ref, zero_s, sem):
    tm = h_ref.shape[0]

    @pl.when(pl.program_id(0) == 0)
    def _():
        zero_s[...] = jnp.zeros_like(zero_s)
        tails = []
        for e in range(N_EXPERTS):
            pos = pad_ref[e]
            for piece in PAD_PIECES:
                hit = (len_ref[e] & piece) != 0
                tails.append((hit, pltpu.make_async_copy(
                    zero_s.at[pl.ds(0, piece), :],
                    xs_ref.at[pl.ds(pl.multiple_of(pos, SUBLANES), piece), :], sem)))
                pos = pos + jnp.where(hit, piece, 0)
        n_tiles = xs_ref.shape[0] // EXPERT_TILE
        for t in range(N_EXPERTS):
            tile = nu_ref[0] + t
            row0 = pl.multiple_of(jnp.minimum(tile, n_tiles - 1) * EXPERT_TILE, EXPERT_TILE)
            tails.append((tile < n_tiles, pltpu.make_async_copy(
                zero_s, xs_ref.at[pl.ds(row0, EXPERT_TILE), :], sem)))
        for hit, cp in tails:
            pl.when(hit)(cp.start)
        for hit, cp in tails:
            pl.when(hit)(cp.wait)

    def issue(g, _):
        for u in range(DMA_UNROLL):
            r = g * DMA_UNROLL + u
            _row_copy(h_ref, r, xs_ref, dest_ref[0, r], sem).start()
            _row_copy(h_ref, r, xs_ref, dest_ref[0, tm + r], sem).start()
        return 0

    lax.fori_loop(0, tm // DMA_UNROLL, issue, 0)

    def drain(g, _):
        for _u in range(2 * DMA_UNROLL):
            _row_copy(h_ref, 0, xs_ref, 0, sem).wait()
        return 0

    lax.fori_loop(0, tm // DMA_UNROLL, drain, 0)


def _dispatch(plan, h2):
    n, d = h2.shape
    dest3 = plan["dest3"]
    tm = dest3.shape[2] // 2
    rows = plan["tile_expert"].shape[0] * EXPERT_TILE
    grid_spec = pltpu.PrefetchScalarGridSpec(
        num_scalar_prefetch=3,
        grid=(n // tm,),
        in_specs=[pl.BlockSpec((None, 1, 2 * tm), lambda i, pad, ln, nu: (i, 0, 0), memory_space=pltpu.SMEM),
                  pl.BlockSpec((tm, d), lambda i, pad, ln, nu: (i, 0))],
        out_specs=pl.BlockSpec(memory_space=pl.ANY),
        scratch_shapes=[pltpu.VMEM((EXPERT_TILE, d), F32), pltpu.SemaphoreType.DMA],
    )
    return pl.pallas_call(
        _dispatch_kernel,
        out_shape=jax.ShapeDtypeStruct((rows, d), F32),
        grid_spec=grid_spec,
        compiler_params=_params(("arbitrary",), has_side_effects=True),
        name="dispatch",
    )(plan["pad_start"], plan["pad_len"], plan["n_used"], dest3, h2)


def _experts_kernel(te_ref, nu_ref, seg_ref, nxt_ref, xs_ref, wg_hbm, wu_hbm, wd_hbm, ys_ref,
                    wg_f, wu_f, wd_f, wg_s, wu_s, wd_s, sem):
    def fetch(e, sl):
        return [pltpu.make_async_copy(src.at[e], dst.at[sl], sem.at[sl])
                for src, dst in ((wg_hbm, wg_f), (wu_hbm, wu_f), (wd_hbm, wd_f))]

    def tile(t, rows):
        @pl.when(t < nu_ref[0])
        def _():
            prev = te_ref[jnp.maximum(t - 1, 0)]

            @pl.when(t == 0)
            def _():
                for cp in fetch(te_ref[0], 0):
                    cp.start(priority=1)

            @pl.when((t == 0) | (te_ref[t] != prev))
            def _():
                sl = seg_ref[t] % 2
                for cp in fetch(te_ref[t], sl):
                    cp.wait()

                @pl.when(nxt_ref[t] >= 0)
                def _():
                    for cp in fetch(nxt_ref[t], 1 - sl):
                        cp.start(priority=1)

                wg_s[...] = wg_f[sl].astype(BF16)
                wu_s[...] = wu_f[sl].astype(BF16)
                wd_s[...] = wd_f[sl].astype(BF16)

            x = xs_ref[rows, :].astype(BF16)
            a = jnp.dot(x, wg_s[...], preferred_element_type=F32)
            u = jnp.dot(x, wu_s[...], preferred_element_type=F32)
            hid = (_silu(a) * u).astype(BF16)
            ys_ref[rows, :] = jnp.dot(hid, wd_s[...], preferred_element_type=F32)

        @pl.when(t >= nu_ref[0])
        def _():
            ys_ref[rows, :] = jnp.zeros((EXPERT_TILE, ys_ref.shape[1]), F32)

    for k in range(EXPERT_STEP_TILES):
        tile(pl.program_id(0) * EXPERT_STEP_TILES + k, pl.ds(k * EXPERT_TILE, EXPERT_TILE))


def _experts(plan, xs, w_gate, w_up, w_down):
    d = xs.shape[1]
    e, _, f = w_gate.shape
    tm = EXPERT_TILE * EXPERT_STEP_TILES
    n_tiles = plan["tile_expert"].shape[0]
    assert n_tiles % EXPERT_STEP_TILES == 0
    row = lambda i, te, nu, seg, nxt: (jnp.minimum(i, (nu[0] - 1) // EXPERT_STEP_TILES), 0)
    hbm = pl.BlockSpec(memory_space=pl.ANY)
    grid_spec = pltpu.PrefetchScalarGridSpec(
        num_scalar_prefetch=4,
        grid=(n_tiles // EXPERT_STEP_TILES,),
        in_specs=[pl.BlockSpec((tm, d), row), hbm, hbm, hbm],
        out_specs=pl.BlockSpec((tm, d), lambda i, te, nu, seg, nxt: (i, 0)),
        scratch_shapes=[pltpu.VMEM((2, d, f), F32), pltpu.VMEM((2, d, f), F32), pltpu.VMEM((2, f, d), F32),
                        pltpu.VMEM((d, f), BF16), pltpu.VMEM((d, f), BF16), pltpu.VMEM((f, d), BF16),
                        pltpu.SemaphoreType.DMA((2,))],
    )
    return pl.pallas_call(
        _experts_kernel,
        out_shape=jax.ShapeDtypeStruct((n_tiles * EXPERT_TILE, d), F32),
        grid_spec=grid_spec,
        compiler_params=_params(("arbitrary",)),
        name="experts",
    )(plan["tile_expert"], plan["n_used"], plan["segment"], plan["next_expert"], xs, w_gate, w_up, w_down)


def _combine_kernel(dest_ref, dnext_ref, ys_ref, x1_ref, route_ref, g2_ref, fw_ref, o_ref, y0_s, y1_s, sem):
    tm = x1_ref.shape[0]
    i = pl.program_id(0)
    slot = i % 2

    def gather(idx_ref, sl):
        def issue(g, _):
            for u in range(DMA_UNROLL):
                r = g * DMA_UNROLL + u
                _row_copy(ys_ref, idx_ref[0, r], y0_s.at[sl], r, sem.at[sl]).start()
                _row_copy(ys_ref, idx_ref[0, tm + r], y1_s.at[sl], r, sem.at[sl]).start()
            return 0

        lax.fori_loop(0, tm // DMA_UNROLL, issue, 0)

    @pl.when(i == 0)
    def _():
        gather(dest_ref, 0)

    @pl.when(i + 1 < pl.num_programs(0))
    def _():
        gather(dnext_ref, 1 - slot)

    def drain(g, _):
        for _u in range(DMA_UNROLL):
            _row_copy(ys_ref, 0, y0_s.at[slot], 0, sem.at[slot]).wait()
            _row_copy(ys_ref, 0, y1_s.at[slot], 0, sem.at[slot]).wait()
        return 0

    lax.fori_loop(0, tm // DMA_UNROLL, drain, 0)
    route = route_ref[...]
    w0 = route[:, 2:3]
    w1 = route[:, 3:4]
    x = x1_ref[...] + g2_ref[...] * (w0 * y0_s[slot] + w1 * y1_s[slot])
    o_ref[...] = _rms(x, RMS_EPS) * fw_ref[...]


def _combine(dest3, ys, x1, route, mod4, final_w, seq):
    n, d = x1.shape
    tm = dest3.shape[2] // 2
    tpb = seq // tm
    row = lambda i: (i, 0)
    steps = n // tm
    return pl.pallas_call(
        _combine_kernel,
        out_shape=jax.ShapeDtypeStruct((n, d), F32),
        grid=(steps,),
        in_specs=[pl.BlockSpec((None, 1, 2 * tm), lambda i: (i, 0, 0), memory_space=pltpu.SMEM),
                  pl.BlockSpec((None, 1, 2 * tm), lambda i: (jnp.minimum(i + 1, steps - 1), 0, 0),
                               memory_space=pltpu.SMEM),
                  pl.BlockSpec(memory_space=pl.ANY),
                  pl.BlockSpec((tm, d), row), pl.BlockSpec((tm, LANES), row),
                  pl.BlockSpec((None, None, 1, d), lambda i: (i // tpb, 5, 0, 0)),
                  pl.BlockSpec((1, d), lambda i: (0, 0))],
        out_specs=pl.BlockSpec((tm, d), row),
        scratch_shapes=[pltpu.VMEM((2, tm, d), F32), pltpu.VMEM((2, tm, d), F32),
                        pltpu.SemaphoreType.DMA((2,))],
        compiler_params=_params(("arbitrary",)),
        name="combine",
    )(dest3, dest3, ys, x1, route, mod4, final_w.reshape(1, d))


PLAN_LANES = 2 * LANES


def _plan_kernel(cnt_ref, route_ref, dest_ref, plan_ref):
    cnt = cnt_ref[...]
    tiles = jnp.floor((cnt + (EXPERT_TILE - 1.0)) * (1.0 / EXPERT_TILE))
    sq = (LANES, LANES)
    row = lax.broadcasted_iota(jnp.int32, sq, 0)
    col = lax.broadcasted_iota(jnp.int32, sq, 1)
    tiles_b = jnp.broadcast_to(tiles, sq)
    tiles_col = jnp.sum(jnp.where(col == row, tiles_b, 0.0), axis=1, keepdims=True)
    end_col = jnp.sum(jnp.where(col <= row, tiles_b, 0.0), axis=1, keepdims=True)
    end_row = jnp.sum(jnp.where(row <= col, jnp.broadcast_to(tiles_col, sq), 0.0), axis=0, keepdims=True)
    start_row = end_row - tiles

    route_t = route_ref[...].T
    blk = route_t.shape[1]
    sub = lax.broadcasted_iota(jnp.int32, (LANES, blk), 0)
    start8 = jnp.broadcast_to(start_row, (SUBLANES, LANES)).astype(BF16)
    dest_ref[...] = jnp.zeros_like(dest_ref)
    for k in range(2):
        e_sub = route_t[k:k + 1, :].astype(jnp.int32) + ROUTER_LANE0
        onehot = jnp.where(sub == e_sub, 1.0, 0.0).astype(BF16)
        start = jnp.dot(start8, onehot, preferred_element_type=F32)[0:1, :]
        dest_ref[k:k + 1, :] = (route_t[4 + k:5 + k, :] + EXPERT_TILE * start).astype(jnp.int32)

    @pl.when(pl.program_id(0) == 0)
    def _():
        wide = (LANES, PLAN_LANES)
        t = lax.broadcasted_iota(jnp.int32, wide, 1).astype(F32)
        e_id = lax.broadcasted_iota(jnp.int32, wide, 0) - ROUTER_LANE0
        is_e = (e_id >= 0) & (e_id < N_EXPERTS)
        n_used = jnp.max(end_row, axis=1, keepdims=True)
        t = jnp.minimum(t, n_used - 1.0)
        end_w = jnp.broadcast_to(end_col, wide)
        has_w = is_e & (jnp.broadcast_to(tiles_col, wide) > 0.0)
        te = jnp.sum(jnp.where(is_e & (end_w <= t), 1, 0), axis=0, keepdims=True)
        seg = jnp.sum(jnp.where(has_w & (e_id < te), 1, 0), axis=0, keepdims=True)
        nxt = jnp.min(jnp.where(has_w & (e_id > te), e_id, N_EXPERTS), axis=0, keepdims=True)
        nxt = jnp.where(nxt == N_EXPERTS, -1, nxt)
        pad_start = jnp.floor((start_row * EXPERT_TILE + cnt) * (1.0 / SUBLANES)) * SUBLANES
        pad_len = end_row * EXPERT_TILE - pad_start
        zeros = jnp.zeros((1, PLAN_LANES - LANES), jnp.int32)
        plan_ref[...] = jnp.zeros_like(plan_ref)
        plan_ref[0:1, :] = te
        plan_ref[1:2, :] = seg
        plan_ref[2:3, :] = nxt
        plan_ref[3:4, :] = jnp.concatenate([pad_start.astype(jnp.int32), zeros], axis=1)
        plan_ref[4:5, :] = jnp.concatenate([pad_len.astype(jnp.int32), zeros], axis=1)
        plan_ref[5:6, :] = jnp.broadcast_to(n_used, (1, PLAN_LANES)).astype(jnp.int32)


def _plan(route, counts_row, seq):
    n = route.shape[0]
    tm = min(MOVE_TILE, seq)
    n_tiles = (2 * n) // EXPERT_TILE + N_EXPERTS
    assert n_tiles <= PLAN_LANES
    blk = min(n, 2048)
    dest, plan = pl.pallas_call(
        _plan_kernel,
        out_shape=(jax.ShapeDtypeStruct((SUBLANES, n), jnp.int32),
                   jax.ShapeDtypeStruct((SUBLANES, PLAN_LANES), jnp.int32)),
        grid=(n // blk,),
        in_specs=[pl.BlockSpec((1, LANES), lambda i: (0, 0)), pl.BlockSpec((blk, LANES), lambda i: (i, 0))],
        out_specs=(pl.BlockSpec((SUBLANES, blk), lambda i: (0, i)),
                   pl.BlockSpec((SUBLANES, PLAN_LANES), lambda i: (0, 0))),
        compiler_params=_params(("arbitrary",)),
        name="plan",
    )(counts_row, route)
    dest3 = jnp.concatenate([dest[0].reshape(n // tm, 1, tm), dest[1].reshape(n // tm, 1, tm)], axis=2)
    experts = slice(ROUTER_LANE0, ROUTER_LANE0 + N_EXPERTS)
    return {"dest3": dest3, "tile_expert": plan[0, :n_tiles], "segment": plan[1, :n_tiles],
            "next_expert": plan[2, :n_tiles], "pad_start": plan[3, experts], "pad_len": plan[4, experts],
            "n_used": plan[5, :1]}


def kernel(x, c, positions, w_ada, b_ada, norm1_w, w_in, lambda_q1, lambda_k1, lambda_q2, lambda_k2, subln_w, hgrn_lb_logits, gnorm_w, w_out, norm2_w, w_group_router, b_group_router, w_expert_router, b_expert_router, w_gate, w_up, w_down, final_norm_w):
    b, s, d = x.shape
    n = b * s
    depth = w_ada.shape[0]
    assert depth == 1, "the combine kernel applies the final norm, so exactly one layer is supported"
    half = DIFF_HEAD_DIM // 2

    inv_freq = ROPE_THETA ** (-jnp.arange(half, dtype=F32) / half)
    pos = jnp.broadcast_to(positions.astype(F32).reshape(n, 1), (n, LANES))
    freq = jnp.tile(inv_freq, LANES // half).reshape(1, LANES)
    sign = jnp.tile(jnp.concatenate([-jnp.ones((half,), F32), jnp.ones((half,), F32)]),
                    LANES // DIFF_HEAD_DIM).reshape(1, LANES)

    xf = x.reshape(n, d)
    for l in range(depth):
        lambda_init = 0.8 - 0.6 * math.exp(-0.3 * l)
        mod4 = _mod(c, w_ada[l], b_ada[l]).reshape(b, 6, 1, d)
        q, k, v, hq, hf, hi, hg = _inproj(xf, mod4, norm1_w[l], w_in[l].astype(BF16), pos, freq, sign, s)
        to3 = lambda t: t.reshape(b, s, SEG)
        a = _attn(to3(q), to3(k), to3(v), lambda_q1[l], lambda_k1[l], lambda_q2[l], lambda_k2[l],
                  subln_w[l], lambda_init)
        r = _hgrn(to3(hq), to3(hf), to3(hi), to3(hg), hgrn_lb_logits, gnorm_w[l], l)

        w_router = jnp.zeros((d, LANES), F32)
        w_router = w_router.at[:, :N_GROUPS].set(w_group_router[l])
        w_router = w_router.at[:, ROUTER_LANE0:ROUTER_LANE0 + N_EXPERTS].set(w_expert_router[l])
        b_router = jnp.zeros((1, LANES), F32)
        b_router = b_router.at[0, :N_GROUPS].set(b_group_router[l])
        b_router = b_router.at[0, ROUTER_LANE0:ROUTER_LANE0 + N_EXPERTS].set(b_expert_router[l].reshape(-1))
        wr_hi = w_router.astype(BF16)
        wr_split = jnp.concatenate([wr_hi, (w_router - wr_hi.astype(F32)).astype(BF16)], axis=1)
        x1, h2, route, counts = _outproj(a.reshape(n, SEG), r.reshape(n, SEG), xf, mod4, norm2_w[l],
                                         w_out[l].astype(BF16), wr_split, b_router, s)

        plan = _plan(route, counts, s)
        xs = _dispatch(plan, h2)
        e_w = lambda w: w.reshape((N_EXPERTS,) + w.shape[2:])
        ys = _experts(plan, xs, e_w(w_gate[l]), e_w(w_up[l]), e_w(w_down[l]))
        xf = _combine(plan["dest3"], ys, x1, route, mod4, final_norm_w, s)
    return xf.reshape(b, s, d)
```

```python
import functools
import math

import jax
import jax.numpy as jnp
from jax import lax
from jax.experimental import pallas as pl
from jax.experimental.pallas import tpu as pltpu

F32 = jnp.float32
BF16 = jnp.bfloat16
HIGHEST = lax.Precision.HIGHEST

N_DIFF_HEADS = 4
DIFF_HEAD_DIM = 64
DIFF_V_DIM = 2 * DIFF_HEAD_DIM
N_HGRN_HEADS = 4
HGRN_EXPAND = 128
HGRN_V_DIM = 128
ROPE_THETA = 10000.0
N_GROUPS = 4
EXPERTS_PER_GROUP = 8
N_EXPERTS = N_GROUPS * EXPERTS_PER_GROUP
RMS_EPS = 1e-6
SUBLN_EPS = 1e-5
SEG = 512
LANES = 128
SUBLANES = 8
ROUTER_LANE0 = N_GROUPS

ROW_TILE = 1024
MOVE_TILE = 1024
OUTPROJ_TILE = 1024
OUTPROJ_CHAINS = 4
ATTN_TILE = 512
ATTN_KV_TILE = 256
ATTN_ONES_ROWS = 16
LOG2E = math.log2(math.e)
HGRN_CHUNK = 64
HGRN_SUB = 16
HGRN_UNROLL = 8
EXPERT_TILE = 256
EXPERT_STEP_TILES = 4
PAD_PIECES = (256, 128, 64, 32, 16, 8)
DMA_UNROLL = 16
VMEM_LIMIT = 48 * 1024 * 1024


def _params(sem, **kw):
    return pltpu.CompilerParams(dimension_semantics=sem, vmem_limit_bytes=VMEM_LIMIT, **kw)


def _silu(x):
    return x * jax.nn.sigmoid(x)


def _rms(x, eps):
    return x * lax.rsqrt(jnp.mean(x * x, axis=-1, keepdims=True) + eps)


def _mod_kernel(c_ref, w_ref, b_ref, o_ref):
    ca = _silu(c_ref[...])
    o_ref[...] = jnp.dot(ca, w_ref[...], preferred_element_type=F32, precision=HIGHEST) + b_ref[...]


def _mod(c, w_ada, b_ada):
    bsz, d = c.shape
    n_out = w_ada.shape[1]
    return pl.pallas_call(
        _mod_kernel,
        out_shape=jax.ShapeDtypeStruct((bsz, n_out), F32),
        grid=(n_out // d,),
        in_specs=[pl.BlockSpec((bsz, d), lambda j: (0, 0)),
                  pl.BlockSpec((d, d), lambda j: (0, j)),
                  pl.BlockSpec((1, d), lambda j: (0, j))],
        out_specs=pl.BlockSpec((bsz, d), lambda j: (0, j)),
        compiler_params=_params(("arbitrary",)),
        name="mod",
    )(c, w_ada, b_ada.reshape(1, n_out))


def _inproj_kernel(x_ref, sc_ref, sh_ref, nw_ref, w_ref, pos_ref, freq_ref, sign_ref,
                   q_ref, k_ref, v_ref, hq_ref, hf_ref, hi_ref, hg_ref):
    h = _rms(x_ref[...], RMS_EPS) * nw_ref[...]
    h = h * (1.0 + sc_ref[...]) + sh_ref[...]
    hb = h.astype(BF16)
    ang = pos_ref[...] * freq_ref[...]
    cos = jnp.cos(ang)
    sin = jnp.sin(ang) * sign_ref[...]
    lane = lax.broadcasted_iota(jnp.int32, cos.shape, 1)
    first_half = (lane % DIFF_HEAD_DIM) < (DIFF_HEAD_DIM // 2)

    def seg(i):
        return jnp.dot(hb, w_ref[:, i * SEG:(i + 1) * SEG], preferred_element_type=F32)

    def rope(p, scale):
        outs = []
        for c in range(SEG // LANES):
            pc = p[:, c * LANES:(c + 1) * LANES]
            partner = jnp.where(first_half, pltpu.roll(pc, LANES - DIFF_HEAD_DIM // 2, 1),
                                pltpu.roll(pc, DIFF_HEAD_DIM // 2, 1))
            outs.append((pc * cos + partner * sin) * scale)
        return jnp.concatenate(outs, axis=1)

    q_ref[...] = rope(seg(0), DIFF_HEAD_DIM ** -0.5 * LOG2E).astype(BF16)
    k_ref[...] = rope(seg(1), 1.0).astype(BF16)
    v_ref[...] = seg(2).astype(BF16)
    hq_ref[...] = seg(3).astype(BF16)
    hf_ref[...] = seg(4)
    hi_ref[...] = seg(5).astype(BF16)
    hg_ref[...] = seg(6).astype(BF16)


def _inproj(x2, mod4, norm_w, w_in_bf, pos, freq, sign, seq):
    n, d = x2.shape
    tm = min(ROW_TILE, seq)
    tpb = seq // tm
    row = lambda i: (i, 0)
    modspec = lambda j: pl.BlockSpec((None, None, 1, d), lambda i: (i // tpb, j, 0, 0))
    seg_spec = pl.BlockSpec((tm, SEG), row)
    bf = jax.ShapeDtypeStruct((n, SEG), BF16)
    return pl.pallas_call(
        _inproj_kernel,
        out_shape=(bf, bf, bf, bf, jax.ShapeDtypeStruct((n, SEG), F32), bf, bf),
        grid=(n // tm,),
        in_specs=[pl.BlockSpec((tm, d), row), modspec(1), modspec(0),
                  pl.BlockSpec((1, d), lambda i: (0, 0)),
                  pl.BlockSpec(w_in_bf.shape, lambda i: (0, 0)),
                  pl.BlockSpec((tm, LANES), row),
                  pl.BlockSpec((1, LANES), lambda i: (0, 0)), pl.BlockSpec((1, LANES), lambda i: (0, 0))],
        out_specs=(seg_spec,) * 7,
        compiler_params=_params(("arbitrary",)),
        name="inproj",
    )(x2, mod4, mod4, norm_w.reshape(1, d), w_in_bf, pos, freq, sign)


def _attn_kernel(q_ref, k_ref, v_ref, lq1_ref, lk1_ref, lq2_ref, lk2_ref, sw_ref, o_ref,
                 vt_s, acc_s, *, lambda_init):
    tq = q_ref.shape[0]
    seq = k_ref.shape[0]
    tk = min(ATTN_KV_TILE, tq)
    vd = DIFF_V_DIM
    qi = pl.program_id(1)
    nt = (((1,), (1,)), ((), ()))
    chains = [(h, mp) for h in range(N_DIFF_HEADS) for mp in range(2)]

    va = vd + ATTN_ONES_ROWS

    @pl.when(qi == 0)
    def _():
        for h in range(N_DIFF_HEADS):
            vt_s[h * va + vd:(h + 1) * va, :] = jnp.ones((ATTN_ONES_ROWS, seq), BF16)
            for c in range(seq // tk):
                vt_s[h * va:h * va + vd, c * tk:(c + 1) * tk] = (
                    v_ref[c * tk:(c + 1) * tk, h * vd:(h + 1) * vd].astype(F32).T.astype(BF16))

    acc_s[...] = jnp.zeros_like(acc_s)
    lane = lax.broadcasted_iota(jnp.int32, (tq, vd), 1)
    qz = []
    for h, mp in chains:
        qh = q_ref[:, h * vd:(h + 1) * vd]
        keep = (lane < DIFF_HEAD_DIM) if mp == 0 else (lane >= DIFF_HEAD_DIM)
        qz.append(jnp.where(keep, qh, jnp.zeros_like(qh)))

    def step(j, ms, c0):
        row0 = pl.multiple_of(j * tk, tk)
        q0 = c0 or 0
        scores = [lax.dot_general(k_ref[pl.ds(row0, tk), h * vd:(h + 1) * vd], qz[c][q0:], nt,
                                  preferred_element_type=F32) for c, (h, _) in enumerate(chains)]
        new_ms, alphas, probs = [], [], []
        for c, s in enumerate(scores):
            if c0 is not None:
                kidx = lax.broadcasted_iota(jnp.int32, s.shape, 0)
                qidx = lax.broadcasted_iota(jnp.int32, s.shape, 1)
                s = jnp.where(kidx <= qidx, s, -jnp.inf)
            m_old = ms[c][:, q0:]
            m_new = jnp.maximum(m_old, jnp.max(s, axis=0, keepdims=True))
            alphas.append(jnp.exp2(m_old - m_new))
            probs.append(jnp.exp2(s - m_new).astype(BF16))
            new_ms.append(m_new if q0 == 0 else jnp.concatenate([ms[c][:, :q0], m_new], axis=1))
        pvs = [jnp.dot(vt_s[h * va:(h + 1) * va, pl.ds(row0, tk)], probs[c], preferred_element_type=F32)
               for c, (h, _) in enumerate(chains)]
        for c, pv in enumerate(pvs):
            acc_s[c, :, q0:] = alphas[c] * acc_s[c, :, q0:] + pv
        return tuple(new_ms)

    init = tuple(jnp.full((1, tq), -jnp.inf, F32) for _ in chains)
    per_q = tq // tk
    ms = lax.fori_loop(0, qi * per_q, lambda j, c: step(j, c, None), init)
    for d in range(per_q):
        ms = step(qi * per_q + d, ms, d * tk)

    lam = (jnp.exp(jnp.sum(lq1_ref[...] * lk1_ref[...], axis=-1, keepdims=True))
           - jnp.exp(jnp.sum(lq2_ref[...] * lk2_ref[...], axis=-1, keepdims=True)) + lambda_init)
    sw = jnp.concatenate([sw_ref[...]] * (tq // LANES), axis=1)
    for h in range(N_DIFF_HEADS):
        inv1 = 1.0 / acc_s[2 * h, vd:vd + 1, :]
        inv2 = 1.0 / acc_s[2 * h + 1, vd:vd + 1, :]
        o = acc_s[2 * h, :vd, :] * inv1 - lam * (acc_s[2 * h + 1, :vd, :] * inv2)
        o = o * lax.rsqrt(jnp.mean(o * o, axis=0, keepdims=True) + SUBLN_EPS)
        o_ref[:, h * vd:(h + 1) * vd] = (o * sw * (1.0 - lambda_init)).T.astype(BF16)


def _attn(q, k, v, lq1, lk1, lq2, lk2, subln_w, lambda_init):
    b, s, width = q.shape
    tq = min(ATTN_TILE, s)
    vec = lambda a: a.reshape(1, -1)
    small = lambda w: pl.BlockSpec((1, w), lambda bi, i: (0, 0))
    kv_spec = pl.BlockSpec((None, s, width), lambda bi, i: (bi, 0, 0))
    q_spec = pl.BlockSpec((None, tq, width), lambda bi, i: (bi, i, 0))
    sw_cols = jnp.broadcast_to(subln_w.reshape(DIFF_V_DIM, 1), (DIFF_V_DIM, LANES))
    return pl.pallas_call(
        functools.partial(_attn_kernel, lambda_init=lambda_init),
        out_shape=jax.ShapeDtypeStruct(q.shape, BF16),
        grid=(b, s // tq),
        in_specs=[q_spec, kv_spec, kv_spec, small(DIFF_HEAD_DIM), small(DIFF_HEAD_DIM),
                  small(DIFF_HEAD_DIM), small(DIFF_HEAD_DIM),
                  pl.BlockSpec((DIFF_V_DIM, LANES), lambda bi, i: (0, 0))],
        out_specs=q_spec,
        scratch_shapes=[pltpu.VMEM((N_DIFF_HEADS * (DIFF_V_DIM + ATTN_ONES_ROWS), s), BF16),
                        pltpu.VMEM((2 * N_DIFF_HEADS, DIFF_V_DIM + ATTN_ONES_ROWS, tq), F32)],
        compiler_params=_params(("arbitrary",) * 2),
        name="attn",
    )(q, k, v, vec(lq1), vec(lk1), vec(lq2), vec(lk2), sw_cols)


def _hgrn_kernel(q_ref, f_ref, i_ref, g_ref, lbl_ref, gw_ref, o_ref, qd_s, oi_s, u_s, d_s, *, layer):
    seq = q_ref.shape[0]
    ck, sub = HGRN_CHUNK, HGRN_SUB
    lg = lbl_ref[...]
    e = jnp.exp(lg - jnp.max(lg, axis=0, keepdims=True))
    lb = jnp.sum(e[:layer + 1], axis=0, keepdims=True) / jnp.sum(e, axis=0, keepdims=True)
    gw = gw_ref[...]
    causal = (lax.broadcasted_iota(jnp.int32, (ck, ck), 1)
              <= lax.broadcasted_iota(jnp.int32, (ck, ck), 0))
    tril = causal.astype(BF16)
    nt = (((1,), (1,)), ((), ()))
    tn = (((0,), (0,)), ((), ()))
    w = HGRN_EXPAND
    nsub = ck // sub

    def phase1(c2, _):
        chunks = [c2 * HGRN_UNROLL + u for u in range(HGRN_UNROLL)]
        rows = [pl.ds(pl.multiple_of(c * ck, ck), ck) for c in chunks]
        kins, splits = [], []
        for r in rows:
            forget = lb + (1.0 - lb) * jax.nn.sigmoid(f_ref[r, :])
            kins.append(1.0 - forget)
            g = jnp.log2(forget)
            g_hi = g.astype(BF16)
            r1 = g - g_hi.astype(F32)
            g_mid = r1.astype(BF16)
            g_lo = (r1 - g_mid.astype(F32)).astype(BF16)
            splits.append(jnp.concatenate([g_hi, g_mid, g_lo], axis=1))
        sums = [jnp.dot(tril, x, preferred_element_type=F32) for x in splits]

        q_blk, k_blk, kdecs = [], [], []
        for c, r, kin, cs in zip(chunks, rows, kins, sums):
            bsum = cs[:, :w] + cs[:, w:2 * w] + cs[:, 2 * w:]
            qf = _silu(q_ref[r, :].astype(F32))
            qd_s[r, :] = (qf * jnp.exp2(bsum)).astype(BF16)
            q_rows, k_cols = [], []
            for i in range(nsub):
                lo, hi = i * sub, (i + 1) * sub
                ref_pt = bsum[lo:lo + 1, :]
                qh = (qf[lo:hi] * jnp.exp2(bsum[lo:hi] - ref_pt)).astype(BF16)
                kh = (kin[:hi] * jnp.exp2(ref_pt - bsum[:hi])).astype(BF16)
                pieces = [qh if j == i else jnp.zeros((sub, w), BF16) for j in range(nsub)]
                q_rows.append(jnp.concatenate(pieces, axis=1))
                k_cols.append(kh if hi == ck else jnp.concatenate([kh, jnp.zeros((ck - hi, w), BF16)], axis=0))
            q_blk.append(jnp.concatenate(q_rows, axis=0))
            k_blk.append(jnp.concatenate(k_cols, axis=1))
            last = bsum[ck - 1:ck, :]
            kdecs.append((kin * jnp.exp2(last - bsum)).astype(BF16))
            d_s[c] = jnp.exp2(last)
        vbs = [i_ref[r, :] for r in rows]
        atts = [lax.dot_general(qb, kb, nt, preferred_element_type=F32) for qb, kb in zip(q_blk, k_blk)]
        for c, vb, kdec in zip(chunks, vbs, kdecs):
            u_s[c] = lax.dot_general(vb, kdec, tn, preferred_element_type=F32)
        probs = [jnp.where(causal, att, 0.0).astype(BF16) for att in atts]
        for r, p, vb in zip(rows, probs, vbs):
            oi_s[r, :] = jnp.dot(p, vb, preferred_element_type=F32)
        return 0

    lax.fori_loop(0, seq // (ck * HGRN_UNROLL), phase1, 0)

    def phase2(c2, state_t):
        for u in range(HGRN_UNROLL):
            c = c2 * HGRN_UNROLL + u
            rows = pl.ds(pl.multiple_of(c * ck, ck), ck)
            o = oi_s[rows, :] + lax.dot_general(qd_s[rows, :], state_t.astype(BF16), nt,
                                                preferred_element_type=F32)
            o = _rms(o, RMS_EPS) * gw * _silu(g_ref[rows, :].astype(F32))
            o_ref[rows, :] = o.astype(BF16)
            state_t = state_t * d_s[c] + u_s[c]
        return state_t

    lax.fori_loop(0, seq // (ck * HGRN_UNROLL), phase2, jnp.zeros((HGRN_V_DIM, HGRN_EXPAND), F32))


def _hgrn(hq, hf, hi, hg, lb_logits, gnorm_w, layer):
    b, s, _ = hq.shape
    w = HGRN_EXPAND
    blk = pl.BlockSpec((None, s, w), lambda bi, h: (bi, 0, h))
    return pl.pallas_call(
        functools.partial(_hgrn_kernel, layer=layer),
        out_shape=jax.ShapeDtypeStruct(hq.shape, BF16),
        grid=(b, N_HGRN_HEADS),
        in_specs=[blk, blk, blk, blk,
                  pl.BlockSpec((lb_logits.shape[0], w), lambda bi, h: (0, h)),
                  pl.BlockSpec((1, w), lambda bi, h: (0, 0))],
        out_specs=blk,
        scratch_shapes=[pltpu.VMEM((s, w), BF16), pltpu.VMEM((s, HGRN_V_DIM), F32),
                        pltpu.VMEM((s // HGRN_CHUNK, HGRN_V_DIM, w), F32),
                        pltpu.VMEM((s // HGRN_CHUNK, 1, w), F32)],
        compiler_params=_params(("arbitrary",) * 2),
        name="hgrn",
    )(hq, hf, hi, hg, lb_logits, gnorm_w.reshape(1, w))


def _route(logits):
    lane = lax.broadcasted_iota(jnp.int32, logits.shape, 1)
    big = jnp.int32(LANES)
    neg = -jnp.inf
    is_g = lane < N_GROUPS
    gl = jnp.where(is_g, logits, neg)
    gmax = jnp.max(gl, axis=-1, keepdims=True)
    gsel = jnp.min(jnp.where(gl == gmax, lane, big), axis=-1, keepdims=True)
    pg = 1.0 / jnp.sum(jnp.where(is_g, jnp.exp(logits - gmax), 0.0), axis=-1, keepdims=True)
    lo = ROUTER_LANE0 + EXPERTS_PER_GROUP * gsel
    el = jnp.where((lane >= lo) & (lane < lo + EXPERTS_PER_GROUP), logits, neg)
    v0 = jnp.max(el, axis=-1, keepdims=True)
    i0 = jnp.min(jnp.where(el == v0, lane, big), axis=-1, keepdims=True)
    el = jnp.where(lane == i0, neg, el)
    v1 = jnp.max(el, axis=-1, keepdims=True)
    i1 = jnp.min(jnp.where(el == v1, lane, big), axis=-1, keepdims=True)
    t = jnp.exp(v1 - v0)
    return lane, i0, i1, pg / (1.0 + t), pg * t / (1.0 + t)


def _outproj_kernel(a_ref, r_ref, x_ref, g1_ref, sc_ref, sh_ref, nw_ref, wo_ref, wr_ref, br_ref,
                    x1_ref, h2_ref, route_ref, cnt_ref, run_ref):
    i = pl.program_id(0)
    sub = x_ref.shape[0] // OUTPROJ_CHAINS
    half = a_ref.shape[1]
    blocks = [pl.ds(u * sub, sub) for u in range(OUTPROJ_CHAINS)]

    @pl.when(i == 0)
    def _():
        run_ref[...] = jnp.zeros_like(run_ref)

    mixes = [jnp.dot(a_ref[rows, :], wo_ref[:half, :], preferred_element_type=F32)
             + jnp.dot(r_ref[rows, :], wo_ref[half:, :], preferred_element_type=F32) for rows in blocks]
    splits = []
    for rows, mix in zip(blocks, mixes):
        x1 = x_ref[rows, :] + g1_ref[...] * mix
        x1_ref[rows, :] = x1
        h2 = _rms(x1, RMS_EPS) * nw_ref[...]
        h2 = h2 * (1.0 + sc_ref[...]) + sh_ref[...]
        h2_ref[rows, :] = h2
        h_hi = h2.astype(BF16)
        splits.append((h_hi, (h2 - h_hi.astype(F32)).astype(BF16)))
    parts = [jnp.dot(h_hi, wr_ref[...], preferred_element_type=F32)
             + jnp.dot(h_lo, wr_ref[...], preferred_element_type=F32) for h_hi, h_lo in splits]
    routed = [_route(part[:, :LANES] + part[:, LANES:] + br_ref[...]) for part in parts]
    before = jnp.where(lax.broadcasted_iota(jnp.int32, (sub, sub), 1)
                       < lax.broadcasted_iota(jnp.int32, (sub, sub), 0), 1.0, 0.0).astype(BF16)
    onehots = [jnp.where((lane == i0) | (lane == i1), 1.0, 0.0) for lane, i0, i1, _, _ in routed]
    prefixes = [jnp.dot(before, oh.astype(BF16), preferred_element_type=F32) for oh in onehots]
    run = run_ref[...]
    for rows, (lane, i0, i1, w0, w1), onehot, prefix in zip(blocks, routed, onehots, prefixes):
        prefix = prefix + run
        rank0 = jnp.sum(jnp.where(lane == i0, prefix, 0.0), axis=-1, keepdims=True)
        rank1 = jnp.sum(jnp.where(lane == i1, prefix, 0.0), axis=-1, keepdims=True)
        run = run + jnp.sum(onehot, axis=0, keepdims=True)
        e0 = (i0 - ROUTER_LANE0).astype(F32)
        e1 = (i1 - ROUTER_LANE0).astype(F32)
        slab = jnp.zeros(lane.shape, F32)
        for ln, val in enumerate((e0, e1, w0, w1, rank0, rank1)):
            slab = jnp.where(lane == ln, val, slab)
        route_ref[rows, :] = slab
    run_ref[...] = run
    cnt_ref[...] = run


def _outproj(a2, r2, x2, mod4, norm_w, w_out_bf, w_router, b_router, seq):
    n, d = x2.shape
    tm = min(OUTPROJ_TILE, seq)
    tpb = seq // tm
    row = lambda i: (i, 0)
    fixed = lambda i: (0, 0)
    modspec = lambda j: pl.BlockSpec((None, None, 1, d), lambda i: (i // tpb, j, 0, 0))
    return pl.pallas_call(
        _outproj_kernel,
        out_shape=(jax.ShapeDtypeStruct((n, d), F32), jax.ShapeDtypeStruct((n, d), F32),
                   jax.ShapeDtypeStruct((n, LANES), F32), jax.ShapeDtypeStruct((1, LANES), F32)),
        grid=(n // tm,),
        in_specs=[pl.BlockSpec((tm, a2.shape[1]), row), pl.BlockSpec((tm, r2.shape[1]), row),
                  pl.BlockSpec((tm, d), row), modspec(2), modspec(4), modspec(3),
                  pl.BlockSpec((1, d), fixed), pl.BlockSpec(w_out_bf.shape, fixed),
                  pl.BlockSpec(w_router.shape, fixed), pl.BlockSpec((1, LANES), fixed)],
        out_specs=(pl.BlockSpec((tm, d), row), pl.BlockSpec((tm, d), row),
                   pl.BlockSpec((tm, LANES), row), pl.BlockSpec((1, LANES), fixed)),
        scratch_shapes=[pltpu.VMEM((1, LANES), F32)],
        compiler_params=_params(("arbitrary",)),
        name="outproj",
    )(a2, r2, x2, mod4, mod4, mod4, norm_w.reshape(1, d), w_out_bf, w_router, b_router)


def _row_copy(src_ref, src_row, dst_ref, dst_row, sem):
    return pltpu.make_async_copy(src_ref.at[pl.ds(src_row, 1), :], dst_ref.at[pl.ds(dst_row, 1), :], sem)


def _dispatch_kernel(pad_ref, len_ref, nu_ref, dest_ref, h_ref, xs_ref, zero_s, sem):
    tm = h_ref.shape[0]

    @pl.when(pl.program_id(0) == 0)
    def _():
        zero_s[...] = jnp.zeros_like(zero_s)
        tails = []
        for e in range(N_EXPERTS):
            pos = pad_ref[e]
            for piece in PAD_PIECES:
                hit = (len_ref[e] & piece) != 0
                tails.append((hit, pltpu.make_async_copy(
                    zero_s.at[pl.ds(0, piece), :],
                    xs_ref.at[pl.ds(pl.multiple_of(pos, SUBLANES), piece), :], sem)))
                pos = pos + jnp.where(hit, piece, 0)
        n_tiles = xs_ref.shape[0] // EXPERT_TILE
        for t in range(N_EXPERTS):
            tile = nu_ref[0] + t
            row0 = pl.multiple_of(jnp.minimum(tile, n_tiles - 1) * EXPERT_TILE, EXPERT_TILE)
            tails.append((tile < n_tiles, pltpu.make_async_copy(
                zero_s, xs_ref.at[pl.ds(row0, EXPERT_TILE), :], sem)))
        for hit, cp in tails:
            pl.when(hit)(cp.start)
        for hit, cp in tails:
            pl.when(hit)(cp.wait)

    def issue(g, _):
        for u in range(DMA_UNROLL):
            r = g * DMA_UNROLL + u
            _row_copy(h_ref, r, xs_ref, dest_ref[0, r], sem).start()
            _row_copy(h_ref, r, xs_ref, dest_ref[0, tm + r], sem).start()
        return 0

    lax.fori_loop(0, tm // DMA_UNROLL, issue, 0)

    def drain(g, _):
        for _u in range(2 * DMA_UNROLL):
            _row_copy(h_ref, 0, xs_ref, 0, sem).wait()
        return 0

    lax.fori_loop(0, tm // DMA_UNROLL, drain, 0)


def _dispatch(plan, h2):
    n, d = h2.shape
    dest3 = plan["dest3"]
    tm = dest3.shape[2] // 2
    rows = plan["tile_expert"].shape[0] * EXPERT_TILE
    grid_spec = pltpu.PrefetchScalarGridSpec(
        num_scalar_prefetch=3,
        grid=(n // tm,),
        in_specs=[pl.BlockSpec((None, 1, 2 * tm), lambda i, pad, ln, nu: (i, 0, 0), memory_space=pltpu.SMEM),
                  pl.BlockSpec((tm, d), lambda i, pad, ln, nu: (i, 0))],
        out_specs=pl.BlockSpec(memory_space=pl.ANY),
        scratch_shapes=[pltpu.VMEM((EXPERT_TILE, d), F32), pltpu.SemaphoreType.DMA],
    )
    return pl.pallas_call(
        _dispatch_kernel,
        out_shape=jax.ShapeDtypeStruct((rows, d), F32),
        grid_spec=grid_spec,
        compiler_params=_params(("arbitrary",), has_side_effects=True),
        name="dispatch",
    )(plan["pad_start"], plan["pad_len"], plan["n_used"], dest3, h2)


def _experts_kernel(te_ref, nu_ref, seg_ref, nxt_ref, xs_ref, wg_hbm, wu_hbm, wd_hbm, ys_ref,
                    wg_f, wu_f, wd_f, wg_s, wu_s, wd_s, sem):
    def fetch(e, sl):
        return [pltpu.make_async_copy(src.at[e], dst.at[sl], sem.at[sl])
                for src, dst in ((wg_hbm, wg_f), (wu_hbm, wu_f), (wd_hbm, wd_f))]

    def tile(t, rows):
        @pl.when(t < nu_ref[0])
        def _():
            prev = te_ref[jnp.maximum(t - 1, 0)]

            @pl.when(t == 0)
            def _():
                for cp in fetch(te_ref[0], 0):
                    cp.start(priority=1)

            @pl.when((t == 0) | (te_ref[t] != prev))
            def _():
                sl = seg_ref[t] % 2
                for cp in fetch(te_ref[t], sl):
                    cp.wait()

                @pl.when(nxt_ref[t] >= 0)
                def _():
                    for cp in fetch(nxt_ref[t], 1 - sl):
                        cp.start(priority=1)

                wg_s[...] = wg_f[sl].astype(BF16)
                wu_s[...] = wu_f[sl].astype(BF16)
                wd_s[...] = wd_f[sl].astype(BF16)

            x = xs_ref[rows, :].astype(BF16)
            a = jnp.dot(x, wg_s[...], preferred_element_type=F32)
            u = jnp.dot(x, wu_s[...], preferred_element_type=F32)
            hid = (_silu(a) * u).astype(BF16)
            ys_ref[rows, :] = jnp.dot(hid, wd_s[...], preferred_element_type=F32)

        @pl.when(t >= nu_ref[0])
        def _():
            ys_ref[rows, :] = jnp.zeros((EXPERT_TILE, ys_ref.shape[1]), F32)

    for k in range(EXPERT_STEP_TILES):
        tile(pl.program_id(0) * EXPERT_STEP_TILES + k, pl.ds(k * EXPERT_TILE, EXPERT_TILE))


def _experts(plan, xs, w_gate, w_up, w_down):
    d = xs.shape[1]
    e, _, f = w_gate.shape
    tm = EXPERT_TILE * EXPERT_STEP_TILES
    n_tiles = plan["tile_expert"].shape[0]
    assert n_tiles % EXPERT_STEP_TILES == 0
    row = lambda i, te, nu, seg, nxt: (jnp.minimum(i, (nu[0] - 1) // EXPERT_STEP_TILES), 0)
    hbm = pl.BlockSpec(memory_space=pl.ANY)
    grid_spec = pltpu.PrefetchScalarGridSpec(
        num_scalar_prefetch=4,
        grid=(n_tiles // EXPERT_STEP_TILES,),
        in_specs=[pl.BlockSpec((tm, d), row), hbm, hbm, hbm],
        out_specs=pl.BlockSpec((tm, d), lambda i, te, nu, seg, nxt: (i, 0)),
        scratch_shapes=[pltpu.VMEM((2, d, f), F32), pltpu.VMEM((2, d, f), F32), pltpu.VMEM((2, f, d), F32),
                        pltpu.VMEM((d, f), BF16), pltpu.VMEM((d, f), BF16), pltpu.VMEM((f, d), BF16),
                        pltpu.SemaphoreType.DMA((2,))],
    )
    return pl.pallas_call(
        _experts_kernel,
        out_shape=jax.ShapeDtypeStruct((n_tiles * EXPERT_TILE, d), F32),
        grid_spec=grid_spec,
        compiler_params=_params(("arbitrary",)),
        name="experts",
    )(plan["tile_expert"], plan["n_used"], plan["segment"], plan["next_expert"], xs, w_gate, w_up, w_down)


def _combine_kernel(dest_ref, dnext_ref, ys_ref, x1_ref, route_ref, g2_ref, fw_ref, o_ref, y0_s, y1_s, sem):
    tm = x1_ref.shape[0]
    i = pl.program_id(0)
    slot = i % 2

    def gather(idx_ref, sl):
        def issue(g, _):
            for u in range(DMA_UNROLL):
                r = g * DMA_UNROLL + u
                _row_copy(ys_ref, idx_ref[0, r], y0_s.at[sl], r, sem.at[sl]).start()
                _row_copy(ys_ref, idx_ref[0, tm + r], y1_s.at[sl], r, sem.at[sl]).start()
            return 0

        lax.fori_loop(0, tm // DMA_UNROLL, issue, 0)

    @pl.when(i == 0)
    def _():
        gather(dest_ref, 0)

    @pl.when(i + 1 < pl.num_programs(0))
    def _():
        gather(dnext_ref, 1 - slot)

    def drain(g, _):
        for _u in range(DMA_UNROLL):
            _row_copy(ys_ref, 0, y0_s.at[slot], 0, sem.at[slot]).wait()
            _row_copy(ys_ref, 0, y1_s.at[slot], 0, sem.at[slot]).wait()
        return 0

    lax.fori_loop(0, tm // DMA_UNROLL, drain, 0)
    route = route_ref[...]
    w0 = route[:, 2:3]
    w1 = route[:, 3:4]
    x = x1_ref[...] + g2_ref[...] * (w0 * y0_s[slot] + w1 * y1_s[slot])
    o_ref[...] = _rms(x, RMS_EPS) * fw_ref[...]


def _combine(dest3, ys, x1, route, mod4, final_w, seq):
    n, d = x1.shape
    tm = dest3.shape[2] // 2
    tpb = seq // tm
    row = lambda i: (i, 0)
    steps = n // tm
    return pl.pallas_call(
        _combine_kernel,
        out_shape=jax.ShapeDtypeStruct((n, d), F32),
        grid=(steps,),
        in_specs=[pl.BlockSpec((None, 1, 2 * tm), lambda i: (i, 0, 0), memory_space=pltpu.SMEM),
                  pl.BlockSpec((None, 1, 2 * tm), lambda i: (jnp.minimum(i + 1, steps - 1), 0, 0),
                               memory_space=pltpu.SMEM),
                  pl.BlockSpec(memory_space=pl.ANY),
                  pl.BlockSpec((tm, d), row), pl.BlockSpec((tm, LANES), row),
                  pl.BlockSpec((None, None, 1, d), lambda i: (i // tpb, 5, 0, 0)),
                  pl.BlockSpec((1, d), lambda i: (0, 0))],
        out_specs=pl.BlockSpec((tm, d), row),
        scratch_shapes=[pltpu.VMEM((2, tm, d), F32), pltpu.VMEM((2, tm, d), F32),
                        pltpu.SemaphoreType.DMA((2,))],
        compiler_params=_params(("arbitrary",)),
        name="combine",
    )(dest3, dest3, ys, x1, route, mod4, final_w.reshape(1, d))


PLAN_LANES = 2 * LANES


def _plan_kernel(cnt_ref, route_ref, dest_ref, plan_ref):
    cnt = cnt_ref[...]
    tiles = jnp.floor((cnt + (EXPERT_TILE - 1.0)) * (1.0 / EXPERT_TILE))
    sq = (LANES, LANES)
    row = lax.broadcasted_iota(jnp.int32, sq, 0)
    col = lax.broadcasted_iota(jnp.int32, sq, 1)
    tiles_b = jnp.broadcast_to(tiles, sq)
    tiles_col = jnp.sum(jnp.where(col == row, tiles_b, 0.0), axis=1, keepdims=True)
    end_col = jnp.sum(jnp.where(col <= row, tiles_b, 0.0), axis=1, keepdims=True)
    end_row = jnp.sum(jnp.where(row <= col, jnp.broadcast_to(tiles_col, sq), 0.0), axis=0, keepdims=True)
    start_row = end_row - tiles

    route_t = route_ref[...].T
    blk = route_t.shape[1]
    sub = lax.broadcasted_iota(jnp.int32, (LANES, blk), 0)
    start8 = jnp.broadcast_to(start_row, (SUBLANES, LANES)).astype(BF16)
    dest_ref[...] = jnp.zeros_like(dest_ref)
    for k in range(2):
        e_sub = route_t[k:k + 1, :].astype(jnp.int32) + ROUTER_LANE0
        onehot = jnp.where(sub == e_sub, 1.0, 0.0).astype(BF16)
        start = jnp.dot(start8, onehot, preferred_element_type=F32)[0:1, :]
        dest_ref[k:k + 1, :] = (route_t[4 + k:5 + k, :] + EXPERT_TILE * start).astype(jnp.int32)

    @pl.when(pl.program_id(0) == 0)
    def _():
        wide = (LANES, PLAN_LANES)
        t = lax.broadcasted_iota(jnp.int32, wide, 1).astype(F32)
        e_id = lax.broadcasted_iota(jnp.int32, wide, 0) - ROUTER_LANE0
        is_e = (e_id >= 0) & (e_id < N_EXPERTS)
        n_used = jnp.max(end_row, axis=1, keepdims=True)
        t = jnp.minimum(t, n_used - 1.0)
        end_w = jnp.broadcast_to(end_col, wide)
        has_w = is_e & (jnp.broadcast_to(tiles_col, wide) > 0.0)
        te = jnp.sum(jnp.where(is_e & (end_w <= t), 1, 0), axis=0, keepdims=True)
        seg = jnp.sum(jnp.where(has_w & (e_id < te), 1, 0), axis=0, keepdims=True)
        nxt = jnp.min(jnp.where(has_w & (e_id > te), e_id, N_EXPERTS), axis=0, keepdims=True)
        nxt = jnp.where(nxt == N_EXPERTS, -1, nxt)
        pad_start = jnp.floor((start_row * EXPERT_TILE + cnt) * (1.0 / SUBLANES)) * SUBLANES
        pad_len = end_row * EXPERT_TILE - pad_start
        zeros = jnp.zeros((1, PLAN_LANES - LANES), jnp.int32)
        plan_ref[...] = jnp.zeros_like(plan_ref)
        plan_ref[0:1, :] = te
        plan_ref[1:2, :] = seg
        plan_ref[2:3, :] = nxt
        plan_ref[3:4, :] = jnp.concatenate([pad_start.astype(jnp.int32), zeros], axis=1)
        plan_ref[4:5, :] = jnp.concatenate([pad_len.astype(jnp.int32), zeros], axis=1)
        plan_ref[5:6, :] = jnp.broadcast_to(n_used, (1, PLAN_LANES)).astype(jnp.int32)


def _plan(route, counts_row, seq):
    n = route.shape[0]
    tm = min(MOVE_TILE, seq)
    n_tiles = (2 * n) // EXPERT_TILE + N_EXPERTS
    assert n_tiles <= PLAN_LANES
    blk = min(n, 2048)
    dest, plan = pl.pallas_call(
        _plan_kernel,
        out_shape=(jax.ShapeDtypeStruct((SUBLANES, n), jnp.int32),
                   jax.ShapeDtypeStruct((SUBLANES, PLAN_LANES), jnp.int32)),
        grid=(n // blk,),
        in_specs=[pl.BlockSpec((1, LANES), lambda i: (0, 0)), pl.BlockSpec((blk, LANES), lambda i: (i, 0))],
        out_specs=(pl.BlockSpec((SUBLANES, blk), lambda i: (0, i)),
                   pl.BlockSpec((SUBLANES, PLAN_LANES), lambda i: (0, 0))),
        compiler_params=_params(("arbitrary",)),
        name="plan",
    )(counts_row, route)
    dest3 = jnp.concatenate([dest[0].reshape(n // tm, 1, tm), dest[1].reshape(n // tm, 1, tm)], axis=2)
    experts = slice(ROUTER_LANE0, ROUTER_LANE0 + N_EXPERTS)
    return {"dest3": dest3, "tile_expert": plan[0, :n_tiles], "segment": plan[1, :n_tiles],
            "next_expert": plan[2, :n_tiles], "pad_start": plan[3, experts], "pad_len": plan[4, experts],
            "n_used": plan[5, :1]}


def kernel(x, c, positions, w_ada, b_ada, norm1_w, w_in, lambda_q1, lambda_k1, lambda_q2, lambda_k2, subln_w, hgrn_lb_logits, gnorm_w, w_out, norm2_w, w_group_router, b_group_router, w_expert_router, b_expert_router, w_gate, w_up, w_down, final_norm_w):
    b, s, d = x.shape
    n = b * s
    depth = w_ada.shape[0]
    assert depth == 1, "the combine kernel applies the final norm, so exactly one layer is supported"
    half = DIFF_HEAD_DIM // 2

    inv_freq = ROPE_THETA ** (-jnp.arange(half, dtype=F32) / half)
    pos = jnp.broadcast_to(positions.astype(F32).reshape(n, 1), (n, LANES))
    freq = jnp.tile(inv_freq, LANES // half).reshape(1, LANES)
    sign = jnp.tile(jnp.concatenate([-jnp.ones((half,), F32), jnp.ones((half,), F32)]),
                    LANES // DIFF_HEAD_DIM).reshape(1, LANES)

    xf = x.reshape(n, d)
    for l in range(depth):
        lambda_init = 0.8 - 0.6 * math.exp(-0.3 * l)
        mod4 = _mod(c, w_ada[l], b_ada[l]).reshape(b, 6, 1, d)
        q, k, v, hq, hf, hi, hg = _inproj(xf, mod4, norm1_w[l], w_in[l].astype(BF16), pos, freq, sign, s)
        to3 = lambda t: t.reshape(b, s, SEG)
        a = _attn(to3(q), to3(k), to3(v), lambda_q1[l], lambda_k1[l], lambda_q2[l], lambda_k2[l],
                  subln_w[l], lambda_init)
        r = _hgrn(to3(hq), to3(hf), to3(hi), to3(hg), hgrn_lb_logits, gnorm_w[l], l)

        w_router = jnp.zeros((d, LANES), F32)
        w_router = w_router.at[:, :N_GROUPS].set(w_group_router[l])
        w_router = w_router.at[:, ROUTER_LANE0:ROUTER_LANE0 + N_EXPERTS].set(w_expert_router[l])
        b_router = jnp.zeros((1, LANES), F32)
        b_router = b_router.at[0, :N_GROUPS].set(b_group_router[l])
        b_router = b_router.at[0, ROUTER_LANE0:ROUTER_LANE0 + N_EXPERTS].set(b_expert_router[l].reshape(-1))
        wr_hi = w_router.astype(BF16)
        wr_split = jnp.concatenate([wr_hi, (w_router - wr_hi.astype(F32)).astype(BF16)], axis=1)
        x1, h2, route, counts = _outproj(a.reshape(n, SEG), r.reshape(n, SEG), xf, mod4, norm2_w[l],
                                         w_out[l].astype(BF16), wr_split, b_router, s)

        plan = _plan(route, counts, s)
        xs = _dispatch(plan, h2)
        e_w = lambda w: w.reshape((N_EXPERTS,) + w.shape[2:])
        ys = _experts(plan, xs, e_w(w_gate[l]), e_w(w_up[l]), e_w(w_down[l]))
        xf = _combine(plan["dest3"], ys, x1, route, mod4, final_norm_w, s)
    return xf.reshape(b, s, d)
```

```python
import functools
import math

import jax
import jax.numpy as jnp
from jax import lax
from jax.experimental import pallas as pl
from jax.experimental.pallas import tpu as pltpu

F32 = jnp.float32
BF16 = jnp.bfloat16
HIGHEST = lax.Precision.HIGHEST

N_DIFF_HEADS = 4
DIFF_HEAD_DIM = 64
DIFF_V_DIM = 2 * DIFF_HEAD_DIM
N_HGRN_HEADS = 4
HGRN_EXPAND = 128
HGRN_V_DIM = 128
ROPE_THETA = 10000.0
N_GROUPS = 4
EXPERTS_PER_GROUP = 8
N_EXPERTS = N_GROUPS * EXPERTS_PER_GROUP
RMS_EPS = 1e-6
SUBLN_EPS = 1e-5
SEG = 512
LANES = 128
SUBLANES = 8
ROUTER_LANE0 = N_GROUPS

ROW_TILE = 1024
DISPATCH_TILE = 1024
COMBINE_TILE = 512
OUTPROJ_TILE = 1024
OUTPROJ_CHAINS = 4
ATTN_TILE = 1024
ATTN_KV_TILE = 256
ATTN_ONES_ROWS = 16
LOG2E = math.log2(math.e)
HGRN_CHUNK = 64
HGRN_SUB = 16
HGRN_UNROLL = 32
EXPERT_TILE = 256
EXPERT_STEP_TILES = 4
PAD_PIECES = (256, 128, 64, 32, 16, 8)
DMA_UNROLL = 16
VMEM_LIMIT = 48 * 1024 * 1024


def _params(sem, **kw):
    return pltpu.CompilerParams(dimension_semantics=sem, vmem_limit_bytes=VMEM_LIMIT, **kw)


def _silu(x):
    return x * jax.nn.sigmoid(x)


def _rms(x, eps):
    return x * lax.rsqrt(jnp.mean(x * x, axis=-1, keepdims=True) + eps)


def _mod_kernel(c_ref, w_ref, b_ref, o_ref):
    ca = _silu(c_ref[...])
    o_ref[...] = jnp.dot(ca, w_ref[...], preferred_element_type=F32, precision=HIGHEST) + b_ref[...]


def _mod(c, w_ada, b_ada):
    bsz, d = c.shape
    n_out = w_ada.shape[1]
    return pl.pallas_call(
        _mod_kernel,
        out_shape=jax.ShapeDtypeStruct((bsz, n_out), F32),
        grid=(n_out // d,),
        in_specs=[pl.BlockSpec((bsz, d), lambda j: (0, 0)),
                  pl.BlockSpec((d, d), lambda j: (0, j)),
                  pl.BlockSpec((1, d), lambda j: (0, j))],
        out_specs=pl.BlockSpec((bsz, d), lambda j: (0, j)),
        compiler_params=_params(("arbitrary",)),
        name="mod",
    )(c, w_ada, b_ada.reshape(1, n_out))


def _inproj_kernel(x_ref, sc_ref, sh_ref, nw_ref, w_ref, pos_ref, freq_ref, sign_ref,
                   q_ref, k_ref, v_ref, hq_ref, hf_ref, hi_ref, hg_ref):
    h = _rms(x_ref[...], RMS_EPS) * nw_ref[...]
    h = h * (1.0 + sc_ref[...]) + sh_ref[...]
    hb = h.astype(BF16)
    ang = pos_ref[...] * freq_ref[...]
    cos = jnp.cos(ang)
    sin = jnp.sin(ang) * sign_ref[...]
    lane = lax.broadcasted_iota(jnp.int32, cos.shape, 1)
    first_half = (lane % DIFF_HEAD_DIM) < (DIFF_HEAD_DIM // 2)

    def seg(i):
        return jnp.dot(hb, w_ref[:, i * SEG:(i + 1) * SEG], preferred_element_type=F32)

    def rope(p, scale):
        outs = []
        for c in range(SEG // LANES):
            pc = p[:, c * LANES:(c + 1) * LANES]
            partner = jnp.where(first_half, pltpu.roll(pc, LANES - DIFF_HEAD_DIM // 2, 1),
                                pltpu.roll(pc, DIFF_HEAD_DIM // 2, 1))
            outs.append((pc * cos + partner * sin) * scale)
        return jnp.concatenate(outs, axis=1)

    q_ref[...] = rope(seg(0), DIFF_HEAD_DIM ** -0.5 * LOG2E).astype(BF16)
    k_ref[...] = rope(seg(1), 1.0).astype(BF16)
    v_ref[...] = seg(2).astype(BF16)
    hq_ref[...] = seg(3).astype(BF16)
    hf_ref[...] = seg(4)
    hi_ref[...] = seg(5).astype(BF16)
    hg_ref[...] = seg(6).astype(BF16)


def _inproj(x2, mod4, norm_w, w_in_bf, pos, freq, sign, seq):
    n, d = x2.shape
    tm = min(ROW_TILE, seq)
    tpb = seq // tm
    row = lambda i: (i, 0)
    modspec = lambda j: pl.BlockSpec((None, None, 1, d), lambda i: (i // tpb, j, 0, 0))
    seg_spec = pl.BlockSpec((tm, SEG), row)
    bf = jax.ShapeDtypeStruct((n, SEG), BF16)
    return pl.pallas_call(
        _inproj_kernel,
        out_shape=(bf, bf, bf, bf, jax.ShapeDtypeStruct((n, SEG), F32), bf, bf),
        grid=(n // tm,),
        in_specs=[pl.BlockSpec((tm, d), row), modspec(1), modspec(0),
                  pl.BlockSpec((1, d), lambda i: (0, 0)),
                  pl.BlockSpec(w_in_bf.shape, lambda i: (0, 0)),
                  pl.BlockSpec((tm, LANES), row),
                  pl.BlockSpec((1, LANES), lambda i: (0, 0)), pl.BlockSpec((1, LANES), lambda i: (0, 0))],
        out_specs=(seg_spec,) * 7,
        compiler_params=_params(("arbitrary",)),
        name="inproj",
    )(x2, mod4, mod4, norm_w.reshape(1, d), w_in_bf, pos, freq, sign)


def _attn_kernel(q_ref, k_ref, v_ref, lq1_ref, lk1_ref, lq2_ref, lk2_ref, sw_ref, o_ref,
                 vt_s, acc_s, *, lambda_init):
    tq = q_ref.shape[0]
    seq = k_ref.shape[0]
    tk = min(ATTN_KV_TILE, tq)
    vd = DIFF_V_DIM
    qi = pl.program_id(1)
    nt = (((1,), (1,)), ((), ()))
    chains = [(h, mp) for h in range(N_DIFF_HEADS) for mp in range(2)]

    va = vd + ATTN_ONES_ROWS

    @pl.when(qi == 0)
    def _():
        for h in range(N_DIFF_HEADS):
            vt_s[h * va + vd:(h + 1) * va, :] = jnp.ones((ATTN_ONES_ROWS, seq), BF16)
            for c in range(seq // tk):
                vt_s[h * va:h * va + vd, c * tk:(c + 1) * tk] = (
                    v_ref[c * tk:(c + 1) * tk, h * vd:(h + 1) * vd].astype(F32).T.astype(BF16))

    acc_s[...] = jnp.zeros_like(acc_s)
    lane = lax.broadcasted_iota(jnp.int32, (tq, vd), 1)
    qz = []
    for h, mp in chains:
        qh = q_ref[:, h * vd:(h + 1) * vd]
        keep = (lane < DIFF_HEAD_DIM) if mp == 0 else (lane >= DIFF_HEAD_DIM)
        qz.append(jnp.where(keep, qh, jnp.zeros_like(qh)))

    def step(j, ms, c0):
        row0 = pl.multiple_of(j * tk, tk)
        q0 = c0 or 0
        scores = [lax.dot_general(k_ref[pl.ds(row0, tk), h * vd:(h + 1) * vd], qz[c][q0:], nt,
                                  preferred_element_type=F32) for c, (h, _) in enumerate(chains)]
        new_ms, alphas, probs = [], [], []
        for c, s in enumerate(scores):
            if c0 is not None:
                kidx = lax.broadcasted_iota(jnp.int32, s.shape, 0)
                qidx = lax.broadcasted_iota(jnp.int32, s.shape, 1)
                s = jnp.where(kidx <= qidx, s, -jnp.inf)
            m_old = ms[c][:, q0:]
            m_new = jnp.maximum(m_old, jnp.max(s, axis=0, keepdims=True))
            alphas.append(jnp.exp2(m_old - m_new))
            probs.append(jnp.exp2(s - m_new).astype(BF16))
            new_ms.append(m_new if q0 == 0 else jnp.concatenate([ms[c][:, :q0], m_new], axis=1))
        pvs = [jnp.dot(vt_s[h * va:(h + 1) * va, pl.ds(row0, tk)], probs[c], preferred_element_type=F32)
               for c, (h, _) in enumerate(chains)]
        for c, pv in enumerate(pvs):
            acc_s[c, :, q0:] = alphas[c] * acc_s[c, :, q0:] + pv
        return tuple(new_ms)

    init = tuple(jnp.full((1, tq), -jnp.inf, F32) for _ in chains)
    per_q = tq // tk
    ms = lax.fori_loop(0, qi * per_q, lambda j, c: step(j, c, None), init)
    for d in range(per_q):
        ms = step(qi * per_q + d, ms, d * tk)

    lam = (jnp.exp(jnp.sum(lq1_ref[...] * lk1_ref[...], axis=-1, keepdims=True))
           - jnp.exp(jnp.sum(lq2_ref[...] * lk2_ref[...], axis=-1, keepdims=True)) + lambda_init)
    sw = jnp.concatenate([sw_ref[...]] * (tq // LANES), axis=1)
    for h in range(N_DIFF_HEADS):
        inv1 = 1.0 / acc_s[2 * h, vd:vd + 1, :]
        inv2 = 1.0 / acc_s[2 * h + 1, vd:vd + 1, :]
        o = acc_s[2 * h, :vd, :] * inv1 - lam * (acc_s[2 * h + 1, :vd, :] * inv2)
        o = o * lax.rsqrt(jnp.mean(o * o, axis=0, keepdims=True) + SUBLN_EPS)
        o_ref[:, h * vd:(h + 1) * vd] = (o * sw * (1.0 - lambda_init)).T.astype(BF16)


def _attn(q, k, v, lq1, lk1, lq2, lk2, subln_w, lambda_init):
    b, s, width = q.shape
    tq = min(ATTN_TILE, s)
    vec = lambda a: a.reshape(1, -1)
    small = lambda w: pl.BlockSpec((1, w), lambda bi, i: (0, 0))
    kv_spec = pl.BlockSpec((None, s, width), lambda bi, i: (bi, 0, 0))
    q_spec = pl.BlockSpec((None, tq, width), lambda bi, i: (bi, i, 0))
    sw_cols = jnp.broadcast_to(subln_w.reshape(DIFF_V_DIM, 1), (DIFF_V_DIM, LANES))
    return pl.pallas_call(
        functools.partial(_attn_kernel, lambda_init=lambda_init),
        out_shape=jax.ShapeDtypeStruct(q.shape, BF16),
        grid=(b, s // tq),
        in_specs=[q_spec, kv_spec, kv_spec, small(DIFF_HEAD_DIM), small(DIFF_HEAD_DIM),
                  small(DIFF_HEAD_DIM), small(DIFF_HEAD_DIM),
                  pl.BlockSpec((DIFF_V_DIM, LANES), lambda bi, i: (0, 0))],
        out_specs=q_spec,
        scratch_shapes=[pltpu.VMEM((N_DIFF_HEADS * (DIFF_V_DIM + ATTN_ONES_ROWS), s), BF16),
                        pltpu.VMEM((2 * N_DIFF_HEADS, DIFF_V_DIM + ATTN_ONES_ROWS, tq), F32)],
        compiler_params=_params(("arbitrary",) * 2),
        name="attn",
    )(q, k, v, vec(lq1), vec(lk1), vec(lq2), vec(lk2), sw_cols)


def _hgrn_kernel(q_ref, f_ref, i_ref, g_ref, lbl_ref, gw_ref, o_ref, qd_s, oi_s, u_s, d_s, *, layer):
    seq = q_ref.shape[0]
    ck, sub = HGRN_CHUNK, HGRN_SUB
    lg = lbl_ref[...]
    e = jnp.exp(lg - jnp.max(lg, axis=0, keepdims=True))
    lb = jnp.sum(e[:layer + 1], axis=0, keepdims=True) / jnp.sum(e, axis=0, keepdims=True)
    gw = gw_ref[...]
    causal = (lax.broadcasted_iota(jnp.int32, (ck, ck), 1)
              <= lax.broadcasted_iota(jnp.int32, (ck, ck), 0))
    tril = causal.astype(BF16)
    nt = (((1,), (1,)), ((), ()))
    tn = (((0,), (0,)), ((), ()))
    w = HGRN_EXPAND
    nsub = ck // sub
    unroll = min(HGRN_UNROLL, seq // ck)

    def phase1(c2, _):
        chunks = [c2 * unroll + u for u in range(unroll)]
        rows = [pl.ds(pl.multiple_of(c * ck, ck), ck) for c in chunks]
        kins, splits = [], []
        for r in rows:
            forget = lb + (1.0 - lb) * jax.nn.sigmoid(f_ref[r, :])
            kins.append(1.0 - forget)
            g = jnp.log2(forget)
            g_hi = g.astype(BF16)
            r1 = g - g_hi.astype(F32)
            g_mid = r1.astype(BF16)
            g_lo = (r1 - g_mid.astype(F32)).astype(BF16)
            splits.append(jnp.concatenate([g_hi, g_mid, g_lo], axis=1))
        sums = [jnp.dot(tril, x, preferred_element_type=F32) for x in splits]

        q_blk, k_blk, kdecs = [], [], []
        for c, r, kin, cs in zip(chunks, rows, kins, sums):
            bsum = cs[:, :w] + cs[:, w:2 * w] + cs[:, 2 * w:]
            qf = _silu(q_ref[r, :].astype(F32))
            qd_s[r, :] = (qf * jnp.exp2(bsum)).astype(BF16)
            q_rows, k_cols = [], []
            for i in range(nsub):
                lo, hi = i * sub, (i + 1) * sub
                ref_pt = bsum[lo:lo + 1, :]
                qh = (qf[lo:hi] * jnp.exp2(bsum[lo:hi] - ref_pt)).astype(BF16)
                kh = (kin[:hi] * jnp.exp2(ref_pt - bsum[:hi])).astype(BF16)
                pieces = [qh if j == i else jnp.zeros((sub, w), BF16) for j in range(nsub)]
                q_rows.append(jnp.concatenate(pieces, axis=1))
                k_cols.append(kh if hi == ck else jnp.concatenate([kh, jnp.zeros((ck - hi, w), BF16)], axis=0))
            q_blk.append(jnp.concatenate(q_rows, axis=0))
            k_blk.append(jnp.concatenate(k_cols, axis=1))
            last = bsum[ck - 1:ck, :]
            kdecs.append((kin * jnp.exp2(last - bsum)).astype(BF16))
            d_s[c] = jnp.exp2(last)
        vbs = [i_ref[r, :] for r in rows]
        atts = [lax.dot_general(qb, kb, nt, preferred_element_type=F32) for qb, kb in zip(q_blk, k_blk)]
        for c, vb, kdec in zip(chunks, vbs, kdecs):
            u_s[c] = lax.dot_general(vb, kdec, tn, preferred_element_type=F32)
        probs = [jnp.where(causal, att, 0.0).astype(BF16) for att in atts]
        for r, p, vb in zip(rows, probs, vbs):
            oi_s[r, :] = jnp.dot(p, vb, preferred_element_type=F32)
        return 0

    lax.fori_loop(0, seq // (ck * unroll), phase1, 0)

    def phase2(c2, state_t):
        for u in range(unroll):
            c = c2 * unroll + u
            rows = pl.ds(pl.multiple_of(c * ck, ck), ck)
            o = oi_s[rows, :] + lax.dot_general(qd_s[rows, :], state_t.astype(BF16), nt,
                                                preferred_element_type=F32)
            o = _rms(o, RMS_EPS) * gw * _silu(g_ref[rows, :].astype(F32))
            o_ref[rows, :] = o.astype(BF16)
            state_t = state_t * d_s[c] + u_s[c]
        return state_t

    lax.fori_loop(0, seq // (ck * unroll), phase2, jnp.zeros((HGRN_V_DIM, HGRN_EXPAND), F32))


def _hgrn(hq, hf, hi, hg, lb_logits, gnorm_w, layer):
    b, s, _ = hq.shape
    w = HGRN_EXPAND
    blk = pl.BlockSpec((None, s, w), lambda bi, h: (bi, 0, h))
    return pl.pallas_call(
        functools.partial(_hgrn_kernel, layer=layer),
        out_shape=jax.ShapeDtypeStruct(hq.shape, BF16),
        grid=(b, N_HGRN_HEADS),
        in_specs=[blk, blk, blk, blk,
                  pl.BlockSpec((lb_logits.shape[0], w), lambda bi, h: (0, h)),
                  pl.BlockSpec((1, w), lambda bi, h: (0, 0))],
        out_specs=blk,
        scratch_shapes=[pltpu.VMEM((s, w), BF16), pltpu.VMEM((s, HGRN_V_DIM), F32),
                        pltpu.VMEM((s // HGRN_CHUNK, HGRN_V_DIM, w), F32),
                        pltpu.VMEM((s // HGRN_CHUNK, 1, w), F32)],
        compiler_params=_params(("arbitrary",) * 2),
        name="hgrn",
    )(hq, hf, hi, hg, lb_logits, gnorm_w.reshape(1, w))


def _route(logits):
    lane = lax.broadcasted_iota(jnp.int32, logits.shape, 1)
    big = jnp.int32(LANES)
    neg = -jnp.inf
    is_g = lane < N_GROUPS
    gl = jnp.where(is_g, logits, neg)
    gmax = jnp.max(gl, axis=-1, keepdims=True)
    gsel = jnp.min(jnp.where(gl == gmax, lane, big), axis=-1, keepdims=True)
    pg = 1.0 / jnp.sum(jnp.where(is_g, jnp.exp(logits - gmax), 0.0), axis=-1, keepdims=True)
    lo = ROUTER_LANE0 + EXPERTS_PER_GROUP * gsel
    el = jnp.where((lane >= lo) & (lane < lo + EXPERTS_PER_GROUP), logits, neg)
    v0 = jnp.max(el, axis=-1, keepdims=True)
    i0 = jnp.min(jnp.where(el == v0, lane, big), axis=-1, keepdims=True)
    el = jnp.where(lane == i0, neg, el)
    v1 = jnp.max(el, axis=-1, keepdims=True)
    i1 = jnp.min(jnp.where(el == v1, lane, big), axis=-1, keepdims=True)
    t = jnp.exp(v1 - v0)
    return lane, i0, i1, pg / (1.0 + t), pg * t / (1.0 + t)


def _outproj_kernel(a_ref, r_ref, x_ref, g1_ref, sc_ref, sh_ref, nw_ref, wo_ref, wr_ref, br_ref,
                    x1_ref, h2_ref, route_ref, cnt_ref, run_ref):
    i = pl.program_id(0)
    sub = x_ref.shape[0] // OUTPROJ_CHAINS
    half = a_ref.shape[1]
    blocks = [pl.ds(u * sub, sub) for u in range(OUTPROJ_CHAINS)]

    @pl.when(i == 0)
    def _():
        run_ref[...] = jnp.zeros_like(run_ref)

    mixes = [jnp.dot(a_ref[rows, :], wo_ref[:half, :], preferred_element_type=F32)
             + jnp.dot(r_ref[rows, :], wo_ref[half:, :], preferred_element_type=F32) for rows in blocks]
    splits = []
    for rows, mix in zip(blocks, mixes):
        x1 = x_ref[rows, :] + g1_ref[...] * mix
        x1_ref[rows, :] = x1
        h2 = _rms(x1, RMS_EPS) * nw_ref[...]
        h2 = h2 * (1.0 + sc_ref[...]) + sh_ref[...]
        h2_ref[rows, :] = h2
        h_hi = h2.astype(BF16)
        splits.append((h_hi, (h2 - h_hi.astype(F32)).astype(BF16)))
    parts = [jnp.dot(h_hi, wr_ref[...], preferred_element_type=F32)
             + jnp.dot(h_lo, wr_ref[...], preferred_element_type=F32) for h_hi, h_lo in splits]
    routed = [_route(part[:, :LANES] + part[:, LANES:] + br_ref[...]) for part in parts]
    before = jnp.where(lax.broadcasted_iota(jnp.int32, (sub, sub), 1)
                       < lax.broadcasted_iota(jnp.int32, (sub, sub), 0), 1.0, 0.0).astype(BF16)
    onehots = [jnp.where((lane == i0) | (lane == i1), 1.0, 0.0) for lane, i0, i1, _, _ in routed]
    prefixes = [jnp.dot(before, oh.astype(BF16), preferred_element_type=F32) for oh in onehots]
    run = run_ref[...]
    for rows, (lane, i0, i1, w0, w1), onehot, prefix in zip(blocks, routed, onehots, prefixes):
        prefix = prefix + run
        rank0 = jnp.sum(jnp.where(lane == i0, prefix, 0.0), axis=-1, keepdims=True)
        rank1 = jnp.sum(jnp.where(lane == i1, prefix, 0.0), axis=-1, keepdims=True)
        run = run + jnp.sum(onehot, axis=0, keepdims=True)
        e0 = (i0 - ROUTER_LANE0).astype(F32)
        e1 = (i1 - ROUTER_LANE0).astype(F32)
        slab = jnp.zeros(lane.shape, F32)
        for ln, val in enumerate((e0, e1, w0, w1, rank0, rank1)):
            slab = jnp.where(lane == ln, val, slab)
        route_ref[rows, :] = slab
    run_ref[...] = run
    cnt_ref[...] = run


def _outproj(a2, r2, x2, mod4, norm_w, w_out_bf, w_router, b_router, seq):
    n, d = x2.shape
    tm = min(OUTPROJ_TILE, seq)
    tpb = seq // tm
    row = lambda i: (i, 0)
    fixed = lambda i: (0, 0)
    modspec = lambda j: pl.BlockSpec((None, None, 1, d), lambda i: (i // tpb, j, 0, 0))
    return pl.pallas_call(
        _outproj_kernel,
        out_shape=(jax.ShapeDtypeStruct((n, d), F32), jax.ShapeDtypeStruct((n, d), F32),
                   jax.ShapeDtypeStruct((n, LANES), F32), jax.ShapeDtypeStruct((1, LANES), F32)),
        grid=(n // tm,),
        in_specs=[pl.BlockSpec((tm, a2.shape[1]), row), pl.BlockSpec((tm, r2.shape[1]), row),
                  pl.BlockSpec((tm, d), row), modspec(2), modspec(4), modspec(3),
                  pl.BlockSpec((1, d), fixed), pl.BlockSpec(w_out_bf.shape, fixed),
                  pl.BlockSpec(w_router.shape, fixed), pl.BlockSpec((1, LANES), fixed)],
        out_specs=(pl.BlockSpec((tm, d), row), pl.BlockSpec((tm, d), row),
                   pl.BlockSpec((tm, LANES), row), pl.BlockSpec((1, LANES), fixed)),
        scratch_shapes=[pltpu.VMEM((1, LANES), F32)],
        compiler_params=_params(("arbitrary",)),
        name="outproj",
    )(a2, r2, x2, mod4, mod4, mod4, norm_w.reshape(1, d), w_out_bf, w_router, b_router)


def _row_copy(src_ref, src_row, dst_ref, dst_row, sem):
    return pltpu.make_async_copy(src_ref.at[pl.ds(src_row, 1), :], dst_ref.at[pl.ds(dst_row, 1), :], sem)


def _dispatch_kernel(pad_ref, len_ref, nu_ref, dest_ref, h_ref, xs_ref, zero_s, sem):
    tm = h_ref.shape[0]

    @pl.when(pl.program_id(0) == 0)
    def _():
        zero_s[...] = jnp.zeros_like(zero_s)
        tails = []
        for e in range(N_EXPERTS):
            pos = pad_ref[e]
            for piece in PAD_PIECES:
                hit = (len_ref[e] & piece) != 0
                tails.append((hit, pltpu.make_async_copy(
                    zero_s.at[pl.ds(0, piece), :],
                    xs_ref.at[pl.ds(pl.multiple_of(pos, SUBLANES), piece), :], sem)))
                pos = pos + jnp.where(hit, piece, 0)
        n_tiles = xs_ref.shape[0] // EXPERT_TILE
        for t in range(N_EXPERTS):
            tile = nu_ref[0] + t
            row0 = pl.multiple_of(jnp.minimum(tile, n_tiles - 1) * EXPERT_TILE, EXPERT_TILE)
            tails.append((tile < n_tiles, pltpu.make_async_copy(
                zero_s, xs_ref.at[pl.ds(row0, EXPERT_TILE), :], sem)))
        for hit, cp in tails:
            pl.when(hit)(cp.start)
        for hit, cp in tails:
            pl.when(hit)(cp.wait)

    def issue(g, _):
        for u in range(DMA_UNROLL):
            r = g * DMA_UNROLL + u
            _row_copy(h_ref, r, xs_ref, dest_ref[0, r], sem).start()
            _row_copy(h_ref, r, xs_ref, dest_ref[0, tm + r], sem).start()
        return 0

    lax.fori_loop(0, tm // DMA_UNROLL, issue, 0)

    def drain(g, _):
        for _u in range(2 * DMA_UNROLL):
            _row_copy(h_ref, 0, xs_ref, 0, sem).wait()
        return 0

    lax.fori_loop(0, tm // DMA_UNROLL, drain, 0)


def _dispatch(plan, h2):
    n, d = h2.shape
    dest3 = plan["dest3"]
    tm = dest3.shape[2] // 2
    rows = plan["tile_expert"].shape[0] * EXPERT_TILE
    grid_spec = pltpu.PrefetchScalarGridSpec(
        num_scalar_prefetch=3,
        grid=(n // tm,),
        in_specs=[pl.BlockSpec((None, 1, 2 * tm), lambda i, pad, ln, nu: (i, 0, 0), memory_space=pltpu.SMEM),
                  pl.BlockSpec((tm, d), lambda i, pad, ln, nu: (i, 0))],
        out_specs=pl.BlockSpec(memory_space=pl.ANY),
        scratch_shapes=[pltpu.VMEM((EXPERT_TILE, d), F32), pltpu.SemaphoreType.DMA],
    )
    return pl.pallas_call(
        _dispatch_kernel,
        out_shape=jax.ShapeDtypeStruct((rows, d), F32),
        grid_spec=grid_spec,
        compiler_params=_params(("arbitrary",), has_side_effects=True),
        name="dispatch",
    )(plan["pad_start"], plan["pad_len"], plan["n_used"], dest3, h2)


def _experts_kernel(te_ref, nu_ref, seg_ref, nxt_ref, xs_ref, wg_hbm, wu_hbm, wd_hbm, ys_ref,
                    wg_f, wu_f, wd_f, wg_s, wu_s, wd_s, sem):
    def fetch(e, sl):
        return [pltpu.make_async_copy(src.at[e], dst.at[sl], sem.at[sl])
                for src, dst in ((wg_hbm, wg_f), (wu_hbm, wu_f), (wd_hbm, wd_f))]

    def tile(t, rows):
        @pl.when(t < nu_ref[0])
        def _():
            prev = te_ref[jnp.maximum(t - 1, 0)]

            @pl.when(t == 0)
            def _():
                for cp in fetch(te_ref[0], 0):
                    cp.start(priority=1)

            @pl.when((t == 0) | (te_ref[t] != prev))
            def _():
                sl = seg_ref[t] % 2
                for cp in fetch(te_ref[t], sl):
                    cp.wait()

                @pl.when(nxt_ref[t] >= 0)
                def _():
                    for cp in fetch(nxt_ref[t], 1 - sl):
                        cp.start(priority=1)

                wg_s[...] = wg_f[sl].astype(BF16)
                wu_s[...] = wu_f[sl].astype(BF16)
                wd_s[...] = wd_f[sl].astype(BF16)

            x = xs_ref[rows, :].astype(BF16)
            a = jnp.dot(x, wg_s[...], preferred_element_type=F32)
            u = jnp.dot(x, wu_s[...], preferred_element_type=F32)
            hid = (_silu(a) * u).astype(BF16)
            ys_ref[rows, :] = jnp.dot(hid, wd_s[...], preferred_element_type=F32)

        @pl.when(t >= nu_ref[0])
        def _():
            ys_ref[rows, :] = jnp.zeros((EXPERT_TILE, ys_ref.shape[1]), F32)

    for k in range(EXPERT_STEP_TILES):
        tile(pl.program_id(0) * EXPERT_STEP_TILES + k, pl.ds(k * EXPERT_TILE, EXPERT_TILE))


def _experts(plan, xs, w_gate, w_up, w_down):
    d = xs.shape[1]
    e, _, f = w_gate.shape
    tm = EXPERT_TILE * EXPERT_STEP_TILES
    n_tiles = plan["tile_expert"].shape[0]
    assert n_tiles % EXPERT_STEP_TILES == 0
    row = lambda i, te, nu, seg, nxt: (jnp.minimum(i, (nu[0] - 1) // EXPERT_STEP_TILES), 0)
    hbm = pl.BlockSpec(memory_space=pl.ANY)
    grid_spec = pltpu.PrefetchScalarGridSpec(
        num_scalar_prefetch=4,
        grid=(n_tiles // EXPERT_STEP_TILES,),
        in_specs=[pl.BlockSpec((tm, d), row), hbm, hbm, hbm],
        out_specs=pl.BlockSpec((tm, d), lambda i, te, nu, seg, nxt: (i, 0)),
        scratch_shapes=[pltpu.VMEM((2, d, f), F32), pltpu.VMEM((2, d, f), F32), pltpu.VMEM((2, f, d), F32),
                        pltpu.VMEM((d, f), BF16), pltpu.VMEM((d, f), BF16), pltpu.VMEM((f, d), BF16),
                        pltpu.SemaphoreType.DMA((2,))],
    )
    return pl.pallas_call(
        _experts_kernel,
        out_shape=jax.ShapeDtypeStruct((n_tiles * EXPERT_TILE, d), F32),
        grid_spec=grid_spec,
        compiler_params=_params(("arbitrary",)),
        name="experts",
    )(plan["tile_expert"], plan["n_used"], plan["segment"], plan["next_expert"], xs, w_gate, w_up, w_down)


def _combine_kernel(dest_ref, dnext_ref, ys_ref, x1_ref, route_ref, g2_ref, fw_ref, o_ref, y0_s, y1_s, sem):
    tm = x1_ref.shape[0]
    i = pl.program_id(0)
    slot = i % 2

    def gather(idx_ref, sl):
        def issue(g, _):
            for u in range(DMA_UNROLL):
                r = g * DMA_UNROLL + u
                _row_copy(ys_ref, idx_ref[0, r], y0_s.at[sl], r, sem.at[sl]).start()
                _row_copy(ys_ref, idx_ref[0, tm + r], y1_s.at[sl], r, sem.at[sl]).start()
            return 0

        lax.fori_loop(0, tm // DMA_UNROLL, issue, 0)

    @pl.when(i == 0)
    def _():
        gather(dest_ref, 0)

    @pl.when(i + 1 < pl.num_programs(0))
    def _():
        gather(dnext_ref, 1 - slot)

    def drain(g, _):
        for _u in range(DMA_UNROLL):
            _row_copy(ys_ref, 0, y0_s.at[slot], 0, sem.at[slot]).wait()
            _row_copy(ys_ref, 0, y1_s.at[slot], 0, sem.at[slot]).wait()
        return 0

    lax.fori_loop(0, tm // DMA_UNROLL, drain, 0)
    route = route_ref[...]
    w0 = route[:, 2:3]
    w1 = route[:, 3:4]
    x = x1_ref[...] + g2_ref[...] * (w0 * y0_s[slot] + w1 * y1_s[slot])
    o_ref[...] = _rms(x, RMS_EPS) * fw_ref[...]


def _combine(dest3, ys, x1, route, mod4, final_w, seq):
    n, d = x1.shape
    tm = dest3.shape[2] // 2
    tpb = seq // tm
    row = lambda i: (i, 0)
    steps = n // tm
    return pl.pallas_call(
        _combine_kernel,
        out_shape=jax.ShapeDtypeStruct((n, d), F32),
        grid=(steps,),
        in_specs=[pl.BlockSpec((None, 1, 2 * tm), lambda i: (i, 0, 0), memory_space=pltpu.SMEM),
                  pl.BlockSpec((None, 1, 2 * tm), lambda i: (jnp.minimum(i + 1, steps - 1), 0, 0),
                               memory_space=pltpu.SMEM),
                  pl.BlockSpec(memory_space=pl.ANY),
                  pl.BlockSpec((tm, d), row), pl.BlockSpec((tm, LANES), row),
                  pl.BlockSpec((None, None, 1, d), lambda i: (i // tpb, 5, 0, 0)),
                  pl.BlockSpec((1, d), lambda i: (0, 0))],
        out_specs=pl.BlockSpec((tm, d), row),
        scratch_shapes=[pltpu.VMEM((2, tm, d), F32), pltpu.VMEM((2, tm, d), F32),
                        pltpu.SemaphoreType.DMA((2,))],
        compiler_params=_params(("arbitrary",)),
        name="combine",
    )(dest3, dest3, ys, x1, route, mod4, final_w.reshape(1, d))


PLAN_LANES = 2 * LANES


def _plan_kernel(cnt_ref, route_ref, dest_ref, plan_ref):
    cnt = cnt_ref[...]
    tiles = jnp.floor((cnt + (EXPERT_TILE - 1.0)) * (1.0 / EXPERT_TILE))
    sq = (LANES, LANES)
    row = lax.broadcasted_iota(jnp.int32, sq, 0)
    col = lax.broadcasted_iota(jnp.int32, sq, 1)
    tiles_b = jnp.broadcast_to(tiles, sq)
    tiles_col = jnp.sum(jnp.where(col == row, tiles_b, 0.0), axis=1, keepdims=True)
    end_col = jnp.sum(jnp.where(col <= row, tiles_b, 0.0), axis=1, keepdims=True)
    end_row = jnp.sum(jnp.where(row <= col, jnp.broadcast_to(tiles_col, sq), 0.0), axis=0, keepdims=True)
    start_row = end_row - tiles

    route_t = route_ref[...].T
    blk = route_t.shape[1]
    sub = lax.broadcasted_iota(jnp.int32, (LANES, blk), 0)
    start8 = jnp.broadcast_to(start_row, (SUBLANES, LANES)).astype(BF16)
    dest_ref[...] = jnp.zeros_like(dest_ref)
    for k in range(2):
        e_sub = route_t[k:k + 1, :].astype(jnp.int32) + ROUTER_LANE0
        onehot = jnp.where(sub == e_sub, 1.0, 0.0).astype(BF16)
        start = jnp.dot(start8, onehot, preferred_element_type=F32)[0:1, :]
        dest_ref[k:k + 1, :] = (route_t[4 + k:5 + k, :] + EXPERT_TILE * start).astype(jnp.int32)

    @pl.when(pl.program_id(0) == 0)
    def _():
        wide = (LANES, PLAN_LANES)
        t = lax.broadcasted_iota(jnp.int32, wide, 1).astype(F32)
        e_id = lax.broadcasted_iota(jnp.int32, wide, 0) - ROUTER_LANE0
        is_e = (e_id >= 0) & (e_id < N_EXPERTS)
        n_used = jnp.max(end_row, axis=1, keepdims=True)
        t = jnp.minimum(t, n_used - 1.0)
        end_w = jnp.broadcast_to(end_col, wide)
        has_w = is_e & (jnp.broadcast_to(tiles_col, wide) > 0.0)
        te = jnp.sum(jnp.where(is_e & (end_w <= t), 1, 0), axis=0, keepdims=True)
        seg = jnp.sum(jnp.where(has_w & (e_id < te), 1, 0), axis=0, keepdims=True)
        nxt = jnp.min(jnp.where(has_w & (e_id > te), e_id, N_EXPERTS), axis=0, keepdims=True)
        nxt = jnp.where(nxt == N_EXPERTS, -1, nxt)
        pad_start = jnp.floor((start_row * EXPERT_TILE + cnt) * (1.0 / SUBLANES)) * SUBLANES
        pad_len = end_row * EXPERT_TILE - pad_start
        zeros = jnp.zeros((1, PLAN_LANES - LANES), jnp.int32)
        plan_ref[...] = jnp.zeros_like(plan_ref)
        plan_ref[0:1, :] = te
        plan_ref[1:2, :] = seg
        plan_ref[2:3, :] = nxt
        plan_ref[3:4, :] = jnp.concatenate([pad_start.astype(jnp.int32), zeros], axis=1)
        plan_ref[4:5, :] = jnp.concatenate([pad_len.astype(jnp.int32), zeros], axis=1)
        plan_ref[5:6, :] = jnp.broadcast_to(n_used, (1, PLAN_LANES)).astype(jnp.int32)


def _plan(route, counts_row, seq):
    n = route.shape[0]
    n_tiles = (2 * n) // EXPERT_TILE + N_EXPERTS
    assert n_tiles <= PLAN_LANES
    blk = min(n, 2048)
    dest, plan = pl.pallas_call(
        _plan_kernel,
        out_shape=(jax.ShapeDtypeStruct((SUBLANES, n), jnp.int32),
                   jax.ShapeDtypeStruct((SUBLANES, PLAN_LANES), jnp.int32)),
        grid=(n // blk,),
        in_specs=[pl.BlockSpec((1, LANES), lambda i: (0, 0)), pl.BlockSpec((blk, LANES), lambda i: (i, 0))],
        out_specs=(pl.BlockSpec((SUBLANES, blk), lambda i: (0, i)),
                   pl.BlockSpec((SUBLANES, PLAN_LANES), lambda i: (0, 0))),
        compiler_params=_params(("arbitrary",)),
        name="plan",
    )(counts_row, route)
    def per_step(tm):
        return jnp.concatenate([dest[0].reshape(n // tm, 1, tm), dest[1].reshape(n // tm, 1, tm)], axis=2)

    experts = slice(ROUTER_LANE0, ROUTER_LANE0 + N_EXPERTS)
    return {"dest3": per_step(min(DISPATCH_TILE, seq)), "dest3_combine": per_step(min(COMBINE_TILE, seq)),
            "tile_expert": plan[0, :n_tiles], "segment": plan[1, :n_tiles],
            "next_expert": plan[2, :n_tiles], "pad_start": plan[3, experts], "pad_len": plan[4, experts],
            "n_used": plan[5, :1]}


def kernel(x, c, positions, w_ada, b_ada, norm1_w, w_in, lambda_q1, lambda_k1, lambda_q2, lambda_k2, subln_w, hgrn_lb_logits, gnorm_w, w_out, norm2_w, w_group_router, b_group_router, w_expert_router, b_expert_router, w_gate, w_up, w_down, final_norm_w):
    b, s, d = x.shape
    n = b * s
    depth = w_ada.shape[0]
    assert depth == 1, "the combine kernel applies the final norm, so exactly one layer is supported"
    half = DIFF_HEAD_DIM // 2

    inv_freq = ROPE_THETA ** (-jnp.arange(half, dtype=F32) / half)
    pos = jnp.broadcast_to(positions.astype(F32).reshape(n, 1), (n, LANES))
    freq = jnp.tile(inv_freq, LANES // half).reshape(1, LANES)
    sign = jnp.tile(jnp.concatenate([-jnp.ones((half,), F32), jnp.ones((half,), F32)]),
                    LANES // DIFF_HEAD_DIM).reshape(1, LANES)

    xf = x.reshape(n, d)
    for l in range(depth):
        lambda_init = 0.8 - 0.6 * math.exp(-0.3 * l)
        mod4 = _mod(c, w_ada[l], b_ada[l]).reshape(b, 6, 1, d)
        q, k, v, hq, hf, hi, hg = _inproj(xf, mod4, norm1_w[l], w_in[l].astype(BF16), pos, freq, sign, s)
        to3 = lambda t: t.reshape(b, s, SEG)
        a = _attn(to3(q), to3(k), to3(v), lambda_q1[l], lambda_k1[l], lambda_q2[l], lambda_k2[l],
                  subln_w[l], lambda_init)
        r = _hgrn(to3(hq), to3(hf), to3(hi), to3(hg), hgrn_lb_logits, gnorm_w[l], l)

        w_router = jnp.zeros((d, LANES), F32)
        w_router = w_router.at[:, :N_GROUPS].set(w_group_router[l])
        w_router = w_router.at[:, ROUTER_LANE0:ROUTER_LANE0 + N_EXPERTS].set(w_expert_router[l])
        b_router = jnp.zeros((1, LANES), F32)
        b_router = b_router.at[0, :N_GROUPS].set(b_group_router[l])
        b_router = b_router.at[0, ROUTER_LANE0:ROUTER_LANE0 + N_EXPERTS].set(b_expert_router[l].reshape(-1))
        wr_hi = w_router.astype(BF16)
        wr_split = jnp.concatenate([wr_hi, (w_router - wr_hi.astype(F32)).astype(BF16)], axis=1)
        x1, h2, route, counts = _outproj(a.reshape(n, SEG), r.reshape(n, SEG), xf, mod4, norm2_w[l],
                                         w_out[l].astype(BF16), wr_split, b_router, s)

        plan = _plan(route, counts, s)
        xs = _dispatch(plan, h2)
        e_w = lambda w: w.reshape((N_EXPERTS,) + w.shape[2:])
        ys = _experts(plan, xs, e_w(w_gate[l]), e_w(w_up[l]), e_w(w_down[l]))
        xf = _combine(plan["dest3_combine"], ys, x1, route, mod4, final_norm_w, s)
    return xf.reshape(b, s, d)
```

```python
import functools
import math

import jax
import jax.numpy as jnp
from jax import lax
from jax.experimental import pallas as pl
from jax.experimental.pallas import tpu as pltpu

F32 = jnp.float32
BF16 = jnp.bfloat16
HIGHEST = lax.Precision.HIGHEST

N_DIFF_HEADS = 4
DIFF_HEAD_DIM = 64
DIFF_V_DIM = 2 * DIFF_HEAD_DIM
N_HGRN_HEADS = 4
HGRN_EXPAND = 128
HGRN_V_DIM = 128
ROPE_THETA = 10000.0
N_GROUPS = 4
EXPERTS_PER_GROUP = 8
N_EXPERTS = N_GROUPS * EXPERTS_PER_GROUP
RMS_EPS = 1e-6
SUBLN_EPS = 1e-5
SEG = 512
LANES = 128
SUBLANES = 8
ROUTER_LANE0 = N_GROUPS

ROW_TILE = 1024
DISPATCH_TILE = 2048
COMBINE_TILE = 512
OUTPROJ_TILE = 1024
OUTPROJ_CHAINS = 4
ATTN_TILE = 1024
ATTN_KV_TILE = 256
ATTN_ONES_ROWS = 16
LOG2E = math.log2(math.e)
HGRN_CHUNK = 64
HGRN_SUB = 16
HGRN_UNROLL = 32
EXPERT_TILE = 256
EXPERT_STEP_TILES = 4
PAD_PIECES = (256, 128, 64, 32, 16, 8)
DMA_UNROLL = 16
VMEM_LIMIT = 48 * 1024 * 1024


def _params(sem, **kw):
    return pltpu.CompilerParams(dimension_semantics=sem, vmem_limit_bytes=VMEM_LIMIT, **kw)


def _silu(x):
    return x * jax.nn.sigmoid(x)


def _rms(x, eps):
    return x * lax.rsqrt(jnp.mean(x * x, axis=-1, keepdims=True) + eps)


def _mod_kernel(c_ref, w_ref, b_ref, o_ref):
    ca = _silu(c_ref[...])
    o_ref[...] = jnp.dot(ca, w_ref[...], preferred_element_type=F32, precision=HIGHEST) + b_ref[...]


def _mod(c, w_ada, b_ada):
    bsz, d = c.shape
    n_out = w_ada.shape[1]
    return pl.pallas_call(
        _mod_kernel,
        out_shape=jax.ShapeDtypeStruct((bsz, n_out), F32),
        grid=(n_out // d,),
        in_specs=[pl.BlockSpec((bsz, d), lambda j: (0, 0)),
                  pl.BlockSpec((d, d), lambda j: (0, j)),
                  pl.BlockSpec((1, d), lambda j: (0, j))],
        out_specs=pl.BlockSpec((bsz, d), lambda j: (0, j)),
        compiler_params=_params(("arbitrary",)),
        name="mod",
    )(c, w_ada, b_ada.reshape(1, n_out))


def _inproj_kernel(x_ref, sc_ref, sh_ref, nw_ref, w_ref, pos_ref, freq_ref, sign_ref,
                   q_ref, k_ref, v_ref, hq_ref, hf_ref, hi_ref, hg_ref):
    h = _rms(x_ref[...], RMS_EPS) * nw_ref[...]
    h = h * (1.0 + sc_ref[...]) + sh_ref[...]
    hb = h.astype(BF16)
    ang = pos_ref[...] * freq_ref[...]
    cos = jnp.cos(ang)
    sin = jnp.sin(ang) * sign_ref[...]
    lane = lax.broadcasted_iota(jnp.int32, cos.shape, 1)
    first_half = (lane % DIFF_HEAD_DIM) < (DIFF_HEAD_DIM // 2)

    def seg(i):
        return jnp.dot(hb, w_ref[:, i * SEG:(i + 1) * SEG], preferred_element_type=F32)

    def rope(p, scale):
        outs = []
        for c in range(SEG // LANES):
            pc = p[:, c * LANES:(c + 1) * LANES]
            partner = jnp.where(first_half, pltpu.roll(pc, LANES - DIFF_HEAD_DIM // 2, 1),
                                pltpu.roll(pc, DIFF_HEAD_DIM // 2, 1))
            outs.append((pc * cos + partner * sin) * scale)
        return jnp.concatenate(outs, axis=1)

    q_ref[...] = rope(seg(0), DIFF_HEAD_DIM ** -0.5 * LOG2E).astype(BF16)
    k_ref[...] = rope(seg(1), 1.0).astype(BF16)
    v_ref[...] = seg(2).astype(BF16)
    hq_ref[...] = seg(3).astype(BF16)
    hf_ref[...] = seg(4)
    hi_ref[...] = seg(5).astype(BF16)
    hg_ref[...] = seg(6).astype(BF16)


def _inproj(x2, mod4, norm_w, w_in_bf, pos, freq, sign, seq):
    n, d = x2.shape
    tm = min(ROW_TILE, seq)
    tpb = seq // tm
    row = lambda i: (i, 0)
    modspec = lambda j: pl.BlockSpec((None, None, 1, d), lambda i: (i // tpb, j, 0, 0))
    seg_spec = pl.BlockSpec((tm, SEG), row)
    bf = jax.ShapeDtypeStruct((n, SEG), BF16)
    return pl.pallas_call(
        _inproj_kernel,
        out_shape=(bf, bf, bf, bf, jax.ShapeDtypeStruct((n, SEG), F32), bf, bf),
        grid=(n // tm,),
        in_specs=[pl.BlockSpec((tm, d), row), modspec(1), modspec(0),
                  pl.BlockSpec((1, d), lambda i: (0, 0)),
                  pl.BlockSpec(w_in_bf.shape, lambda i: (0, 0)),
                  pl.BlockSpec((tm, 1), row),
                  pl.BlockSpec((1, LANES), lambda i: (0, 0)), pl.BlockSpec((1, LANES), lambda i: (0, 0))],
        out_specs=(seg_spec,) * 7,
        compiler_params=_params(("arbitrary",)),
        name="inproj",
    )(x2, mod4, mod4, norm_w.reshape(1, d), w_in_bf, pos, freq, sign)


def _attn_kernel(q_ref, k_ref, v_ref, lq1_ref, lk1_ref, lq2_ref, lk2_ref, sw_ref, o_ref,
                 vt_s, acc_s, *, lambda_init):
    tq = q_ref.shape[0]
    seq = k_ref.shape[0]
    tk = min(ATTN_KV_TILE, tq)
    vd = DIFF_V_DIM
    qi = pl.program_id(1)
    nt = (((1,), (1,)), ((), ()))
    chains = [(h, mp) for h in range(N_DIFF_HEADS) for mp in range(2)]

    va = vd + ATTN_ONES_ROWS

    @pl.when(qi == 0)
    def _():
        for h in range(N_DIFF_HEADS):
            vt_s[h * va + vd:(h + 1) * va, :] = jnp.ones((ATTN_ONES_ROWS, seq), BF16)
            for c in range(seq // tk):
                vt_s[h * va:h * va + vd, c * tk:(c + 1) * tk] = (
                    v_ref[c * tk:(c + 1) * tk, h * vd:(h + 1) * vd].astype(F32).T.astype(BF16))

    acc_s[...] = jnp.zeros_like(acc_s)
    lane = lax.broadcasted_iota(jnp.int32, (tq, vd), 1)
    qz = []
    for h, mp in chains:
        qh = q_ref[:, h * vd:(h + 1) * vd]
        keep = (lane < DIFF_HEAD_DIM) if mp == 0 else (lane >= DIFF_HEAD_DIM)
        qz.append(jnp.where(keep, qh, jnp.zeros_like(qh)))

    def step(j, ms, c0):
        row0 = pl.multiple_of(j * tk, tk)
        q0 = c0 or 0
        scores = [lax.dot_general(k_ref[pl.ds(row0, tk), h * vd:(h + 1) * vd], qz[c][q0:], nt,
                                  preferred_element_type=F32) for c, (h, _) in enumerate(chains)]
        new_ms, alphas, probs = [], [], []
        for c, s in enumerate(scores):
            if c0 is not None:
                kidx = lax.broadcasted_iota(jnp.int32, s.shape, 0)
                qidx = lax.broadcasted_iota(jnp.int32, s.shape, 1)
                s = jnp.where(kidx <= qidx, s, -jnp.inf)
            m_old = ms[c][:, q0:]
            m_new = jnp.maximum(m_old, jnp.max(s, axis=0, keepdims=True))
            alphas.append(jnp.exp2(m_old - m_new))
            probs.append(jnp.exp2(s - m_new).astype(BF16))
            new_ms.append(m_new if q0 == 0 else jnp.concatenate([ms[c][:, :q0], m_new], axis=1))
        pvs = [jnp.dot(vt_s[h * va:(h + 1) * va, pl.ds(row0, tk)], probs[c], preferred_element_type=F32)
               for c, (h, _) in enumerate(chains)]
        for c, pv in enumerate(pvs):
            acc_s[c, :, q0:] = alphas[c] * acc_s[c, :, q0:] + pv
        return tuple(new_ms)

    init = tuple(jnp.full((1, tq), -jnp.inf, F32) for _ in chains)
    per_q = tq // tk
    ms = lax.fori_loop(0, qi * per_q, lambda j, c: step(j, c, None), init)
    for d in range(per_q):
        ms = step(qi * per_q + d, ms, d * tk)

    lam = (jnp.exp(jnp.sum(lq1_ref[...] * lk1_ref[...], axis=-1, keepdims=True))
           - jnp.exp(jnp.sum(lq2_ref[...] * lk2_ref[...], axis=-1, keepdims=True)) + lambda_init)
    sw = jnp.concatenate([sw_ref[...]] * (tq // LANES), axis=1)
    for h in range(N_DIFF_HEADS):
        inv1 = 1.0 / acc_s[2 * h, vd:vd + 1, :]
        inv2 = 1.0 / acc_s[2 * h + 1, vd:vd + 1, :]
        o = acc_s[2 * h, :vd, :] * inv1 - lam * (acc_s[2 * h + 1, :vd, :] * inv2)
        o = o * lax.rsqrt(jnp.mean(o * o, axis=0, keepdims=True) + SUBLN_EPS)
        o_ref[:, h * vd:(h + 1) * vd] = (o * sw * (1.0 - lambda_init)).T.astype(BF16)


def _attn(q, k, v, lq1, lk1, lq2, lk2, subln_w, lambda_init):
    b, s, width = q.shape
    tq = min(ATTN_TILE, s)
    vec = lambda a: a.reshape(1, -1)
    small = lambda w: pl.BlockSpec((1, w), lambda bi, i: (0, 0))
    kv_spec = pl.BlockSpec((None, s, width), lambda bi, i: (bi, 0, 0))
    q_spec = pl.BlockSpec((None, tq, width), lambda bi, i: (bi, i, 0))
    sw_cols = jnp.broadcast_to(subln_w.reshape(DIFF_V_DIM, 1), (DIFF_V_DIM, LANES))
    return pl.pallas_call(
        functools.partial(_attn_kernel, lambda_init=lambda_init),
        out_shape=jax.ShapeDtypeStruct(q.shape, BF16),
        grid=(b, s // tq),
        in_specs=[q_spec, kv_spec, kv_spec, small(DIFF_HEAD_DIM), small(DIFF_HEAD_DIM),
                  small(DIFF_HEAD_DIM), small(DIFF_HEAD_DIM),
                  pl.BlockSpec((DIFF_V_DIM, LANES), lambda bi, i: (0, 0))],
        out_specs=q_spec,
        scratch_shapes=[pltpu.VMEM((N_DIFF_HEADS * (DIFF_V_DIM + ATTN_ONES_ROWS), s), BF16),
                        pltpu.VMEM((2 * N_DIFF_HEADS, DIFF_V_DIM + ATTN_ONES_ROWS, tq), F32)],
        compiler_params=_params(("arbitrary",) * 2),
        name="attn",
    )(q, k, v, vec(lq1), vec(lk1), vec(lq2), vec(lk2), sw_cols)


def _hgrn_kernel(q_ref, f_ref, i_ref, g_ref, lbl_ref, gw_ref, o_ref, qd_s, oi_s, u_s, d_s, *, layer):
    seq = q_ref.shape[0]
    ck, sub = HGRN_CHUNK, HGRN_SUB
    lg = lbl_ref[...]
    e = jnp.exp(lg - jnp.max(lg, axis=0, keepdims=True))
    lb = jnp.sum(e[:layer + 1], axis=0, keepdims=True) / jnp.sum(e, axis=0, keepdims=True)
    gw = gw_ref[...]
    causal = (lax.broadcasted_iota(jnp.int32, (ck, ck), 1)
              <= lax.broadcasted_iota(jnp.int32, (ck, ck), 0))
    tril = causal.astype(BF16)
    nt = (((1,), (1,)), ((), ()))
    tn = (((0,), (0,)), ((), ()))
    w = HGRN_EXPAND
    nsub = ck // sub
    unroll = min(HGRN_UNROLL, seq // ck)

    def phase1(c2, _):
        chunks = [c2 * unroll + u for u in range(unroll)]
        rows = [pl.ds(pl.multiple_of(c * ck, ck), ck) for c in chunks]
        kins, splits = [], []
        for r in rows:
            forget = lb + (1.0 - lb) * jax.nn.sigmoid(f_ref[r, :])
            kins.append(1.0 - forget)
            g = jnp.log2(forget)
            g_hi = g.astype(BF16)
            r1 = g - g_hi.astype(F32)
            g_mid = r1.astype(BF16)
            g_lo = (r1 - g_mid.astype(F32)).astype(BF16)
            splits.append(jnp.concatenate([g_hi, g_mid, g_lo], axis=1))
        sums = [jnp.dot(tril, x, preferred_element_type=F32) for x in splits]

        q_blk, k_blk, kdecs = [], [], []
        for c, r, kin, cs in zip(chunks, rows, kins, sums):
            bsum = cs[:, :w] + cs[:, w:2 * w] + cs[:, 2 * w:]
            qf = _silu(q_ref[r, :].astype(F32))
            qd_s[r, :] = (qf * jnp.exp2(bsum)).astype(BF16)
            q_rows, k_cols = [], []
            for i in range(nsub):
                lo, hi = i * sub, (i + 1) * sub
                ref_pt = bsum[lo:lo + 1, :]
                qh = (qf[lo:hi] * jnp.exp2(bsum[lo:hi] - ref_pt)).astype(BF16)
                kh = (kin[:hi] * jnp.exp2(ref_pt - bsum[:hi])).astype(BF16)
                pieces = [qh if j == i else jnp.zeros((sub, w), BF16) for j in range(nsub)]
                q_rows.append(jnp.concatenate(pieces, axis=1))
                k_cols.append(kh if hi == ck else jnp.concatenate([kh, jnp.zeros((ck - hi, w), BF16)], axis=0))
            q_blk.append(jnp.concatenate(q_rows, axis=0))
            k_blk.append(jnp.concatenate(k_cols, axis=1))
            last = bsum[ck - 1:ck, :]
            kdecs.append((kin * jnp.exp2(last - bsum)).astype(BF16))
            d_s[c] = jnp.exp2(last)
        vbs = [i_ref[r, :] for r in rows]
        atts = [lax.dot_general(qb, kb, nt, preferred_element_type=F32) for qb, kb in zip(q_blk, k_blk)]
        for c, vb, kdec in zip(chunks, vbs, kdecs):
            u_s[c] = lax.dot_general(vb, kdec, tn, preferred_element_type=F32)
        probs = [jnp.where(causal, att, 0.0).astype(BF16) for att in atts]
        for r, p, vb in zip(rows, probs, vbs):
            oi_s[r, :] = jnp.dot(p, vb, preferred_element_type=F32)
        return 0

    lax.fori_loop(0, seq // (ck * unroll), phase1, 0)

    def phase2(c2, state_t):
        for u in range(unroll):
            c = c2 * unroll + u
            rows = pl.ds(pl.multiple_of(c * ck, ck), ck)
            o = oi_s[rows, :] + lax.dot_general(qd_s[rows, :], state_t.astype(BF16), nt,
                                                preferred_element_type=F32)
            o = _rms(o, RMS_EPS) * gw * _silu(g_ref[rows, :].astype(F32))
            o_ref[rows, :] = o.astype(BF16)
            state_t = state_t * d_s[c] + u_s[c]
        return state_t

    lax.fori_loop(0, seq // (ck * unroll), phase2, jnp.zeros((HGRN_V_DIM, HGRN_EXPAND), F32))


def _hgrn(hq, hf, hi, hg, lb_logits, gnorm_w, layer):
    b, s, _ = hq.shape
    w = HGRN_EXPAND
    blk = pl.BlockSpec((None, s, w), lambda bi, h: (bi, 0, h))
    return pl.pallas_call(
        functools.partial(_hgrn_kernel, layer=layer),
        out_shape=jax.ShapeDtypeStruct(hq.shape, BF16),
        grid=(b, N_HGRN_HEADS),
        in_specs=[blk, blk, blk, blk,
                  pl.BlockSpec((lb_logits.shape[0], w), lambda bi, h: (0, h)),
                  pl.BlockSpec((1, w), lambda bi, h: (0, 0))],
        out_specs=blk,
        scratch_shapes=[pltpu.VMEM((s, w), BF16), pltpu.VMEM((s, HGRN_V_DIM), F32),
                        pltpu.VMEM((s // HGRN_CHUNK, HGRN_V_DIM, w), F32),
                        pltpu.VMEM((s // HGRN_CHUNK, 1, w), F32)],
        compiler_params=_params(("arbitrary",) * 2),
        name="hgrn",
    )(hq, hf, hi, hg, lb_logits, gnorm_w.reshape(1, w))


def _route(logits):
    lane = lax.broadcasted_iota(jnp.int32, logits.shape, 1)
    big = jnp.int32(LANES)
    neg = -jnp.inf
    is_g = lane < N_GROUPS
    gl = jnp.where(is_g, logits, neg)
    gmax = jnp.max(gl, axis=-1, keepdims=True)
    gsel = jnp.min(jnp.where(gl == gmax, lane, big), axis=-1, keepdims=True)
    pg = 1.0 / jnp.sum(jnp.where(is_g, jnp.exp(logits - gmax), 0.0), axis=-1, keepdims=True)
    lo = ROUTER_LANE0 + EXPERTS_PER_GROUP * gsel
    el = jnp.where((lane >= lo) & (lane < lo + EXPERTS_PER_GROUP), logits, neg)
    v0 = jnp.max(el, axis=-1, keepdims=True)
    i0 = jnp.min(jnp.where(el == v0, lane, big), axis=-1, keepdims=True)
    el = jnp.where(lane == i0, neg, el)
    v1 = jnp.max(el, axis=-1, keepdims=True)
    i1 = jnp.min(jnp.where(el == v1, lane, big), axis=-1, keepdims=True)
    t = jnp.exp(v1 - v0)
    return lane, i0, i1, pg / (1.0 + t), pg * t / (1.0 + t)


def _outproj_kernel(a_ref, r_ref, x_ref, g1_ref, sc_ref, sh_ref, nw_ref, wo_ref, wr_ref, br_ref,
                    x1_ref, h2_ref, route_ref, cnt_ref, run_ref):
    i = pl.program_id(0)
    sub = x_ref.shape[0] // OUTPROJ_CHAINS
    half = a_ref.shape[1]
    blocks = [pl.ds(u * sub, sub) for u in range(OUTPROJ_CHAINS)]

    @pl.when(i == 0)
    def _():
        run_ref[...] = jnp.zeros_like(run_ref)

    mixes = [jnp.dot(a_ref[rows, :], wo_ref[:half, :], preferred_element_type=F32)
             + jnp.dot(r_ref[rows, :], wo_ref[half:, :], preferred_element_type=F32) for rows in blocks]
    splits = []
    for rows, mix in zip(blocks, mixes):
        x1 = x_ref[rows, :] + g1_ref[...] * mix
        x1_ref[rows, :] = x1
        h2 = _rms(x1, RMS_EPS) * nw_ref[...]
        h2 = h2 * (1.0 + sc_ref[...]) + sh_ref[...]
        h2_ref[rows, :] = h2
        h_hi = h2.astype(BF16)
        splits.append((h_hi, (h2 - h_hi.astype(F32)).astype(BF16)))
    parts = [jnp.dot(h_hi, wr_ref[...], preferred_element_type=F32)
             + jnp.dot(h_lo, wr_ref[...], preferred_element_type=F32) for h_hi, h_lo in splits]
    routed = [_route(part[:, :LANES] + part[:, LANES:] + br_ref[...]) for part in parts]
    before = jnp.where(lax.broadcasted_iota(jnp.int32, (sub, sub), 1)
                       < lax.broadcasted_iota(jnp.int32, (sub, sub), 0), 1.0, 0.0).astype(BF16)
    onehots = [jnp.where((lane == i0) | (lane == i1), 1.0, 0.0) for lane, i0, i1, _, _ in routed]
    prefixes = [jnp.dot(before, oh.astype(BF16), preferred_element_type=F32) for oh in onehots]
    run = run_ref[...]
    for rows, (lane, i0, i1, w0, w1), onehot, prefix in zip(blocks, routed, onehots, prefixes):
        prefix = prefix + run
        rank0 = jnp.sum(jnp.where(lane == i0, prefix, 0.0), axis=-1, keepdims=True)
        rank1 = jnp.sum(jnp.where(lane == i1, prefix, 0.0), axis=-1, keepdims=True)
        run = run + jnp.sum(onehot, axis=0, keepdims=True)
        e0 = (i0 - ROUTER_LANE0).astype(F32)
        e1 = (i1 - ROUTER_LANE0).astype(F32)
        slab = jnp.zeros(lane.shape, F32)
        for ln, val in enumerate((e0, e1, w0, w1, rank0, rank1)):
            slab = jnp.where(lane == ln, val, slab)
        route_ref[rows, :] = slab
    run_ref[...] = run
    cnt_ref[...] = run


def _outproj(a2, r2, x2, mod4, norm_w, w_out_bf, w_router, b_router, seq):
    n, d = x2.shape
    tm = min(OUTPROJ_TILE, seq)
    tpb = seq // tm
    row = lambda i: (i, 0)
    fixed = lambda i: (0, 0)
    modspec = lambda j: pl.BlockSpec((None, None, 1, d), lambda i: (i // tpb, j, 0, 0))
    return pl.pallas_call(
        _outproj_kernel,
        out_shape=(jax.ShapeDtypeStruct((n, d), F32), jax.ShapeDtypeStruct((n, d), F32),
                   jax.ShapeDtypeStruct((n, LANES), F32), jax.ShapeDtypeStruct((1, LANES), F32)),
        grid=(n // tm,),
        in_specs=[pl.BlockSpec((tm, a2.shape[1]), row), pl.BlockSpec((tm, r2.shape[1]), row),
                  pl.BlockSpec((tm, d), row), modspec(2), modspec(4), modspec(3),
                  pl.BlockSpec((1, d), fixed), pl.BlockSpec(w_out_bf.shape, fixed),
                  pl.BlockSpec(w_router.shape, fixed), pl.BlockSpec((1, LANES), fixed)],
        out_specs=(pl.BlockSpec((tm, d), row), pl.BlockSpec((tm, d), row),
                   pl.BlockSpec((tm, LANES), row), pl.BlockSpec((1, LANES), fixed)),
        scratch_shapes=[pltpu.VMEM((1, LANES), F32)],
        compiler_params=_params(("arbitrary",)),
        name="outproj",
    )(a2, r2, x2, mod4, mod4, mod4, norm_w.reshape(1, d), w_out_bf, w_router, b_router)


def _row_copy(src_ref, src_row, dst_ref, dst_row, sem):
    return pltpu.make_async_copy(src_ref.at[pl.ds(src_row, 1), :], dst_ref.at[pl.ds(dst_row, 1), :], sem)


def _dispatch_kernel(pad_ref, len_ref, nu_ref, dest_ref, h_ref, xs_ref, zero_s, sem):
    tm = h_ref.shape[0]

    @pl.when(pl.program_id(0) == 0)
    def _():
        zero_s[...] = jnp.zeros_like(zero_s)
        tails = []
        for e in range(N_EXPERTS):
            pos = pad_ref[e]
            for piece in PAD_PIECES:
                hit = (len_ref[e] & piece) != 0
                tails.append((hit, pltpu.make_async_copy(
                    zero_s.at[pl.ds(0, piece), :],
                    xs_ref.at[pl.ds(pl.multiple_of(pos, SUBLANES), piece), :], sem)))
                pos = pos + jnp.where(hit, piece, 0)
        n_tiles = xs_ref.shape[0] // EXPERT_TILE
        for t in range(N_EXPERTS):
            tile = nu_ref[0] + t
            row0 = pl.multiple_of(jnp.minimum(tile, n_tiles - 1) * EXPERT_TILE, EXPERT_TILE)
            tails.append((tile < n_tiles, pltpu.make_async_copy(
                zero_s, xs_ref.at[pl.ds(row0, EXPERT_TILE), :], sem)))
        for hit, cp in tails:
            pl.when(hit)(cp.start)
        for hit, cp in tails:
            pl.when(hit)(cp.wait)

    def issue(g, _):
        for u in range(DMA_UNROLL):
            r = g * DMA_UNROLL + u
            _row_copy(h_ref, r, xs_ref, dest_ref[0, r], sem).start()
            _row_copy(h_ref, r, xs_ref, dest_ref[0, tm + r], sem).start()
        return 0

    lax.fori_loop(0, tm // DMA_UNROLL, issue, 0)

    def drain(g, _):
        for _u in range(2 * DMA_UNROLL):
            _row_copy(h_ref, 0, xs_ref, 0, sem).wait()
        return 0

    lax.fori_loop(0, tm // DMA_UNROLL, drain, 0)


def _dispatch(plan, h2):
    n, d = h2.shape
    dest3 = plan["dest3"]
    tm = dest3.shape[2] // 2
    rows = plan["tile_expert"].shape[0] * EXPERT_TILE
    grid_spec = pltpu.PrefetchScalarGridSpec(
        num_scalar_prefetch=3,
        grid=(n // tm,),
        in_specs=[pl.BlockSpec((None, 1, 2 * tm), lambda i, pad, ln, nu: (i, 0, 0), memory_space=pltpu.SMEM),
                  pl.BlockSpec((tm, d), lambda i, pad, ln, nu: (i, 0))],
        out_specs=pl.BlockSpec(memory_space=pl.ANY),
        scratch_shapes=[pltpu.VMEM((EXPERT_TILE, d), F32), pltpu.SemaphoreType.DMA],
    )
    return pl.pallas_call(
        _dispatch_kernel,
        out_shape=jax.ShapeDtypeStruct((rows, d), F32),
        grid_spec=grid_spec,
        compiler_params=_params(("arbitrary",), has_side_effects=True),
        name="dispatch",
    )(plan["pad_start"], plan["pad_len"], plan["n_used"], dest3, h2)


def _experts_kernel(te_ref, nu_ref, seg_ref, nxt_ref, xs_ref, wg_hbm, wu_hbm, wd_hbm, ys_ref,
                    wg_f, wu_f, wd_f, wg_s, wu_s, wd_s, sem):
    def fetch(e, sl):
        return [pltpu.make_async_copy(src.at[e], dst.at[sl], sem.at[sl])
                for src, dst in ((wg_hbm, wg_f), (wu_hbm, wu_f), (wd_hbm, wd_f))]

    def tile(t, rows):
        @pl.when(t < nu_ref[0])
        def _():
            prev = te_ref[jnp.maximum(t - 1, 0)]

            @pl.when(t == 0)
            def _():
                for cp in fetch(te_ref[0], 0):
                    cp.start(priority=1)

            @pl.when((t == 0) | (te_ref[t] != prev))
            def _():
                sl = seg_ref[t] % 2
                for cp in fetch(te_ref[t], sl):
                    cp.wait()

                @pl.when(nxt_ref[t] >= 0)
                def _():
                    for cp in fetch(nxt_ref[t], 1 - sl):
                        cp.start(priority=1)

                wg_s[...] = wg_f[sl].astype(BF16)
                wu_s[...] = wu_f[sl].astype(BF16)
                wd_s[...] = wd_f[sl].astype(BF16)

            x = xs_ref[rows, :].astype(BF16)
            a = jnp.dot(x, wg_s[...], preferred_element_type=F32)
            u = jnp.dot(x, wu_s[...], preferred_element_type=F32)
            hid = (_silu(a) * u).astype(BF16)
            ys_ref[rows, :] = jnp.dot(hid, wd_s[...], preferred_element_type=F32)

        @pl.when(t >= nu_ref[0])
        def _():
            ys_ref[rows, :] = jnp.zeros((EXPERT_TILE, ys_ref.shape[1]), F32)

    for k in range(EXPERT_STEP_TILES):
        tile(pl.program_id(0) * EXPERT_STEP_TILES + k, pl.ds(k * EXPERT_TILE, EXPERT_TILE))


def _experts(plan, xs, w_gate, w_up, w_down):
    d = xs.shape[1]
    e, _, f = w_gate.shape
    tm = EXPERT_TILE * EXPERT_STEP_TILES
    n_tiles = plan["tile_expert"].shape[0]
    assert n_tiles % EXPERT_STEP_TILES == 0
    row = lambda i, te, nu, seg, nxt: (jnp.minimum(i, (nu[0] - 1) // EXPERT_STEP_TILES), 0)
    hbm = pl.BlockSpec(memory_space=pl.ANY)
    grid_spec = pltpu.PrefetchScalarGridSpec(
        num_scalar_prefetch=4,
        grid=(n_tiles // EXPERT_STEP_TILES,),
        in_specs=[pl.BlockSpec((tm, d), row), hbm, hbm, hbm],
        out_specs=pl.BlockSpec((tm, d), lambda i, te, nu, seg, nxt: (i, 0)),
        scratch_shapes=[pltpu.VMEM((2, d, f), F32), pltpu.VMEM((2, d, f), F32), pltpu.VMEM((2, f, d), F32),
                        pltpu.VMEM((d, f), BF16), pltpu.VMEM((d, f), BF16), pltpu.VMEM((f, d), BF16),
                        pltpu.SemaphoreType.DMA((2,))],
    )
    return pl.pallas_call(
        _experts_kernel,
        out_shape=jax.ShapeDtypeStruct((n_tiles * EXPERT_TILE, d), F32),
        grid_spec=grid_spec,
        compiler_params=_params(("arbitrary",)),
        name="experts",
    )(plan["tile_expert"], plan["n_used"], plan["segment"], plan["next_expert"], xs, w_gate, w_up, w_down)


def _combine_kernel(dest_ref, dnext_ref, ys_ref, x1_ref, route_ref, g2_ref, fw_ref, o_ref, y0_s, y1_s, sem):
    tm = x1_ref.shape[0]
    i = pl.program_id(0)
    slot = i % 2

    def gather(idx_ref, sl):
        def issue(g, _):
            for u in range(DMA_UNROLL):
                r = g * DMA_UNROLL + u
                _row_copy(ys_ref, idx_ref[0, r], y0_s.at[sl], r, sem.at[sl]).start()
                _row_copy(ys_ref, idx_ref[0, tm + r], y1_s.at[sl], r, sem.at[sl]).start()
            return 0

        lax.fori_loop(0, tm // DMA_UNROLL, issue, 0)

    @pl.when(i == 0)
    def _():
        gather(dest_ref, 0)

    @pl.when(i + 1 < pl.num_programs(0))
    def _():
        gather(dnext_ref, 1 - slot)

    def drain(g, _):
        for _u in range(DMA_UNROLL):
            _row_copy(ys_ref, 0, y0_s.at[slot], 0, sem.at[slot]).wait()
            _row_copy(ys_ref, 0, y1_s.at[slot], 0, sem.at[slot]).wait()
        return 0

    lax.fori_loop(0, tm // DMA_UNROLL, drain, 0)
    route = route_ref[...]
    w0 = route[:, 2:3]
    w1 = route[:, 3:4]
    x = x1_ref[...] + g2_ref[...] * (w0 * y0_s[slot] + w1 * y1_s[slot])
    o_ref[...] = _rms(x, RMS_EPS) * fw_ref[...]


def _combine(dest3, ys, x1, route, mod4, final_w, seq):
    n, d = x1.shape
    tm = dest3.shape[2] // 2
    tpb = seq // tm
    row = lambda i: (i, 0)
    steps = n // tm
    return pl.pallas_call(
        _combine_kernel,
        out_shape=jax.ShapeDtypeStruct((n, d), F32),
        grid=(steps,),
        in_specs=[pl.BlockSpec((None, 1, 2 * tm), lambda i: (i, 0, 0), memory_space=pltpu.SMEM),
                  pl.BlockSpec((None, 1, 2 * tm), lambda i: (jnp.minimum(i + 1, steps - 1), 0, 0),
                               memory_space=pltpu.SMEM),
                  pl.BlockSpec(memory_space=pl.ANY),
                  pl.BlockSpec((tm, d), row), pl.BlockSpec((tm, LANES), row),
                  pl.BlockSpec((None, None, 1, d), lambda i: (i // tpb, 5, 0, 0)),
                  pl.BlockSpec((1, d), lambda i: (0, 0))],
        out_specs=pl.BlockSpec((tm, d), row),
        scratch_shapes=[pltpu.VMEM((2, tm, d), F32), pltpu.VMEM((2, tm, d), F32),
                        pltpu.SemaphoreType.DMA((2,))],
        compiler_params=_params(("arbitrary",)),
        name="combine",
    )(dest3, dest3, ys, x1, route, mod4, final_w.reshape(1, d))


PLAN_LANES = 2 * LANES


def _plan_kernel(cnt_ref, route_ref, dest_ref, plan_ref):
    cnt = cnt_ref[...]
    tiles = jnp.floor((cnt + (EXPERT_TILE - 1.0)) * (1.0 / EXPERT_TILE))
    sq = (LANES, LANES)
    row = lax.broadcasted_iota(jnp.int32, sq, 0)
    col = lax.broadcasted_iota(jnp.int32, sq, 1)
    tiles_b = jnp.broadcast_to(tiles, sq)
    tiles_col = jnp.sum(jnp.where(col == row, tiles_b, 0.0), axis=1, keepdims=True)
    end_col = jnp.sum(jnp.where(col <= row, tiles_b, 0.0), axis=1, keepdims=True)
    end_row = jnp.sum(jnp.where(row <= col, jnp.broadcast_to(tiles_col, sq), 0.0), axis=0, keepdims=True)
    start_row = end_row - tiles

    route_t = route_ref[...].T
    blk = route_t.shape[1]
    sub = lax.broadcasted_iota(jnp.int32, (LANES, blk), 0)
    start8 = jnp.broadcast_to(start_row, (SUBLANES, LANES)).astype(BF16)
    dest_ref[...] = jnp.zeros_like(dest_ref)
    for k in range(2):
        e_sub = route_t[k:k + 1, :].astype(jnp.int32) + ROUTER_LANE0
        onehot = jnp.where(sub == e_sub, 1.0, 0.0).astype(BF16)
        start = jnp.dot(start8, onehot, preferred_element_type=F32)[0:1, :]
        dest_ref[k:k + 1, :] = (route_t[4 + k:5 + k, :] + EXPERT_TILE * start).astype(jnp.int32)

    @pl.when(pl.program_id(0) == 0)
    def _():
        wide = (LANES, PLAN_LANES)
        t = lax.broadcasted_iota(jnp.int32, wide, 1).astype(F32)
        e_id = lax.broadcasted_iota(jnp.int32, wide, 0) - ROUTER_LANE0
        is_e = (e_id >= 0) & (e_id < N_EXPERTS)
        n_used = jnp.max(end_row, axis=1, keepdims=True)
        t = jnp.minimum(t, n_used - 1.0)
        end_w = jnp.broadcast_to(end_col, wide)
        has_w = is_e & (jnp.broadcast_to(tiles_col, wide) > 0.0)
        te = jnp.sum(jnp.where(is_e & (end_w <= t), 1, 0), axis=0, keepdims=True)
        seg = jnp.sum(jnp.where(has_w & (e_id < te), 1, 0), axis=0, keepdims=True)
        nxt = jnp.min(jnp.where(has_w & (e_id > te), e_id, N_EXPERTS), axis=0, keepdims=True)
        nxt = jnp.where(nxt == N_EXPERTS, -1, nxt)
        pad_start = jnp.floor((start_row * EXPERT_TILE + cnt) * (1.0 / SUBLANES)) * SUBLANES
        pad_len = end_row * EXPERT_TILE - pad_start
        zeros = jnp.zeros((1, PLAN_LANES - LANES), jnp.int32)
        plan_ref[...] = jnp.zeros_like(plan_ref)
        plan_ref[0:1, :] = te
        plan_ref[1:2, :] = seg
        plan_ref[2:3, :] = nxt
        plan_ref[3:4, :] = jnp.concatenate([pad_start.astype(jnp.int32), zeros], axis=1)
        plan_ref[4:5, :] = jnp.concatenate([pad_len.astype(jnp.int32), zeros], axis=1)
        plan_ref[5:6, :] = jnp.broadcast_to(n_used, (1, PLAN_LANES)).astype(jnp.int32)


def _plan(route, counts_row, seq):
    n = route.shape[0]
    n_tiles = (2 * n) // EXPERT_TILE + N_EXPERTS
    assert n_tiles <= PLAN_LANES
    blk = min(n, 2048)
    dest, plan = pl.pallas_call(
        _plan_kernel,
        out_shape=(jax.ShapeDtypeStruct((SUBLANES, n), jnp.int32),
                   jax.ShapeDtypeStruct((SUBLANES, PLAN_LANES), jnp.int32)),
        grid=(n // blk,),
        in_specs=[pl.BlockSpec((1, LANES), lambda i: (0, 0)), pl.BlockSpec((blk, LANES), lambda i: (i, 0))],
        out_specs=(pl.BlockSpec((SUBLANES, blk), lambda i: (0, i)),
                   pl.BlockSpec((SUBLANES, PLAN_LANES), lambda i: (0, 0))),
        compiler_params=_params(("arbitrary",)),
        name="plan",
    )(counts_row, route)
    def per_step(tm):
        return jnp.concatenate([dest[0].reshape(n // tm, 1, tm), dest[1].reshape(n // tm, 1, tm)], axis=2)

    experts = slice(ROUTER_LANE0, ROUTER_LANE0 + N_EXPERTS)
    return {"dest3": per_step(min(DISPATCH_TILE, seq)), "dest3_combine": per_step(min(COMBINE_TILE, seq)),
            "tile_expert": plan[0, :n_tiles], "segment": plan[1, :n_tiles],
            "next_expert": plan[2, :n_tiles], "pad_start": plan[3, experts], "pad_len": plan[4, experts],
            "n_used": plan[5, :1]}


def kernel(x, c, positions, w_ada, b_ada, norm1_w, w_in, lambda_q1, lambda_k1, lambda_q2, lambda_k2, subln_w, hgrn_lb_logits, gnorm_w, w_out, norm2_w, w_group_router, b_group_router, w_expert_router, b_expert_router, w_gate, w_up, w_down, final_norm_w):
    b, s, d = x.shape
    n = b * s
    depth = w_ada.shape[0]
    assert depth == 1, "the combine kernel applies the final norm, so exactly one layer is supported"
    half = DIFF_HEAD_DIM // 2

    inv_freq = ROPE_THETA ** (-jnp.arange(half, dtype=F32) / half)
    pos = positions.astype(F32).reshape(n, 1)
    freq = jnp.tile(inv_freq, LANES // half).reshape(1, LANES)
    sign = jnp.tile(jnp.concatenate([-jnp.ones((half,), F32), jnp.ones((half,), F32)]),
                    LANES // DIFF_HEAD_DIM).reshape(1, LANES)

    xf = x.reshape(n, d)
    for l in range(depth):
        lambda_init = 0.8 - 0.6 * math.exp(-0.3 * l)
        mod4 = _mod(c, w_ada[l], b_ada[l]).reshape(b, 6, 1, d)
        q, k, v, hq, hf, hi, hg = _inproj(xf, mod4, norm1_w[l], w_in[l].astype(BF16), pos, freq, sign, s)
        to3 = lambda t: t.reshape(b, s, SEG)
        a = _attn(to3(q), to3(k), to3(v), lambda_q1[l], lambda_k1[l], lambda_q2[l], lambda_k2[l],
                  subln_w[l], lambda_init)
        r = _hgrn(to3(hq), to3(hf), to3(hi), to3(hg), hgrn_lb_logits, gnorm_w[l], l)

        w_router = jnp.zeros((d, LANES), F32)
        w_router = w_router.at[:, :N_GROUPS].set(w_group_router[l])
        w_router = w_router.at[:, ROUTER_LANE0:ROUTER_LANE0 + N_EXPERTS].set(w_expert_router[l])
        b_router = jnp.zeros((1, LANES), F32)
        b_router = b_router.at[0, :N_GROUPS].set(b_group_router[l])
        b_router = b_router.at[0, ROUTER_LANE0:ROUTER_LANE0 + N_EXPERTS].set(b_expert_router[l].reshape(-1))
        wr_hi = w_router.astype(BF16)
        wr_split = jnp.concatenate([wr_hi, (w_router - wr_hi.astype(F32)).astype(BF16)], axis=1)
        x1, h2, route, counts = _outproj(a.reshape(n, SEG), r.reshape(n, SEG), xf, mod4, norm2_w[l],
                                         w_out[l].astype(BF16), wr_split, b_router, s)

        plan = _plan(route, counts, s)
        xs = _dispatch(plan, h2)
        e_w = lambda w: w.reshape((N_EXPERTS,) + w.shape[2:])
        ys = _experts(plan, xs, e_w(w_gate[l]), e_w(w_up[l]), e_w(w_down[l]))
        xf = _combine(plan["dest3_combine"], ys, x1, route, mod4, final_norm_w, s)
    return xf.reshape(b, s, d)
```
